```python
import math
import jax, jax.numpy as jnp
from jax import lax
import numpy as np

D_MODEL = 2048
BATCH = 2
SEQ = 4096
DEPTH = 2

HEAD_DIM = 128
DIL_HEADS = 6
DIL_BRANCHES = ((128, 1), (512, 4), (2048, 16))
DIL_BLOCK = 64
DIFF_HEADS = 4
DIFF_QK_DIM = HEAD_DIM // 2
DIFF_BLOCK = 128
SWA_Q_HEADS = 6
SWA_KV_HEADS = 2
SWA_RADIUS = 128
SWA_BLOCK = 128
D_FF = 5504

N_ALIBI_HEADS = SWA_Q_HEADS + DIL_HEADS + DIFF_HEADS
MIX_WIDTH = (DIL_HEADS + DIFF_HEADS + SWA_Q_HEADS) * HEAD_DIM
A_QKV = DIL_HEADS * HEAD_DIM
B_QK = DIFF_HEADS * 2 * DIFF_QK_DIM
B_V = DIFF_HEADS * HEAD_DIM
C_Q = SWA_Q_HEADS * HEAD_DIM
C_KV = SWA_KV_HEADS * HEAD_DIM
IN_SPLITS = (A_QKV, A_QKV, A_QKV, B_QK, B_QK, B_V, C_Q, C_KV, C_KV)
IN_WIDTH = sum(IN_SPLITS)
RMS_EPS = 1e-6
NEG = -1e30

kernel_name = "hybrid_dilated_diff_swa_macaron_encoder"


def rms_norm(x, g):
    xf = x.astype(jnp.float32)
    y = xf * lax.rsqrt(jnp.mean(xf * xf, axis=-1, keepdims=True) + RMS_EPS)
    return (y * g.astype(jnp.float32)).astype(x.dtype)


def swiglu(h, w_gate, w_up, w_down):
    return (jax.nn.silu(h @ w_gate) * (h @ w_up)) @ w_down


def alibi_slopes():
    n = N_ALIBI_HEADS
    return jnp.asarray(2.0 ** (-8.0 * np.arange(1, n + 1) / n), dtype=jnp.float32)


def banded_attention(q, k, v, radius, blk, slopes, dist_scale, sink=None):
    B, L, H, Dh = q.shape
    G = k.shape[2]
    R = H // G
    nb = -(-L // blk)
    Lp = nb * blk
    pad = Lp - L
    qb = jnp.pad(q, ((0, 0), (0, pad), (0, 0), (0, 0))).reshape(B, nb, blk, G, R, Dh)
    kp = jnp.pad(k, ((0, 0), (blk, pad + blk), (0, 0), (0, 0))).reshape(B, nb + 2, blk, G, Dh)
    vp = jnp.pad(v, ((0, 0), (blk, pad + blk), (0, 0), (0, 0))).reshape(B, nb + 2, blk, G, Dh)
    kb = jnp.concatenate([kp[:, :-2], kp[:, 1:-1], kp[:, 2:]], axis=2)
    vb = jnp.concatenate([vp[:, :-2], vp[:, 1:-1], vp[:, 2:]], axis=2)
    s = jnp.einsum('bnqgrd,bnkgd->bngrqk', qb, kb,
                   preferred_element_type=jnp.float32) * (Dh ** -0.5)
    a = jnp.arange(blk)
    c = jnp.arange(3 * blk)
    rel = c[None, :] - blk - a[:, None]
    kpos = jnp.arange(nb)[:, None] * blk + c[None, :] - blk
    valid = (jnp.abs(rel) <= radius)[None] & ((kpos >= 0) & (kpos < L))[:, None, :]
    bias = slopes.reshape(G, R)[:, :, None, None] * (dist_scale * jnp.abs(rel).astype(jnp.float32))
    s = jnp.where(valid[None, :, None, None], s - bias[None, None], NEG)
    m = jnp.max(s, axis=-1)
    if sink is not None:
        sink_b = sink.astype(jnp.float32).reshape(G, R)[None, None, :, :, None]
        m = jnp.maximum(m, sink_b)
    p = jnp.exp(s - m[..., None])
    den = jnp.sum(p, axis=-1)
    if sink is not None:
        den = den + jnp.exp(sink_b - m)
    o = jnp.einsum('bngrqk,bnkgd->bnqgrd', p.astype(v.dtype), vb,
                   preferred_element_type=jnp.float32)
    o = o / den.transpose(0, 1, 4, 2, 3)[..., None]
    lse = (m + jnp.log(den)).transpose(0, 1, 4, 2, 3)
    return o.reshape(B, Lp, H, Dh)[:, :L], lse.reshape(B, Lp, H)[:, :L]


def dilated_attention(q, k, v, slopes):
    B, S, H, Dh = q.shape
    outs, lses = [], []
    for window, dil in DIL_BRANCHES:
        L = S // dil

        def to_sub(t):
            return t.reshape(B, L, dil, H, Dh).transpose(0, 2, 1, 3, 4).reshape(B * dil, L, H, Dh)

        o, lse = banded_attention(to_sub(q), to_sub(k), to_sub(v), window // (2 * dil),
                                  DIL_BLOCK, slopes, float(dil))
        outs.append(o.reshape(B, dil, L, H, Dh).transpose(0, 2, 1, 3, 4).reshape(B, S, H, Dh))
        lses.append(lse.reshape(B, dil, L, H).transpose(0, 2, 1, 3).reshape(B, S, H))
    w = jax.nn.softmax(jnp.stack(lses, axis=0), axis=0)
    return jnp.einsum('ibsh,ibshd->bshd', w, jnp.stack(outs, axis=0))


def diff_attention(q, k, v, lam, slopes):
    B, S, H, _, Dq = q.shape
    Dv = v.shape[-1]
    nblk = S // DIFF_BLOCK
    qb = q.reshape(B, nblk, DIFF_BLOCK, H, 2, Dq).transpose(1, 0, 2, 3, 4, 5)
    kpos = jnp.arange(S)

    def block(args):
        qi, i = args
        s = jnp.einsum('bqhcd,bkhcd->bhcqk', qi, k,
                       preferred_element_type=jnp.float32) * (Dq ** -0.5)
        qpos = i * DIFF_BLOCK + jnp.arange(DIFF_BLOCK)
        dist = jnp.abs(qpos[:, None] - kpos[None, :]).astype(jnp.float32)
        s = s - slopes[None, :, None, None, None] * dist
        p = jax.nn.softmax(s, axis=-1)
        a = p[:, :, 0] - lam * p[:, :, 1]
        return jnp.einsum('bhqk,bkhd->bqhd', a.astype(v.dtype), v,
                          preferred_element_type=jnp.float32)

    o = lax.map(block, (qb, jnp.arange(nblk)))
    return o.transpose(1, 0, 2, 3, 4).reshape(B, S, H, Dv)


def mixing(h, w_in, w_out, lq1, lk1, lq2, lk2, subln_g, sink, lambda_init):
    B, S, _ = h.shape
    slopes = alibi_slopes()
    sl_c = slopes[:SWA_Q_HEADS]
    sl_a = slopes[SWA_Q_HEADS:SWA_Q_HEADS + DIL_HEADS]
    sl_b = slopes[SWA_Q_HEADS + DIL_HEADS:]
    idx = [int(i) for i in np.cumsum(IN_SPLITS)[:-1]]
    aq, ak, av, bq, bk, bv, cq, ck, cv = jnp.split(h @ w_in, idx, axis=-1)
    hd = lambda t, n: t.reshape(B, S, n, HEAD_DIM)
    out_a = dilated_attention(hd(aq, DIL_HEADS), hd(ak, DIL_HEADS), hd(av, DIL_HEADS), sl_a)
    lam = (jnp.exp(jnp.sum(lq1.astype(jnp.float32) * lk1.astype(jnp.float32)))
           - jnp.exp(jnp.sum(lq2.astype(jnp.float32) * lk2.astype(jnp.float32))) + lambda_init)
    out_b = diff_attention(bq.reshape(B, S, DIFF_HEADS, 2, DIFF_QK_DIM),
                           bk.reshape(B, S, DIFF_HEADS, 2, DIFF_QK_DIM),
                           hd(bv, DIFF_HEADS), lam, sl_b)
    out_b = rms_norm(out_b, subln_g) * (1.0 - lambda_init)
    out_c, _ = banded_attention(hd(cq, SWA_Q_HEADS), hd(ck, SWA_KV_HEADS), hd(cv, SWA_KV_HEADS),
                                SWA_RADIUS, SWA_BLOCK, sl_c, 1.0, sink)
    y = jnp.concatenate([out_a.reshape(B, S, -1), out_b.reshape(B, S, -1),
                         out_c.reshape(B, S, -1)], axis=-1).astype(h.dtype)
    return y @ w_out


def setup_inputs(seed: int = 0) -> dict:
    key = jax.random.key(seed)
    ks = jax.random.split(key, 20)
    f32 = jnp.float32

    def nrm(k, shape, scale):
        return jax.random.normal(k, shape, f32) * scale

    def gain(k, shape):
        return 1.0 + 0.05 * jax.random.normal(k, shape, f32)

    return {
        "x": nrm(ks[0], (BATCH, SEQ, D_MODEL), 1.0),
        "ffn1_norm": gain(ks[1], (DEPTH, D_MODEL)),
        "ffn1_w_gate": nrm(ks[2], (DEPTH, D_MODEL, D_FF), D_MODEL ** -0.5),
        "ffn1_w_up": nrm(ks[3], (DEPTH, D_MODEL, D_FF), D_MODEL ** -0.5),
        "ffn1_w_down": nrm(ks[4], (DEPTH, D_FF, D_MODEL), D_FF ** -0.5),
        "mix_norm": gain(ks[5], (DEPTH, D_MODEL)),
        "w_in": nrm(ks[6], (DEPTH, D_MODEL, IN_WIDTH), D_MODEL ** -0.5),
        "w_out": nrm(ks[7], (DEPTH, MIX_WIDTH, D_MODEL), MIX_WIDTH ** -0.5),
        "diff_lambda_q1": nrm(ks[8], (DEPTH, DIFF_QK_DIM), 0.1),
        "diff_lambda_k1": nrm(ks[9], (DEPTH, DIFF_QK_DIM), 0.1),
        "diff_lambda_q2": nrm(ks[10], (DEPTH, DIFF_QK_DIM), 0.1),
        "diff_lambda_k2": nrm(ks[11], (DEPTH, DIFF_QK_DIM), 0.1),
        "diff_subln": gain(ks[12], (DEPTH, HEAD_DIM)),
        "swa_sink": nrm(ks[13], (DEPTH, SWA_Q_HEADS), 1.0),
        "ffn2_norm": gain(ks[14], (DEPTH, D_MODEL)),
        "ffn2_w_gate": nrm(ks[15], (DEPTH, D_MODEL, D_FF), D_MODEL ** -0.5),
        "ffn2_w_up": nrm(ks[16], (DEPTH, D_MODEL, D_FF), D_MODEL ** -0.5),
        "ffn2_w_down": nrm(ks[17], (DEPTH, D_FF, D_MODEL), D_FF ** -0.5),
        "final_norm": gain(ks[18], (D_MODEL,)),
    }


def reference(x, ffn1_norm, ffn1_w_gate, ffn1_w_up, ffn1_w_down, mix_norm, w_in, w_out,
              diff_lambda_q1, diff_lambda_k1, diff_lambda_q2, diff_lambda_k2, diff_subln,
              swa_sink, ffn2_norm, ffn2_w_gate, ffn2_w_up, ffn2_w_down, final_norm):
    for l in range(DEPTH):
        lambda_init = 0.8 - 0.6 * math.exp(-0.3 * l)
        h = rms_norm(x, ffn1_norm[l])
        x = x + 0.5 * swiglu(h, ffn1_w_gate[l], ffn1_w_up[l], ffn1_w_down[l])
        h = rms_norm(x, mix_norm[l])
        x = x + mixing(h, w_in[l], w_out[l], diff_lambda_q1[l], diff_lambda_k1[l],
                       diff_lambda_q2[l], diff_lambda_k2[l], diff_subln[l], swa_sink[l],
                       lambda_init)
        h = rms_norm(x, ffn2_norm[l])
        x = x + 0.5 * swiglu(h, ffn2_w_gate[l], ffn2_w_up[l], ffn2_w_down[l])
    return rms_norm(x, final_norm)
```

```python
import functools
import math

import numpy as np
import jax
import jax.numpy as jnp
from jax import lax
from jax.experimental import pallas as pl
from jax.experimental.pallas import tpu as pltpu

F32 = jnp.float32
BF16 = jnp.bfloat16

HEAD_DIM = 128
DIL_HEADS = 6
DIL_BRANCHES = ((128, 1), (512, 4), (2048, 16))
DIL_RADIUS = 64
DIFF_HEADS = 4
DIFF_QK_DIM = HEAD_DIM // 2
SWA_Q_HEADS = 6
SWA_KV_HEADS = 2
SWA_GROUP = SWA_Q_HEADS // SWA_KV_HEADS
SWA_RADIUS = 128
N_ALIBI_HEADS = SWA_Q_HEADS + DIL_HEADS + DIFF_HEADS
RMS_EPS = 1e-6
NEG = -1e30
ATTN_SCALE = HEAD_DIM ** -0.5

SLAB_AQ, SLAB_AK, SLAB_AV = 0, 6, 12
SLAB_BQ, SLAB_BK, SLAB_BV = 18, 22, 26
SLAB_CQ, SLAB_CK, SLAB_CV = 30, 36, 38
N_SLABS = 40

LANES = 128
VMEM_LIMIT_BYTES = 60000 * 1024

_SLOPES = [float(np.float32(2.0 ** (-8.0 * (n + 1) / N_ALIBI_HEADS))) for n in range(N_ALIBI_HEADS)]
SLOPES_C = _SLOPES[:SWA_Q_HEADS]
SLOPES_A = _SLOPES[SWA_Q_HEADS:SWA_Q_HEADS + DIL_HEADS]
SLOPES_B = _SLOPES[SWA_Q_HEADS + DIL_HEADS:]


def _params(n_grid):
    return pltpu.CompilerParams(dimension_semantics=("arbitrary",) * n_grid,
                                vmem_limit_bytes=VMEM_LIMIT_BYTES)


def _rms(x, g):
    ms = jnp.mean(x * x, axis=-1, keepdims=True)
    return x * lax.rsqrt(ms + RMS_EPS) * g


def _dot(a, b):
    return jnp.dot(a, b, preferred_element_type=F32)


def _dot_nt(a, b):
    return lax.dot_general(a, b, (((1,), (1,)), ((), ())), preferred_element_type=F32)


def _norm_kernel(x_ref, g_ref, o_ref):
    o_ref[...] = _rms(x_ref[...], g_ref[...]).astype(BF16)


def _norm_call(x, gains, layer):
    T, D = x.shape
    tm = 512
    return pl.pallas_call(
        _norm_kernel,
        grid=(T // tm,),
        in_specs=[pl.BlockSpec((tm, D), lambda i: (i, 0)),
                  pl.BlockSpec((None, 1, D), lambda i: (layer, 0, 0))],
        out_specs=pl.BlockSpec((tm, D), lambda i: (i, 0)),
        out_shape=jax.ShapeDtypeStruct((T, D), BF16),
        compiler_params=_params(1),
        name="rms_norm_bf16",
    )(x, gains)


def _ffn_kernel(h_ref, x_ref, wg_ref, wu_ref, wd_ref, wgt_ref, wut_ref, wdt_ref, gn_ref,
                *out_refs, n_full, n_chunk, rows, final):
    o_ref = out_refs[0]
    j = pl.program_id(1)

    @pl.when(j == 0)
    def _():
        o_ref[...] = jnp.zeros_like(o_ref)

    @pl.when(j < n_chunk)
    def _():
        r0 = pl.multiple_of(j * rows, rows)
        o_ref[pl.ds(r0, rows), :] += x_ref[...]

    def step(wg, wu, wd):
        h = h_ref[...]
        g = _dot(h, wg.astype(BF16))
        u = _dot(h, wu.astype(BF16))
        act = (0.5 * (g * jax.nn.sigmoid(g)) * u).astype(BF16)
        o_ref[...] += _dot(act, wd.astype(BF16))

    @pl.when(j < n_full)
    def _():
        step(wg_ref[...], wu_ref[...], wd_ref[...])

    @pl.when(j == n_full)
    def _():
        step(wgt_ref[...], wut_ref[...], wdt_ref[...])
        xn = o_ref[...]
        if final:
            o_ref[...] = _rms(xn, gn_ref[...])
        else:
            out_refs[1][...] = _rms(xn, gn_ref[...]).astype(BF16)


def _ffn_call(h, x, w_gate, w_up, w_down, layer, gains, gain_layer, final):
    T, D = x.shape
    d_ff = w_gate.shape[-1]
    tm, tf, tail = 1024, 256, 128
    n_full = d_ff // tf
    assert n_full * tf + tail == d_ff
    tail_blk = (n_full * tf) // tail
    n_chunk = 8
    rows = tm // n_chunk
    last = n_full - 1
    kern = functools.partial(_ffn_kernel, n_full=n_full, n_chunk=n_chunk, rows=rows, final=final)
    once = pl.Buffered(1)
    in_specs = [
        pl.BlockSpec((tm, D), lambda i, j: (i, 0)),
        pl.BlockSpec((rows, D), lambda i, j: (i * n_chunk + jnp.minimum(j, n_chunk - 1), 0)),
        pl.BlockSpec((None, D, tf), lambda i, j: (layer, 0, jnp.minimum(j, last))),
        pl.BlockSpec((None, D, tf), lambda i, j: (layer, 0, jnp.minimum(j, last))),
        pl.BlockSpec((None, tf, D), lambda i, j: (layer, jnp.minimum(j, last), 0)),
        pl.BlockSpec((None, D, tail), lambda i, j: (layer, 0, tail_blk), pipeline_mode=once),
        pl.BlockSpec((None, D, tail), lambda i, j: (layer, 0, tail_blk), pipeline_mode=once),
        pl.BlockSpec((None, tail, D), lambda i, j: (layer, tail_blk, 0), pipeline_mode=once),
        pl.BlockSpec((None, 1, D), lambda i, j: (gain_layer, 0, 0)),
    ]
    row_spec = pl.BlockSpec((tm, D), lambda i, j: (i, 0))
    if final:
        out_specs = row_spec
        out_shape = jax.ShapeDtypeStruct((T, D), F32)
    else:
        out_specs = [row_spec, row_spec]
        out_shape = [jax.ShapeDtypeStruct((T, D), F32), jax.ShapeDtypeStruct((T, D), BF16)]
    return pl.pallas_call(
        kern,
        grid=(T // tm, n_full + 1),
        in_specs=in_specs,
        out_specs=out_specs,
        out_shape=out_shape,
        compiler_params=_params(2),
        name="swiglu_half_step",
    )(h, x, w_gate, w_up, w_down, w_gate, w_up, w_down, gains)


def _inproj_kernel(h_ref, w_ref, o_ref):
    r = _dot(h_ref[...], w_ref[...].astype(BF16))
    o_ref[0] = r[:, :LANES].astype(BF16)
    o_ref[1] = r[:, LANES:].astype(BF16)


def _inproj_call(h, w_in, layer):
    T, D = h.shape
    tm, tn = 1024, 2 * LANES
    n_blk = w_in.shape[-1] // tn
    return pl.pallas_call(
        _inproj_kernel,
        grid=(T // tm, n_blk),
        in_specs=[pl.BlockSpec((tm, D), lambda i, j: (i, 0)),
                  pl.BlockSpec((None, D, tn), lambda i, j: (layer, 0, j))],
        out_specs=pl.BlockSpec((2, tm, LANES), lambda i, j: (j, i, 0)),
        out_shape=jax.ShapeDtypeStruct((2 * n_blk, T, LANES), BF16),
        compiler_params=_params(2),
        name="input_projection",
    )(h, w_in)


def _dil_kernel(*refs, dil, tq, sub_len, win, first, last):
    q_ref, k_ref, v_ref = refs[:3]
    pos = 3
    if not first:
        oin_ref, lin_ref = refs[pos:pos + 2]
        pos += 2
    if last:
        y_ref = refs[pos]
    else:
        oout_ref, lout_ref = refs[pos:pos + 2]

    i = pl.program_id(2)
    q0 = i * tq
    start = pl.multiple_of(jnp.clip(q0 - DIL_RADIUS, 0, sub_len - win), DIL_RADIUS)
    row = lax.broadcasted_iota(jnp.int32, (tq, win), 0)
    col = lax.broadcasted_iota(jnp.int32, (tq, win), 1)
    absrel = jnp.abs((start - q0) + (col - row))
    valid = absrel <= DIL_RADIUS
    dist = absrel.astype(F32) * float(dil)
    lane = lax.broadcasted_iota(jnp.int32, (tq, LANES), 1)
    lse_tile = jnp.zeros((tq, LANES), F32)

    for hh in range(DIL_HEADS):
        q = q_ref[hh]
        k = k_ref[hh, pl.ds(start, win), :]
        v = v_ref[hh, pl.ds(start, win), :]
        s = _dot_nt(q, k) * ATTN_SCALE
        s = jnp.where(valid, s - SLOPES_A[hh] * dist, NEG)
        m = jnp.max(s, axis=-1, keepdims=True)
        p = jnp.exp(s - m)
        den = jnp.sum(p, axis=-1, keepdims=True)
        o = _dot(p.astype(BF16), v) / den
        lse = m + jnp.log(den)
        if not first:
            lse_prev = lin_ref[:, hh:hh + 1]
            mx = jnp.maximum(lse_prev, lse)
            e_prev = jnp.exp(lse_prev - mx)
            e_cur = jnp.exp(lse - mx)
            tot = e_prev + e_cur
            o = (e_prev * oin_ref[hh] + e_cur * o) / tot
            lse = mx + jnp.log(tot)
        if last:
            y_ref[hh] = o.astype(BF16)
        else:
            oout_ref[hh] = o
            lse_tile = jnp.where(lane == hh, lse, lse_tile)
    if not last:
        lout_ref[...] = lse_tile


def _dil_call(qkv, state, dil, batch, seq, first, last):
    T = batch * seq
    sub_len = seq // dil
    tq = 128
    win = min(tq + 2 * DIL_RADIUS, sub_len)
    qkv_v = qkv.reshape(N_SLABS, batch, sub_len, dil * LANES)
    blk_q = pl.BlockSpec((DIL_HEADS, None, tq, LANES), lambda b, r, i: (SLAB_AQ // DIL_HEADS, b, i, r))
    blk_k = pl.BlockSpec((DIL_HEADS, None, sub_len, LANES), lambda b, r, i: (SLAB_AK // DIL_HEADS, b, 0, r))
    blk_v = pl.BlockSpec((DIL_HEADS, None, sub_len, LANES), lambda b, r, i: (SLAB_AV // DIL_HEADS, b, 0, r))
    blk_o = pl.BlockSpec((DIL_HEADS, None, tq, LANES), lambda b, r, i: (0, b, i, r))
    blk_l = pl.BlockSpec((None, tq, LANES), lambda b, r, i: (b, i, r))
    args = [qkv_v, qkv_v, qkv_v]
    in_specs = [blk_q, blk_k, blk_v]
    if not first:
        o_prev, l_prev = state
        args += [o_prev.reshape(DIL_HEADS, batch, sub_len, dil * LANES),
                 l_prev.reshape(batch, sub_len, dil * LANES)]
        in_specs += [blk_o, blk_l]
    if last:
        out_specs = blk_o
        out_shape = jax.ShapeDtypeStruct((DIL_HEADS, batch, sub_len, dil * LANES), BF16)
    else:
        out_specs = [blk_o, blk_l]
        out_shape = [jax.ShapeDtypeStruct((DIL_HEADS, batch, sub_len, dil * LANES), F32),
                     jax.ShapeDtypeStruct((batch, sub_len, dil * LANES), F32)]
    kern = functools.partial(_dil_kernel, dil=dil, tq=tq, sub_len=sub_len, win=win,
                             first=first, last=last)
    out = pl.pallas_call(
        kern,
        grid=(batch, dil, sub_len // tq),
        in_specs=in_specs,
        out_specs=out_specs,
        out_shape=out_shape,
        compiler_params=_params(3),
        name="dilated_attention_d%d" % dil,
    )(*args)
    if last:
        return out.reshape(DIL_HEADS, T, LANES)
    o_new, l_new = out
    return o_new.reshape(DIL_HEADS, batch, seq, LANES), l_new.reshape(batch, seq, LANES)


def _dilated_attention(qkv, batch, seq):
    state = None
    n = len(DIL_BRANCHES)
    for idx, (_, dil) in enumerate(DIL_BRANCHES):
        state = _dil_call(qkv, state, dil, batch, seq, first=idx == 0, last=idx == n - 1)
    return state


def _diff_kernel(slopes_ref, q_ref, k_ref, v_ref, lq1_ref, lk1_ref, lq2_ref, lk2_ref, sg_ref,
                 y_ref, *, tq, tk, seq, lambda_init):
    h = pl.program_id(1)
    i = pl.program_id(2)
    slope = slopes_ref[h]
    lam = (jnp.exp(jnp.sum(lq1_ref[...] * lk1_ref[...], axis=-1, keepdims=True))
           - jnp.exp(jnp.sum(lq2_ref[...] * lk2_ref[...], axis=-1, keepdims=True))
           + lambda_init)

    q = q_ref[...].astype(F32) * (DIFF_QK_DIM ** -0.5)
    lane = lax.broadcasted_iota(jnp.int32, (tq, LANES), 1)
    q_maps = (jnp.where(lane < DIFF_QK_DIM, q, 0.0).astype(BF16),
              jnp.where(lane >= DIFF_QK_DIM, q, 0.0).astype(BF16))
    row = lax.broadcasted_iota(jnp.int32, (tq, tk), 0)
    col = lax.broadcasted_iota(jnp.int32, (tq, tk), 1)
    base = row - col + i * tq

    def body(kb, carry):
        k0 = pl.multiple_of(kb * tk, tk)
        k = k_ref[pl.ds(k0, tk), :]
        v = v_ref[pl.ds(k0, tk), :]
        bias = slope * jnp.abs(base - k0).astype(F32)
        new = []
        for c in range(2):
            m, l, acc = carry[3 * c:3 * c + 3]
            s = _dot_nt(q_maps[c], k) - bias
            m_new = jnp.maximum(m, jnp.max(s, axis=-1, keepdims=True))
            alpha = jnp.exp(m - m_new)
            p = jnp.exp(s - m_new)
            l = alpha * l + jnp.sum(p, axis=-1, keepdims=True)
            acc = alpha * acc + _dot(p.astype(BF16), v)
            new += [m_new, l, acc]
        return tuple(new)

    init = (jnp.full((tq, 1), NEG, F32), jnp.zeros((tq, 1), F32), jnp.zeros((tq, HEAD_DIM), F32)) * 2
    m0, l0, a0, m1, l1, a1 = lax.fori_loop(0, seq // tk, body, init)
    o = a0 / l0 - lam * (a1 / l1)
    y_ref[...] = (_rms(o, sg_ref[...]) * (1.0 - lambda_init)).astype(BF16)


def _diff_call(qkv, slopes_b, lq1, lk1, lq2, lk2, subln, layer, lambda_init, batch, seq):
    T = batch * seq
    tq, tk = 256, 512
    qkv_v = qkv.reshape(N_SLABS, batch, seq, LANES)
    vec = lambda a: a.reshape(a.shape[0], 1, a.shape[1])
    lam_spec = pl.BlockSpec((None, 1, DIFF_QK_DIM), lambda b, h, i: (layer, 0, 0))
    kern = functools.partial(_diff_kernel, tq=tq, tk=tk, seq=seq, lambda_init=lambda_init)
    out = pl.pallas_call(
        kern,
        grid=(batch, DIFF_HEADS, seq // tq),
        in_specs=[
            pl.BlockSpec(memory_space=pltpu.SMEM),
            pl.BlockSpec((None, None, tq, LANES), lambda b, h, i: (SLAB_BQ + h, b, i, 0)),
            pl.BlockSpec((None, None, seq, LANES), lambda b, h, i: (SLAB_BK + h, b, 0, 0)),
            pl.BlockSpec((None, None, seq, LANES), lambda b, h, i: (SLAB_BV + h, b, 0, 0)),
            lam_spec, lam_spec, lam_spec, lam_spec,
            pl.BlockSpec((None, 1, HEAD_DIM), lambda b, h, i: (layer, 0, 0)),
        ],
        out_specs=pl.BlockSpec((None, None, tq, LANES), lambda b, h, i: (h, b, i, 0)),
        out_shape=jax.ShapeDtypeStruct((DIFF_HEADS, batch, seq, LANES), BF16),
        compiler_params=_params(3),
        name="differential_attention",
    )(slopes_b, qkv_v, qkv_v, qkv_v, vec(lq1), vec(lk1), vec(lq2), vec(lk2), vec(subln))
    return out.reshape(DIFF_HEADS, T, LANES)


def _swa_kernel(slopes_ref, sink_ref, q_ref, k_ref, v_ref, y_ref, *, tq, win, seq, layer):
    g = pl.program_id(1)
    i = pl.program_id(2)
    q0 = i * tq
    start = pl.multiple_of(jnp.clip(q0 - SWA_RADIUS, 0, seq - win), SWA_RADIUS)
    row = lax.broadcasted_iota(jnp.int32, (tq, win), 0)
    col = lax.broadcasted_iota(jnp.int32, (tq, win), 1)
    absrel = jnp.abs((start - q0) + (col - row))
    valid = absrel <= SWA_RADIUS
    dist = absrel.astype(F32)
    k = k_ref[pl.ds(start, win), :]
    v = v_ref[pl.ds(start, win), :]
    for r in range(SWA_GROUP):
        head = g * SWA_GROUP + r
        slope = slopes_ref[head]
        sink = sink_ref[layer * SWA_Q_HEADS + head]
        s = _dot_nt(q_ref[r], k) * ATTN_SCALE
        s = jnp.where(valid, s - slope * dist, NEG)
        m = jnp.maximum(jnp.max(s, axis=-1, keepdims=True), sink)
        p = jnp.exp(s - m)
        den = jnp.sum(p, axis=-1, keepdims=True) + jnp.exp(sink - m)
        y_ref[r] = (_dot(p.astype(BF16), v) / den).astype(BF16)


def _swa_call(qkv, slopes_c, sink_flat, layer, batch, seq):
    T = batch * seq
    tq = 256
    win = tq + 2 * SWA_RADIUS
    qkv_v = qkv.reshape(N_SLABS, batch, seq, LANES)
    kern = functools.partial(_swa_kernel, tq=tq, win=win, seq=seq, layer=layer)
    out = pl.pallas_call(
        kern,
        grid=(batch, SWA_KV_HEADS, seq // tq),
        in_specs=[
            pl.BlockSpec(memory_space=pltpu.SMEM),
            pl.BlockSpec(memory_space=pltpu.SMEM),
            pl.BlockSpec((SWA_GROUP, None, tq, LANES), lambda b, g, i: (SLAB_CQ // SWA_GROUP + g, b, i, 0)),
            pl.BlockSpec((None, None, seq, LANES), lambda b, g, i: (SLAB_CK + g, b, 0, 0)),
            pl.BlockSpec((None, None, seq, LANES), lambda b, g, i: (SLAB_CV + g, b, 0, 0)),
        ],
        out_specs=pl.BlockSpec((SWA_GROUP, None, tq, LANES), lambda b, g, i: (g, b, i, 0)),
        out_shape=jax.ShapeDtypeStruct((SWA_Q_HEADS, batch, seq, LANES), BF16),
        compiler_params=_params(3),
        name="window_gqa_attention",
    )(slopes_c, sink_flat, qkv_v, qkv_v, qkv_v)
    return out.reshape(SWA_Q_HEADS, T, LANES)


def _outproj_kernel(ya_ref, yb_ref, yc_ref, x_ref, w_ref, gn_ref, o_ref, hn_ref, *, n_a, n_b, n_k, rows):
    k = pl.program_id(1)

    @pl.when(k == 0)
    def _():
        o_ref[...] = jnp.zeros_like(o_ref)

    r0 = pl.multiple_of(k * rows, rows)
    o_ref[pl.ds(r0, rows), :] += x_ref[...]

    def accumulate(y_ref):
        y = jnp.concatenate([y_ref[0], y_ref[1]], axis=-1)
        o_ref[...] += _dot(y, w_ref[...].astype(BF16))

    @pl.when(k < n_a)
    def _():
        accumulate(ya_ref)

    @pl.when((k >= n_a) & (k < n_a + n_b))
    def _():
        accumulate(yb_ref)

    @pl.when(k >= n_a + n_b)
    def _():
        accumulate(yc_ref)

    @pl.when(k == n_k - 1)
    def _():
        hn_ref[...] = _rms(o_ref[...], gn_ref[...]).astype(BF16)


def _outproj_call(ya, yb, yc, x, w_out, layer, gains, gain_layer):
    T, D = x.shape
    tm = 1024
    n_a, n_b, n_c = ya.shape[0] // 2, yb.shape[0] // 2, yc.shape[0] // 2
    n_k = n_a + n_b + n_c
    rows = tm // n_k
    kern = functools.partial(_outproj_kernel, n_a=n_a, n_b=n_b, n_k=n_k, rows=rows)
    row_spec = pl.BlockSpec((tm, D), lambda i, k: (i, 0))
    return pl.pallas_call(
        kern,
        grid=(T // tm, n_k),
        in_specs=[
            pl.BlockSpec((2, tm, LANES), lambda i, k: (jnp.minimum(k, n_a - 1), i, 0)),
            pl.BlockSpec((2, tm, LANES), lambda i, k: (jnp.clip(k - n_a, 0, n_b - 1), i, 0)),
            pl.BlockSpec((2, tm, LANES), lambda i, k: (jnp.clip(k - n_a - n_b, 0, n_c - 1), i, 0)),
            pl.BlockSpec((rows, D), lambda i, k: (i * n_k + k, 0)),
            pl.BlockSpec((None, 2 * LANES, D), lambda i, k: (layer, k, 0)),
            pl.BlockSpec((None, 1, D), lambda i, k: (gain_layer, 0, 0)),
        ],
        out_specs=[row_spec, row_spec],
        out_shape=[jax.ShapeDtypeStruct((T, D), F32), jax.ShapeDtypeStruct((T, D), BF16)],
        compiler_params=_params(2),
        name="output_projection",
    )(ya, yb, yc, x, w_out, gains)


def kernel(x, ffn1_norm, ffn1_w_gate, ffn1_w_up, ffn1_w_down, mix_norm, w_in, w_out, diff_lambda_q1, diff_lambda_k1, diff_lambda_q2, diff_lambda_k2, diff_subln, swa_sink, ffn2_norm, ffn2_w_gate, ffn2_w_up, ffn2_w_down, final_norm):
    batch, seq, d_model = x.shape
    depth = w_in.shape[0]
    xf = x.reshape(batch * seq, d_model)
    gain3 = lambda a: a.reshape(a.shape[0], 1, a.shape[1])
    ffn1_g, mix_g, ffn2_g = gain3(ffn1_norm), gain3(mix_norm), gain3(ffn2_norm)
    final_g = final_norm.reshape(1, 1, d_model)
    slopes_b = jnp.asarray(SLOPES_B, F32)
    slopes_c = jnp.asarray(SLOPES_C, F32)
    sink_flat = swa_sink.astype(F32).reshape(-1)

    h = _norm_call(xf, ffn1_g, 0)
    for l in range(depth):
        lambda_init = 0.8 - 0.6 * math.exp(-0.3 * l)
        xf, h = _ffn_call(h, xf, ffn1_w_gate, ffn1_w_up, ffn1_w_down, l, mix_g, l, final=False)
        qkv = _inproj_call(h, w_in, l)
        ya = _dilated_attention(qkv, batch, seq)
        yb = _diff_call(qkv, slopes_b, diff_lambda_q1, diff_lambda_k1, diff_lambda_q2, diff_lambda_k2,
                        diff_subln, l, lambda_init, batch, seq)
        yc = _swa_call(qkv, slopes_c, sink_flat, l, batch, seq)
        xf, h = _outproj_call(ya, yb, yc, xf, w_out, l, ffn2_g, l)
        if l + 1 < depth:
            xf, h = _ffn_call(h, xf, ffn2_w_gate, ffn2_w_up, ffn2_w_down, l, ffn1_g, l + 1, final=False)
        else:
            xf = _ffn_call(h, xf, ffn2_w_gate, ffn2_w_up, ffn2_w_down, l, final_g, 0, final=True)
    return xf.reshape(batch, seq, d_model)
```

```python
import functools
import math

import numpy as np
import jax
import jax.numpy as jnp
from jax import lax
from jax.experimental import pallas as pl
from jax.experimental.pallas import tpu as pltpu

F32 = jnp.float32
BF16 = jnp.bfloat16

HEAD_DIM = 128
DIL_HEADS = 6
DIL_BRANCHES = ((128, 1), (512, 4), (2048, 16))
DIL_RADIUS = 64
DIFF_HEADS = 4
DIFF_QK_DIM = HEAD_DIM // 2
SWA_Q_HEADS = 6
SWA_KV_HEADS = 2
SWA_GROUP = SWA_Q_HEADS // SWA_KV_HEADS
SWA_RADIUS = 128
N_ALIBI_HEADS = SWA_Q_HEADS + DIL_HEADS + DIFF_HEADS
RMS_EPS = 1e-6
NEG = -1e30
ATTN_SCALE = HEAD_DIM ** -0.5

SLAB_AQ, SLAB_AK, SLAB_AV = 0, 6, 12
SLAB_BQ, SLAB_BK, SLAB_BV = 18, 22, 26
SLAB_CQ, SLAB_CK, SLAB_CV = 30, 36, 38
N_SLABS = 40

LANES = 128
VMEM_LIMIT_BYTES = 60000 * 1024

_SLOPES = [float(np.float32(2.0 ** (-8.0 * (n + 1) / N_ALIBI_HEADS))) for n in range(N_ALIBI_HEADS)]
SLOPES_C = _SLOPES[:SWA_Q_HEADS]
SLOPES_A = _SLOPES[SWA_Q_HEADS:SWA_Q_HEADS + DIL_HEADS]
SLOPES_B = _SLOPES[SWA_Q_HEADS + DIL_HEADS:]


def _params(n_grid):
    return pltpu.CompilerParams(dimension_semantics=("arbitrary",) * n_grid,
                                vmem_limit_bytes=VMEM_LIMIT_BYTES)


def _rms(x, g):
    ms = jnp.mean(x * x, axis=-1, keepdims=True)
    return x * lax.rsqrt(ms + RMS_EPS) * g


def _dot(a, b):
    return jnp.dot(a, b, preferred_element_type=F32)


def _dot_nt(a, b):
    return lax.dot_general(a, b, (((1,), (1,)), ((), ())), preferred_element_type=F32)


def _norm_kernel(x_ref, g_ref, o_ref):
    o_ref[...] = _rms(x_ref[...], g_ref[...]).astype(BF16)


def _norm_call(x, gains, layer):
    T, D = x.shape
    tm = 512
    return pl.pallas_call(
        _norm_kernel,
        grid=(T // tm,),
        in_specs=[pl.BlockSpec((tm, D), lambda i: (i, 0)),
                  pl.BlockSpec((None, 1, D), lambda i: (layer, 0, 0))],
        out_specs=pl.BlockSpec((tm, D), lambda i: (i, 0)),
        out_shape=jax.ShapeDtypeStruct((T, D), BF16),
        compiler_params=_params(1),
        name="rms_norm_bf16",
    )(x, gains)


def _ffn_kernel(h_ref, x_ref, wg_ref, wu_ref, wd_ref, wgt_ref, wut_ref, wdt_ref, gn_ref,
                *out_refs, n_full, n_chunk, rows, final):
    o_ref = out_refs[0]
    j = pl.program_id(1)

    @pl.when(j == 0)
    def _():
        o_ref[...] = jnp.zeros_like(o_ref)

    @pl.when(j < n_chunk)
    def _():
        r0 = pl.multiple_of(j * rows, rows)
        o_ref[pl.ds(r0, rows), :] += x_ref[...]

    def step(wg, wu, wd):
        h = h_ref[...]
        g = _dot(h, wg.astype(BF16))
        u = _dot(h, wu.astype(BF16))
        act = (0.5 * (g * jax.nn.sigmoid(g)) * u).astype(BF16)
        o_ref[...] += _dot(act, wd.astype(BF16))

    @pl.when(j < n_full)
    def _():
        step(wg_ref[...], wu_ref[...], wd_ref[...])

    @pl.when(j == n_full)
    def _():
        step(wgt_ref[...], wut_ref[...], wdt_ref[...])
        xn = o_ref[...]
        if final:
            o_ref[...] = _rms(xn, gn_ref[...])
        else:
            out_refs[1][...] = _rms(xn, gn_ref[...]).astype(BF16)


def _ffn_call(h, x, w_gate, w_up, w_down, layer, gains, gain_layer, final):
    T, D = x.shape
    d_ff = w_gate.shape[-1]
    tm, tf, tail = 1024, 256, 128
    n_full = d_ff // tf
    assert n_full * tf + tail == d_ff
    tail_blk = (n_full * tf) // tail
    n_chunk = 8
    rows = tm // n_chunk
    last = n_full - 1
    kern = functools.partial(_ffn_kernel, n_full=n_full, n_chunk=n_chunk, rows=rows, final=final)
    once = pl.Buffered(1)
    in_specs = [
        pl.BlockSpec((tm, D), lambda i, j: (i, 0)),
        pl.BlockSpec((rows, D), lambda i, j: (i * n_chunk + jnp.minimum(j, n_chunk - 1), 0)),
        pl.BlockSpec((None, D, tf), lambda i, j: (layer, 0, jnp.minimum(j, last))),
        pl.BlockSpec((None, D, tf), lambda i, j: (layer, 0, jnp.minimum(j, last))),
        pl.BlockSpec((None, tf, D), lambda i, j: (layer, jnp.minimum(j, last), 0)),
        pl.BlockSpec((None, D, tail), lambda i, j: (layer, 0, tail_blk), pipeline_mode=once),
        pl.BlockSpec((None, D, tail), lambda i, j: (layer, 0, tail_blk), pipeline_mode=once),
        pl.BlockSpec((None, tail, D), lambda i, j: (layer, tail_blk, 0), pipeline_mode=once),
        pl.BlockSpec((None, 1, D), lambda i, j: (gain_layer, 0, 0)),
    ]
    row_spec = pl.BlockSpec((tm, D), lambda i, j: (i, 0))
    if final:
        out_specs = row_spec
        out_shape = jax.ShapeDtypeStruct((T, D), F32)
    else:
        out_specs = [row_spec, row_spec]
        out_shape = [jax.ShapeDtypeStruct((T, D), F32), jax.ShapeDtypeStruct((T, D), BF16)]
    return pl.pallas_call(
        kern,
        grid=(T // tm, n_full + 1),
        in_specs=in_specs,
        out_specs=out_specs,
        out_shape=out_shape,
        compiler_params=_params(2),
        name="swiglu_half_step",
    )(h, x, w_gate, w_up, w_down, w_gate, w_up, w_down, gains)


def _inproj_kernel(h_ref, w_ref, o_ref):
    r = _dot(h_ref[...], w_ref[...].astype(BF16))
    o_ref[0] = r[:, :LANES].astype(BF16)
    o_ref[1] = r[:, LANES:].astype(BF16)


def _inproj_call(h, w_in, layer):
    T, D = h.shape
    tm, tn = 1024, 2 * LANES
    n_blk = w_in.shape[-1] // tn
    return pl.pallas_call(
        _inproj_kernel,
        grid=(T // tm, n_blk),
        in_specs=[pl.BlockSpec((tm, D), lambda i, j: (i, 0)),
                  pl.BlockSpec((None, D, tn), lambda i, j: (layer, 0, j))],
        out_specs=pl.BlockSpec((2, tm, LANES), lambda i, j: (j, i, 0)),
        out_shape=jax.ShapeDtypeStruct((2 * n_blk, T, LANES), BF16),
        compiler_params=_params(2),
        name="input_projection",
    )(h, w_in)


def _dil_kernel(slopes_ref, q_ref, k_ref, v_ref, y_ref, kf, vf, qf, o_acc, l_acc, *, tq, seq, blk):
    h = pl.program_id(1)
    i = pl.program_id(2)
    slope = slopes_ref[h]

    @pl.when(i == 0)
    def _():
        kf[...] = k_ref[...].astype(F32)
        vf[...] = v_ref[...].astype(F32)

    qf[...] = q_ref[...].astype(F32)

    def rows(start, n, dil):
        if dil > 1:
            return pl.ds(start, n, stride=dil)
        if isinstance(start, int):
            return pl.ds(start, n)
        return pl.ds(pl.multiple_of(start, DIL_RADIUS), n)

    def block_logits(dil, n_q, r, b, first):
        sub_len = seq // dil
        win = min(n_q + 2 * DIL_RADIUS, sub_len)
        local_row = b * (n_q * dil) + r
        u0 = i * (tq // dil) + b * n_q
        sub_start = jnp.clip(u0 - DIL_RADIUS, 0, sub_len - win)
        tok_start = sub_start * dil + r
        row = lax.broadcasted_iota(jnp.int32, (n_q, win), 0)
        col = lax.broadcasted_iota(jnp.int32, (n_q, win), 1)
        absrel = jnp.abs((sub_start - u0) + (col - row))
        q = qf[rows(local_row, n_q, dil), :].astype(BF16)
        k = kf[rows(tok_start, win, dil), :].astype(BF16)
        s = _dot_nt(q, k) * ATTN_SCALE
        s = jnp.where(absrel <= DIL_RADIUS, s - slope * (absrel.astype(F32) * float(dil)), NEG)
        return s, rows(tok_start, win, dil), rows(local_row, n_q, dil), n_q, first

    def block_finish(s, src, dst, n_q, first):
        m = jnp.max(s, axis=-1, keepdims=True)
        p = jnp.exp(s - m)
        den = jnp.sum(p, axis=-1, keepdims=True)
        o = _dot(p.astype(BF16), vf[src, :].astype(BF16)) / den
        lse = jnp.broadcast_to(m + jnp.log(den), (n_q, LANES))
        if not first:
            lse_prev = l_acc[dst, :]
            mx = jnp.maximum(lse_prev, lse)
            e_prev = jnp.exp(lse_prev - mx)
            e_cur = jnp.exp(lse - mx)
            tot = e_prev + e_cur
            o = (e_prev * o_acc[dst, :] + e_cur * o) / tot
            lse = mx + jnp.log(tot)
        o_acc[dst, :] = o
        l_acc[dst, :] = lse

    blocks = []
    for idx, (_, dil) in enumerate(DIL_BRANCHES):
        per_class = tq // dil
        n_q = min(blk, per_class)
        blocks += [(dil, n_q, r, b, idx == 0) for r in range(dil) for b in range(per_class // n_q)]

    pending = block_logits(*blocks[0])
    for nxt in blocks[1:]:
        ahead = block_logits(*nxt)
        block_finish(*pending)
        pending = ahead
    block_finish(*pending)

    y_ref[...] = o_acc[...].astype(BF16)


def _dilated_attention(qkv, slopes_a, batch, seq):
    T = batch * seq
    tq, blk = 1024, 128
    assert all(tq % (d * 8) == 0 for _, d in DIL_BRANCHES)
    qkv_v = qkv.reshape(N_SLABS, batch, seq, LANES)
    kern = functools.partial(_dil_kernel, tq=tq, seq=seq, blk=blk)
    out = pl.pallas_call(
        kern,
        grid=(batch, DIL_HEADS, seq // tq),
        in_specs=[
            pl.BlockSpec(memory_space=pltpu.SMEM),
            pl.BlockSpec((None, None, tq, LANES), lambda b, h, i: (SLAB_AQ + h, b, i, 0)),
            pl.BlockSpec((None, None, seq, LANES), lambda b, h, i: (SLAB_AK + h, b, 0, 0)),
            pl.BlockSpec((None, None, seq, LANES), lambda b, h, i: (SLAB_AV + h, b, 0, 0)),
        ],
        out_specs=pl.BlockSpec((None, None, tq, LANES), lambda b, h, i: (h, b, i, 0)),
        out_shape=jax.ShapeDtypeStruct((DIL_HEADS, batch, seq, LANES), BF16),
        scratch_shapes=[pltpu.VMEM((seq, LANES), F32), pltpu.VMEM((seq, LANES), F32),
                        pltpu.VMEM((tq, LANES), F32), pltpu.VMEM((tq, LANES), F32),
                        pltpu.VMEM((tq, LANES), F32)],
        compiler_params=_params(3),
        name="dilated_attention",
    )(slopes_a, qkv_v, qkv_v, qkv_v)
    return out.reshape(DIL_HEADS, T, LANES)


def _diff_kernel(slopes_ref, q_ref, k_ref, v_ref, lq1_ref, lk1_ref, lq2_ref, lk2_ref, sg_ref,
                 y_ref, vt_ref, u_ref, *, tq, tk, seq, lambda_init):
    h = pl.program_id(1)
    i = pl.program_id(2)
    slope = slopes_ref[h]
    n_kb = seq // tk

    @pl.when(i == 0)
    def _():
        for c in range(n_kb):
            vt_ref[c] = v_ref[c * tk:(c + 1) * tk, :].astype(F32).T.astype(BF16)

    krow = lax.broadcasted_iota(jnp.int32, (tk, tq), 0)
    qcol = lax.broadcasted_iota(jnp.int32, (tk, tq), 1)
    qk_off = qcol - krow
    u = slope * qk_off.astype(F32)
    u_ref[0] = -u
    u_ref[1] = u

    q = q_ref[...].astype(F32) * (DIFF_QK_DIM ** -0.5)
    lane = lax.broadcasted_iota(jnp.int32, (tq, LANES), 1)
    qt_maps = (jnp.where(lane < DIFF_QK_DIM, q, 0.0).T.astype(BF16),
               jnp.where(lane >= DIFF_QK_DIM, q, 0.0).T.astype(BF16))
    q0 = i * tq
    kb_diag = q0 // tk

    def logits(j):
        kb = kb_diag + j
        kb = jnp.where(kb >= n_kb, kb - n_kb, kb)
        k0 = pl.multiple_of(kb * tk, tk)
        k = k_ref[pl.ds(k0, tk), :]
        if j == 0:
            tile = -slope * jnp.abs(qk_off + (q0 - k0)).astype(F32)
            shift_c = 0.0
        else:
            tile = u_ref[(kb > kb_diag).astype(jnp.int32)]
            shift_c = -slope * jnp.abs(q0 - k0).astype(F32)
        return kb, shift_c, [_dot(k, qt) + tile for qt in qt_maps]

    def local_softmax(kb, shift_c, s_maps):
        vt = vt_ref[kb]
        out = []
        for s in s_maps:
            m = jnp.max(s, axis=0, keepdims=True)
            p = jnp.exp(s - m)
            l = jnp.sum(p, axis=0, keepdims=True)
            out.append((m + shift_c, l, _dot(vt, p.astype(BF16))))
        return out

    state = None
    nxt = logits(0)
    for j in range(n_kb):
        cur = nxt
        if j + 1 < n_kb:
            nxt = logits(j + 1)
        part = local_softmax(*cur)
        if state is None:
            state = part
            continue
        merged = []
        for (m, l, acc), (m_b, l_b, acc_b) in zip(state, part):
            m_new = jnp.maximum(m, m_b)
            alpha, beta = jnp.exp(m - m_new), jnp.exp(m_b - m_new)
            merged.append((m_new, alpha * l + beta * l_b, alpha * acc + beta * acc_b))
        state = merged
    (_, l0, a0), (_, l1, a1) = state

    lam = (jnp.exp(jnp.sum(lq1_ref[...] * lk1_ref[...], axis=-1, keepdims=True))
           - jnp.exp(jnp.sum(lq2_ref[...] * lk2_ref[...], axis=-1, keepdims=True))
           + lambda_init)
    o = (a0 / l0 - lam * (a1 / l1)).T
    y_ref[...] = (_rms(o, sg_ref[...]) * (1.0 - lambda_init)).astype(BF16)


def _diff_call(qkv, slopes_b, lq1, lk1, lq2, lk2, subln, layer, lambda_init, batch, seq):
    T = batch * seq
    tq, tk = 256, 512
    assert tk % tq == 0 and seq % tk == 0
    qkv_v = qkv.reshape(N_SLABS, batch, seq, LANES)
    vec = lambda a: a.reshape(a.shape[0], 1, a.shape[1])
    lam_spec = pl.BlockSpec((None, 1, DIFF_QK_DIM), lambda b, h, i: (layer, 0, 0))
    kern = functools.partial(_diff_kernel, tq=tq, tk=tk, seq=seq, lambda_init=lambda_init)
    out = pl.pallas_call(
        kern,
        grid=(batch, DIFF_HEADS, seq // tq),
        in_specs=[
            pl.BlockSpec(memory_space=pltpu.SMEM),
            pl.BlockSpec((None, None, tq, LANES), lambda b, h, i: (SLAB_BQ + h, b, i, 0)),
            pl.BlockSpec((None, None, seq, LANES), lambda b, h, i: (SLAB_BK + h, b, 0, 0)),
            pl.BlockSpec((None, None, seq, LANES), lambda b, h, i: (SLAB_BV + h, b, 0, 0)),
            lam_spec, lam_spec, lam_spec, lam_spec,
            pl.BlockSpec((None, 1, HEAD_DIM), lambda b, h, i: (layer, 0, 0)),
        ],
        out_specs=pl.BlockSpec((None, None, tq, LANES), lambda b, h, i: (h, b, i, 0)),
        out_shape=jax.ShapeDtypeStruct((DIFF_HEADS, batch, seq, LANES), BF16),
        scratch_shapes=[pltpu.VMEM((seq // tk, HEAD_DIM, tk), BF16), pltpu.VMEM((2, tk, tq), F32)],
        compiler_params=_params(3),
        name="differential_attention",
    )(slopes_b, qkv_v, qkv_v, qkv_v, vec(lq1), vec(lk1), vec(lq2), vec(lk2), vec(subln))
    return out.reshape(DIFF_HEADS, T, LANES)


def _swa_kernel(slopes_ref, sink_ref, q_ref, k_ref, v_ref, y_ref, *, tq, win, seq, layer):
    g = pl.program_id(1)
    i = pl.program_id(2)
    q0 = i * tq
    start = pl.multiple_of(jnp.clip(q0 - SWA_RADIUS, 0, seq - win), SWA_RADIUS)
    row = lax.broadcasted_iota(jnp.int32, (tq, win), 0)
    col = lax.broadcasted_iota(jnp.int32, (tq, win), 1)
    absrel = jnp.abs((start - q0) + (col - row))
    valid = absrel <= SWA_RADIUS
    dist = absrel.astype(F32)
    k = k_ref[pl.ds(start, win), :]
    v = v_ref[pl.ds(start, win), :]
    logits = [_dot_nt(q_ref[r], k) * ATTN_SCALE for r in range(SWA_GROUP)]
    for r in range(SWA_GROUP):
        head = g * SWA_GROUP + r
        slope = slopes_ref[head]
        sink = sink_ref[layer * SWA_Q_HEADS + head]
        s = jnp.where(valid, logits[r] - slope * dist, NEG)
        m = jnp.maximum(jnp.max(s, axis=-1, keepdims=True), sink)
        p = jnp.exp(s - m)
        den = jnp.sum(p, axis=-1, keepdims=True) + jnp.exp(sink - m)
        y_ref[r] = (_dot(p.astype(BF16), v) / den).astype(BF16)


def _swa_call(qkv, slopes_c, sink_flat, layer, batch, seq):
    T = batch * seq
    tq = 256
    win = tq + 2 * SWA_RADIUS
    qkv_v = qkv.reshape(N_SLABS, batch, seq, LANES)
    kern = functools.partial(_swa_kernel, tq=tq, win=win, seq=seq, layer=layer)
    out = pl.pallas_call(
        kern,
        grid=(batch, SWA_KV_HEADS, seq // tq),
        in_specs=[
            pl.BlockSpec(memory_space=pltpu.SMEM),
            pl.BlockSpec(memory_space=pltpu.SMEM),
            pl.BlockSpec((SWA_GROUP, None, tq, LANES), lambda b, g, i: (SLAB_CQ // SWA_GROUP + g, b, i, 0)),
            pl.BlockSpec((None, None, seq, LANES), lambda b, g, i: (SLAB_CK + g, b, 0, 0)),
            pl.BlockSpec((None, None, seq, LANES), lambda b, g, i: (SLAB_CV + g, b, 0, 0)),
        ],
        out_specs=pl.BlockSpec((SWA_GROUP, None, tq, LANES), lambda b, g, i: (g, b, i, 0)),
        out_shape=jax.ShapeDtypeStruct((SWA_Q_HEADS, batch, seq, LANES), BF16),
        compiler_params=_params(3),
        name="window_gqa_attention",
    )(slopes_c, sink_flat, qkv_v, qkv_v, qkv_v)
    return out.reshape(SWA_Q_HEADS, T, LANES)


def _outproj_kernel(ya_ref, yb_ref, yc_ref, x_ref, w_ref, gn_ref, o_ref, hn_ref, *, n_a, n_b, n_k, rows):
    k = pl.program_id(1)

    @pl.when(k == 0)
    def _():
        o_ref[...] = jnp.zeros_like(o_ref)

    r0 = pl.multiple_of(k * rows, rows)
    o_ref[pl.ds(r0, rows), :] += x_ref[...]

    def accumulate(y_ref):
        y = jnp.concatenate([y_ref[0], y_ref[1]], axis=-1)
        o_ref[...] += _dot(y, w_ref[...].astype(BF16))

    @pl.when(k < n_a)
    def _():
        accumulate(ya_ref)

    @pl.when((k >= n_a) & (k < n_a + n_b))
    def _():
        accumulate(yb_ref)

    @pl.when(k >= n_a + n_b)
    def _():
        accumulate(yc_ref)

    @pl.when(k == n_k - 1)
    def _():
        hn_ref[...] = _rms(o_ref[...], gn_ref[...]).astype(BF16)


def _outproj_call(ya, yb, yc, x, w_out, layer, gains, gain_layer):
    T, D = x.shape
    tm = 1024
    n_a, n_b, n_c = ya.shape[0] // 2, yb.shape[0] // 2, yc.shape[0] // 2
    n_k = n_a + n_b + n_c
    rows = tm // n_k
    kern = functools.partial(_outproj_kernel, n_a=n_a, n_b=n_b, n_k=n_k, rows=rows)
    row_spec = pl.BlockSpec((tm, D), lambda i, k: (i, 0))
    return pl.pallas_call(
        kern,
        grid=(T // tm, n_k),
        in_specs=[
            pl.BlockSpec((2, tm, LANES), lambda i, k: (jnp.minimum(k, n_a - 1), i, 0)),
            pl.BlockSpec((2, tm, LANES), lambda i, k: (jnp.clip(k - n_a, 0, n_b - 1), i, 0)),
            pl.BlockSpec((2, tm, LANES), lambda i, k: (jnp.clip(k - n_a - n_b, 0, n_c - 1), i, 0)),
            pl.BlockSpec((rows, D), lambda i, k: (i * n_k + k, 0)),
            pl.BlockSpec((None, 2 * LANES, D), lambda i, k: (layer, k, 0)),
            pl.BlockSpec((None, 1, D), lambda i, k: (gain_layer, 0, 0)),
        ],
        out_specs=[row_spec, row_spec],
        out_shape=[jax.ShapeDtypeStruct((T, D), F32), jax.ShapeDtypeStruct((T, D), BF16)],
        compiler_params=_params(2),
        name="output_projection",
    )(ya, yb, yc, x, w_out, gains)


def kernel(x, ffn1_norm, ffn1_w_gate, ffn1_w_up, ffn1_w_down, mix_norm, w_in, w_out, diff_lambda_q1, diff_lambda_k1, diff_lambda_q2, diff_lambda_k2, diff_subln, swa_sink, ffn2_norm, ffn2_w_gate, ffn2_w_up, ffn2_w_down, final_norm):
    batch, seq, d_model = x.shape
    depth = w_in.shape[0]
    xf = x.reshape(batch * seq, d_model)
    gain3 = lambda a: a.reshape(a.shape[0], 1, a.shape[1])
    ffn1_g, mix_g, ffn2_g = gain3(ffn1_norm), gain3(mix_norm), gain3(ffn2_norm)
    final_g = final_norm.reshape(1, 1, d_model)
    slopes_a = jnp.asarray(SLOPES_A, F32)
    slopes_b = jnp.asarray(SLOPES_B, F32)
    slopes_c = jnp.asarray(SLOPES_C, F32)
    sink_flat = swa_sink.astype(F32).reshape(-1)

    h = _norm_call(xf, ffn1_g, 0)
    for l in range(depth):
        lambda_init = 0.8 - 0.6 * math.exp(-0.3 * l)
        xf, h = _ffn_call(h, xf, ffn1_w_gate, ffn1_w_up, ffn1_w_down, l, mix_g, l, final=False)
        qkv = _inproj_call(h, w_in, l)
        ya = _dilated_attention(qkv, slopes_a, batch, seq)
        yb = _diff_call(qkv, slopes_b, diff_lambda_q1, diff_lambda_k1, diff_lambda_q2, diff_lambda_k2,
                        diff_subln, l, lambda_init, batch, seq)
        yc = _swa_call(qkv, slopes_c, sink_flat, l, batch, seq)
        xf, h = _outproj_call(ya, yb, yc, xf, w_out, l, ffn2_g, l)
        if l + 1 < depth:
            xf, h = _ffn_call(h, xf, ffn2_w_gate, ffn2_w_up, ffn2_w_down, l, ffn1_g, l + 1, final=False)
        else:
            xf = _ffn_call(h, xf, ffn2_w_gate, ffn2_w_up, ffn2_w_down, l, final_g, 0, final=True)
    return xf.reshape(batch, seq, d_model)
```

```python
import functools
import math

import numpy as np
import jax
import jax.numpy as jnp
from jax import lax
from jax.experimental import pallas as pl
from jax.experimental.pallas import tpu as pltpu

F32 = jnp.float32
BF16 = jnp.bfloat16

HEAD_DIM = 128
DIL_HEADS = 6
DIL_BRANCHES = ((128, 1), (512, 4), (2048, 16))
DIL_RADIUS = 64
DIFF_HEADS = 4
DIFF_QK_DIM = HEAD_DIM // 2
SWA_Q_HEADS = 6
SWA_KV_HEADS = 2
SWA_GROUP = SWA_Q_HEADS // SWA_KV_HEADS
SWA_RADIUS = 128
N_ALIBI_HEADS = SWA_Q_HEADS + DIL_HEADS + DIFF_HEADS
RMS_EPS = 1e-6
NEG = -1e30
ATTN_SCALE = HEAD_DIM ** -0.5
LOG2E = math.log2(math.e)
DIFF_Q_SCALE = DIFF_QK_DIM ** -0.5 * LOG2E

SLAB_AQ, SLAB_AK, SLAB_AV = 0, 6, 12
SLAB_BQ, SLAB_BK, SLAB_BV = 18, 22, 26
SLAB_CQ, SLAB_CK, SLAB_CV = 30, 36, 38
N_SLABS = 40

LANES = 128
VMEM_LIMIT_BYTES = 60000 * 1024

_SLOPES = [float(np.float32(2.0 ** (-8.0 * (n + 1) / N_ALIBI_HEADS))) for n in range(N_ALIBI_HEADS)]
SLOPES_C = _SLOPES[:SWA_Q_HEADS]
SLOPES_A = _SLOPES[SWA_Q_HEADS:SWA_Q_HEADS + DIL_HEADS]
SLOPES_B = _SLOPES[SWA_Q_HEADS + DIL_HEADS:]


def _params(n_grid):
    return pltpu.CompilerParams(dimension_semantics=("arbitrary",) * n_grid,
                                vmem_limit_bytes=VMEM_LIMIT_BYTES)


def _rms(x, g):
    ms = jnp.mean(x * x, axis=-1, keepdims=True)
    return x * lax.rsqrt(ms + RMS_EPS) * g


def _dot(a, b):
    return jnp.dot(a, b, preferred_element_type=F32)


def _dot_nt(a, b):
    return lax.dot_general(a, b, (((1,), (1,)), ((), ())), preferred_element_type=F32)


def _norm_kernel(x_ref, g_ref, o_ref):
    o_ref[...] = _rms(x_ref[...], g_ref[...]).astype(BF16)


def _norm_call(x, gains, layer):
    T, D = x.shape
    tm = 512
    return pl.pallas_call(
        _norm_kernel,
        grid=(T // tm,),
        in_specs=[pl.BlockSpec((tm, D), lambda i: (i, 0)),
                  pl.BlockSpec((None, 1, D), lambda i: (layer, 0, 0))],
        out_specs=pl.BlockSpec((tm, D), lambda i: (i, 0)),
        out_shape=jax.ShapeDtypeStruct((T, D), BF16),
        compiler_params=_params(1),
        name="rms_norm_bf16",
    )(x, gains)


def _ffn_kernel(h_ref, x_ref, wg_ref, wu_ref, wd_ref, wgt_ref, wut_ref, wdt_ref, gn_ref,
                *out_refs, n_full, n_chunk, rows, final):
    o_ref = out_refs[0]
    j = pl.program_id(1)

    @pl.when(j == 0)
    def _():
        o_ref[...] = jnp.zeros_like(o_ref)

    @pl.when(j < n_chunk)
    def _():
        r0 = pl.multiple_of(j * rows, rows)
        o_ref[pl.ds(r0, rows), :] += x_ref[...]

    def step(wg, wu, wd):
        h = h_ref[...]
        g = _dot(h, wg.astype(BF16))
        u = _dot(h, wu.astype(BF16))
        act = (0.5 * (g * jax.nn.sigmoid(g)) * u).astype(BF16)
        o_ref[...] += _dot(act, wd.astype(BF16))

    @pl.when(j < n_full)
    def _():
        step(wg_ref[...], wu_ref[...], wd_ref[...])

    @pl.when(j == n_full)
    def _():
        step(wgt_ref[...], wut_ref[...], wdt_ref[...])
        xn = o_ref[...]
        if final:
            o_ref[...] = _rms(xn, gn_ref[...])
        else:
            out_refs[1][...] = _rms(xn, gn_ref[...]).astype(BF16)


def _ffn_call(h, x, w_gate, w_up, w_down, layer, gains, gain_layer, final):
    T, D = x.shape
    d_ff = w_gate.shape[-1]
    tm, tf, tail = 1024, 256, 128
    n_full = d_ff // tf
    assert n_full * tf + tail == d_ff
    tail_blk = (n_full * tf) // tail
    n_chunk = 8
    rows = tm // n_chunk
    last = n_full - 1
    kern = functools.partial(_ffn_kernel, n_full=n_full, n_chunk=n_chunk, rows=rows, final=final)
    once = pl.Buffered(1)
    in_specs = [
        pl.BlockSpec((tm, D), lambda i, j: (i, 0)),
        pl.BlockSpec((rows, D), lambda i, j: (i * n_chunk + jnp.minimum(j, n_chunk - 1), 0)),
        pl.BlockSpec((None, D, tf), lambda i, j: (layer, 0, jnp.minimum(j, last))),
        pl.BlockSpec((None, D, tf), lambda i, j: (layer, 0, jnp.minimum(j, last))),
        pl.BlockSpec((None, tf, D), lambda i, j: (layer, jnp.minimum(j, last), 0)),
        pl.BlockSpec((None, D, tail), lambda i, j: (layer, 0, tail_blk), pipeline_mode=once),
        pl.BlockSpec((None, D, tail), lambda i, j: (layer, 0, tail_blk), pipeline_mode=once),
        pl.BlockSpec((None, tail, D), lambda i, j: (layer, tail_blk, 0), pipeline_mode=once),
        pl.BlockSpec((None, 1, D), lambda i, j: (gain_layer, 0, 0)),
    ]
    row_spec = pl.BlockSpec((tm, D), lambda i, j: (i, 0))
    if final:
        out_specs = row_spec
        out_shape = jax.ShapeDtypeStruct((T, D), F32)
    else:
        out_specs = [row_spec, row_spec]
        out_shape = [jax.ShapeDtypeStruct((T, D), F32), jax.ShapeDtypeStruct((T, D), BF16)]
    return pl.pallas_call(
        kern,
        grid=(T // tm, n_full + 1),
        in_specs=in_specs,
        out_specs=out_specs,
        out_shape=out_shape,
        compiler_params=_params(2),
        name="swiglu_half_step",
    )(h, x, w_gate, w_up, w_down, w_gate, w_up, w_down, gains)


def _inproj_kernel(scale_ref, h_ref, w_ref, o_ref, *, n_sub):
    j = pl.program_id(1)
    r = _dot(h_ref[...], w_ref[...].astype(BF16))
    for t in range(n_sub):
        o_ref[t] = (r[:, t * LANES:(t + 1) * LANES] * scale_ref[j * n_sub + t]).astype(BF16)


def _inproj_call(h, w_in, layer):
    T, D = h.shape
    tm, tn = 1024, 8 * LANES
    n_sub = tn // LANES
    n_blk = w_in.shape[-1] // tn
    assert n_blk * n_sub == N_SLABS
    scale = np.ones((N_SLABS,), np.float32)
    scale[SLAB_BQ:SLAB_BK] = DIFF_Q_SCALE
    return pl.pallas_call(
        functools.partial(_inproj_kernel, n_sub=n_sub),
        grid=(T // tm, n_blk),
        in_specs=[pl.BlockSpec(memory_space=pltpu.SMEM),
                  pl.BlockSpec((tm, D), lambda i, j: (i, 0)),
                  pl.BlockSpec((None, D, tn), lambda i, j: (layer, 0, j))],
        out_specs=pl.BlockSpec((n_sub, tm, LANES), lambda i, j: (j, i, 0)),
        out_shape=jax.ShapeDtypeStruct((N_SLABS, T, LANES), BF16),
        compiler_params=_params(2),
        name="input_projection",
    )(jnp.asarray(scale), h, w_in)


def _dil_kernel(slopes_ref, q_ref, k_ref, v_ref, y_ref, kf, vf, qf, o_acc, l_acc, *, tq, seq, blk):
    h = pl.program_id(1)
    i = pl.program_id(2)
    slope = slopes_ref[h]

    @pl.when(i == 0)
    def _():
        kf[...] = k_ref[...].astype(F32)
        vf[...] = v_ref[...].astype(F32)

    qf[...] = q_ref[...].astype(F32)

    def rows(start, n, dil):
        if dil > 1:
            return pl.ds(start, n, stride=dil)
        if isinstance(start, int):
            return pl.ds(start, n)
        return pl.ds(pl.multiple_of(start, DIL_RADIUS), n)

    def block_logits(dil, n_q, r, b, first):
        sub_len = seq // dil
        win = min(n_q + 2 * DIL_RADIUS, sub_len)
        local_row = b * (n_q * dil) + r
        u0 = i * (tq // dil) + b * n_q
        sub_start = jnp.clip(u0 - DIL_RADIUS, 0, sub_len - win)
        tok_start = sub_start * dil + r
        row = lax.broadcasted_iota(jnp.int32, (n_q, win), 0)
        col = lax.broadcasted_iota(jnp.int32, (n_q, win), 1)
        absrel = jnp.abs((sub_start - u0) + (col - row))
        q = qf[rows(local_row, n_q, dil), :].astype(BF16)
        k = kf[rows(tok_start, win, dil), :].astype(BF16)
        s = _dot_nt(q, k) * ATTN_SCALE
        s = jnp.where(absrel <= DIL_RADIUS, s - slope * (absrel.astype(F32) * float(dil)), NEG)
        return s, rows(tok_start, win, dil), rows(local_row, n_q, dil), n_q, first

    def block_finish(s, src, dst, n_q, first):
        m = jnp.max(s, axis=-1, keepdims=True)
        p = jnp.exp(s - m)
        den = jnp.sum(p, axis=-1, keepdims=True)
        o = _dot(p.astype(BF16), vf[src, :].astype(BF16)) / den
        lse = jnp.broadcast_to(m + jnp.log(den), (n_q, LANES))
        if not first:
            lse_prev = l_acc[dst, :]
            mx = jnp.maximum(lse_prev, lse)
            e_prev = jnp.exp(lse_prev - mx)
            e_cur = jnp.exp(lse - mx)
            tot = e_prev + e_cur
            o = (e_prev * o_acc[dst, :] + e_cur * o) / tot
            lse = mx + jnp.log(tot)
        o_acc[dst, :] = o
        l_acc[dst, :] = lse

    blocks = []
    for idx, (_, dil) in enumerate(DIL_BRANCHES):
        per_class = tq // dil
        n_q = min(blk, per_class)
        blocks += [(dil, n_q, r, b, idx == 0) for r in range(dil) for b in range(per_class // n_q)]

    pending = block_logits(*blocks[0])
    for nxt in blocks[1:]:
        ahead = block_logits(*nxt)
        block_finish(*pending)
        pending = ahead
    block_finish(*pending)

    y_ref[...] = o_acc[...].astype(BF16)


def _dilated_attention(qkv, slopes_a, batch, seq):
    T = batch * seq
    tq, blk = 1024, 128
    assert all(tq % (d * 8) == 0 for _, d in DIL_BRANCHES)
    qkv_v = qkv.reshape(N_SLABS, batch, seq, LANES)
    kern = functools.partial(_dil_kernel, tq=tq, seq=seq, blk=blk)
    out = pl.pallas_call(
        kern,
        grid=(batch, DIL_HEADS, seq // tq),
        in_specs=[
            pl.BlockSpec(memory_space=pltpu.SMEM),
            pl.BlockSpec((None, None, tq, LANES), lambda b, h, i: (SLAB_AQ + h, b, i, 0)),
            pl.BlockSpec((None, None, seq, LANES), lambda b, h, i: (SLAB_AK + h, b, 0, 0)),
            pl.BlockSpec((None, None, seq, LANES), lambda b, h, i: (SLAB_AV + h, b, 0, 0)),
        ],
        out_specs=pl.BlockSpec((None, None, tq, LANES), lambda b, h, i: (h, b, i, 0)),
        out_shape=jax.ShapeDtypeStruct((DIL_HEADS, batch, seq, LANES), BF16),
        scratch_shapes=[pltpu.VMEM((seq, LANES), F32), pltpu.VMEM((seq, LANES), F32),
                        pltpu.VMEM((tq, LANES), F32), pltpu.VMEM((tq, LANES), F32),
                        pltpu.VMEM((tq, LANES), F32)],
        compiler_params=_params(3),
        name="dilated_attention",
    )(slopes_a, qkv_v, qkv_v, qkv_v)
    return out.reshape(DIL_HEADS, T, LANES)


DIFF_FEAT = 3
DIFF_VT_ROWS = HEAD_DIM + 16


def _diff_kernel(slopes_ref, q_ref, k_ref, v_ref, lq1_ref, lk1_ref, lq2_ref, lk2_ref, sg_ref,
                 y_ref, ka_ref, vt_ref, qt_ref, *, tq, tk, seq, lambda_init):
    h = pl.program_id(1)
    i = pl.program_id(2)
    slope2 = slopes_ref[h] * LOG2E
    n_kb = seq // tk
    half = DIFF_QK_DIM

    @pl.when(i == 0)
    def _():
        lane = lax.broadcasted_iota(jnp.int32, (tk, LANES), 1)
        k_local = lax.broadcasted_iota(jnp.int32, (tk, LANES), 0)
        rem = slope2 * k_local.astype(F32)
        parts = []
        for _ in range(DIFF_FEAT):
            part = rem.astype(BF16).astype(F32)
            parts.append(part)
            rem = rem - part
        ones_row = (lax.broadcasted_iota(jnp.int32, (DIFF_VT_ROWS - HEAD_DIM, tk), 0) == 0)
        for c in range(n_kb):
            kf = k_ref[c * tk:(c + 1) * tk, :].astype(F32)
            for mp in range(2):
                lo, fo = mp * half, half - mp * half
                ka = jnp.where((lane >= lo) & (lane < lo + half), kf, 0.0)
                for t, part in enumerate(parts):
                    ka = jnp.where(lane == fo + t, part, ka)
                ka_ref[mp, c * tk:(c + 1) * tk, :] = ka.astype(BF16)
            vt_ref[c, :HEAD_DIM, :] = v_ref[c * tk:(c + 1) * tk, :].astype(F32).T.astype(BF16)
            vt_ref[c, HEAD_DIM:, :] = ones_row.astype(BF16)

    qt = q_ref[...].astype(F32).T
    row = lax.broadcasted_iota(jnp.int32, (LANES, tq), 0)
    for mp in range(2):
        lo, fo = mp * half, half - mp * half
        own = jnp.where((row >= lo) & (row < lo + half), qt, 0.0)
        feat = (row >= fo) & (row < fo + DIFF_FEAT)
        qt_ref[0, mp] = jnp.where(feat, 1.0, own).astype(BF16)
        qt_ref[1, mp] = jnp.where(feat, -1.0, own).astype(BF16)
        qt_ref[2, mp] = own.astype(BF16)

    q0 = i * tq
    kb_diag = q0 // tk
    q_local = lax.broadcasted_iota(jnp.int32, (1, tq), 1).astype(F32)

    def logits(j):
        kb = kb_diag + j
        kb = jnp.where(kb >= n_kb, kb - n_kb, kb)
        k0 = pl.multiple_of(kb * tk, tk)
        if j == 0:
            krow = lax.broadcasted_iota(jnp.int32, (tk, tq), 0)
            qcol = lax.broadcasted_iota(jnp.int32, (tk, tq), 1)
            tile = slope2 * jnp.abs((qcol - krow) + (q0 - k0)).astype(F32)
            return kb, 0.0, [_dot(ka_ref[mp, pl.ds(k0, tk), :], qt_ref[2, mp]) - tile for mp in range(2)]
        right = (kb > kb_diag).astype(jnp.int32)
        sgn = (1 - 2 * right).astype(F32)
        row_term = (-sgn * slope2) * ((q0 - k0).astype(F32) + q_local)
        return kb, row_term, [_dot(ka_ref[mp, pl.ds(k0, tk), :], qt_ref[right, mp]) for mp in range(2)]

    def local_softmax(kb, row_term, s_maps):
        vt = vt_ref[kb]
        out = []
        for s in s_maps:
            m = jnp.max(s, axis=0, keepdims=True)
            p = jnp.exp2(s - m).astype(BF16)
            out.append((m + row_term, _dot(vt, p)))
        return out

    state = None
    ahead = 2
    queue = [logits(j) for j in range(min(ahead, n_kb))]
    for j in range(n_kb):
        cur = queue.pop(0)
        if j + ahead < n_kb:
            queue.append(logits(j + ahead))
        part = local_softmax(*cur)
        if state is None:
            state = part
            continue
        merged = []
        for (m, acc), (m_b, acc_b) in zip(state, part):
            m_new = jnp.maximum(m, m_b)
            merged.append((m_new, jnp.exp2(m - m_new) * acc + jnp.exp2(m_b - m_new) * acc_b))
        state = merged
    (_, acc0), (_, acc1) = state

    lam = (jnp.exp(jnp.sum(lq1_ref[...] * lk1_ref[...], axis=-1, keepdims=True))
           - jnp.exp(jnp.sum(lq2_ref[...] * lk2_ref[...], axis=-1, keepdims=True))
           + lambda_init)
    o0 = acc0[:HEAD_DIM] / acc0[HEAD_DIM:HEAD_DIM + 1]
    o1 = acc1[:HEAD_DIM] / acc1[HEAD_DIM:HEAD_DIM + 1]
    o = (o0 - lam * o1).T
    y_ref[...] = (_rms(o, sg_ref[...]) * (1.0 - lambda_init)).astype(BF16)


def _diff_call(qkv, slopes_b, lq1, lk1, lq2, lk2, subln, layer, lambda_init, batch, seq):
    T = batch * seq
    tq, tk = 256, 512
    assert tk % tq == 0 and seq % tk == 0
    qkv_v = qkv.reshape(N_SLABS, batch, seq, LANES)
    vec = lambda a: a.reshape(a.shape[0], 1, a.shape[1])
    lam_spec = pl.BlockSpec((None, 1, DIFF_QK_DIM), lambda b, h, i: (layer, 0, 0))
    kern = functools.partial(_diff_kernel, tq=tq, tk=tk, seq=seq, lambda_init=lambda_init)
    out = pl.pallas_call(
        kern,
        grid=(batch, DIFF_HEADS, seq // tq),
        in_specs=[
            pl.BlockSpec(memory_space=pltpu.SMEM),
            pl.BlockSpec((None, None, tq, LANES), lambda b, h, i: (SLAB_BQ + h, b, i, 0)),
            pl.BlockSpec((None, None, seq, LANES), lambda b, h, i: (SLAB_BK + h, b, 0, 0)),
            pl.BlockSpec((None, None, seq, LANES), lambda b, h, i: (SLAB_BV + h, b, 0, 0)),
            lam_spec, lam_spec, lam_spec, lam_spec,
            pl.BlockSpec((None, 1, HEAD_DIM), lambda b, h, i: (layer, 0, 0)),
        ],
        out_specs=pl.BlockSpec((None, None, tq, LANES), lambda b, h, i: (h, b, i, 0)),
        out_shape=jax.ShapeDtypeStruct((DIFF_HEADS, batch, seq, LANES), BF16),
        scratch_shapes=[pltpu.VMEM((2, seq, LANES), BF16), pltpu.VMEM((seq // tk, DIFF_VT_ROWS, tk), BF16),
                        pltpu.VMEM((3, 2, LANES, tq), BF16)],
        compiler_params=_params(3),
        name="differential_attention",
    )(slopes_b, qkv_v, qkv_v, qkv_v, vec(lq1), vec(lk1), vec(lq2), vec(lk2), vec(subln))
    return out.reshape(DIFF_HEADS, T, LANES)


def _swa_kernel(slopes_ref, sink_ref, q_ref, k_ref, v_ref, y_ref, *, tq, win, seq, layer):
    g = pl.program_id(1)
    i = pl.program_id(2)
    q0 = i * tq
    start = pl.multiple_of(jnp.clip(q0 - SWA_RADIUS, 0, seq - win), SWA_RADIUS)
    row = lax.broadcasted_iota(jnp.int32, (tq, win), 0)
    col = lax.broadcasted_iota(jnp.int32, (tq, win), 1)
    absrel = jnp.abs((start - q0) + (col - row))
    valid = absrel <= SWA_RADIUS
    dist = absrel.astype(F32)
    k = k_ref[pl.ds(start, win), :]
    v = v_ref[pl.ds(start, win), :]
    logits = [_dot_nt(q_ref[r], k) * ATTN_SCALE for r in range(SWA_GROUP)]
    for r in range(SWA_GROUP):
        head = g * SWA_GROUP + r
        slope = slopes_ref[head]
        sink = sink_ref[layer * SWA_Q_HEADS + head]
        s = jnp.where(valid, logits[r] - slope * dist, NEG)
        m = jnp.maximum(jnp.max(s, axis=-1, keepdims=True), sink)
        p = jnp.exp(s - m)
        den = jnp.sum(p, axis=-1, keepdims=True) + jnp.exp(sink - m)
        y_ref[r] = (_dot(p.astype(BF16), v) / den).astype(BF16)


def _swa_call(qkv, slopes_c, sink_flat, layer, batch, seq):
    T = batch * seq
    tq = 256
    win = tq + 2 * SWA_RADIUS
    qkv_v = qkv.reshape(N_SLABS, batch, seq, LANES)
    kern = functools.partial(_swa_kernel, tq=tq, win=win, seq=seq, layer=layer)
    out = pl.pallas_call(
        kern,
        grid=(batch, SWA_KV_HEADS, seq // tq),
        in_specs=[
            pl.BlockSpec(memory_space=pltpu.SMEM),
            pl.BlockSpec(memory_space=pltpu.SMEM),
            pl.BlockSpec((SWA_GROUP, None, tq, LANES), lambda b, g, i: (SLAB_CQ // SWA_GROUP + g, b, i, 0)),
            pl.BlockSpec((None, None, seq, LANES), lambda b, g, i: (SLAB_CK + g, b, 0, 0)),
            pl.BlockSpec((None, None, seq, LANES), lambda b, g, i: (SLAB_CV + g, b, 0, 0)),
        ],
        out_specs=pl.BlockSpec((SWA_GROUP, None, tq, LANES), lambda b, g, i: (g, b, i, 0)),
        out_shape=jax.ShapeDtypeStruct((SWA_Q_HEADS, batch, seq, LANES), BF16),
        compiler_params=_params(3),
        name="window_gqa_attention",
    )(slopes_c, sink_flat, qkv_v, qkv_v, qkv_v)
    return out.reshape(SWA_Q_HEADS, T, LANES)


def _outproj_kernel(ya_ref, yb_ref, yc_ref, x_ref, w_ref, gn_ref, o_ref, hn_ref, wb_ref, *, cast_rows):
    @pl.when(pl.program_id(0) == 0)
    def _():
        for r0 in range(0, wb_ref.shape[0], cast_rows):
            wb_ref[r0:r0 + cast_rows, :] = w_ref[r0:r0 + cast_rows, :].astype(BF16)

    slabs = [y_ref[s] for y_ref in (ya_ref, yb_ref, yc_ref) for s in range(y_ref.shape[0])]
    y = jnp.concatenate(slabs, axis=-1)
    xn = x_ref[...] + _dot(y, wb_ref[...])
    o_ref[...] = xn
    hn_ref[...] = _rms(xn, gn_ref[...]).astype(BF16)


def _outproj_call(ya, yb, yc, x, w_out, layer, gains, gain_layer):
    T, D = x.shape
    tm = 512
    row_spec = pl.BlockSpec((tm, D), lambda i: (i, 0))
    slab_spec = lambda y: pl.BlockSpec((y.shape[0], tm, LANES), lambda i: (0, i, 0))
    return pl.pallas_call(
        functools.partial(_outproj_kernel, cast_rows=256),
        grid=(T // tm,),
        in_specs=[
            slab_spec(ya), slab_spec(yb), slab_spec(yc),
            row_spec,
            pl.BlockSpec((None,) + w_out.shape[1:], lambda i: (layer, 0, 0), pipeline_mode=pl.Buffered(1)),
            pl.BlockSpec((None, 1, D), lambda i: (gain_layer, 0, 0)),
        ],
        out_specs=[row_spec, row_spec],
        out_shape=[jax.ShapeDtypeStruct((T, D), F32), jax.ShapeDtypeStruct((T, D), BF16)],
        scratch_shapes=[pltpu.VMEM(w_out.shape[1:], BF16)],
        compiler_params=_params(1),
        name="output_projection",
    )(ya, yb, yc, x, w_out, gains)


def kernel(x, ffn1_norm, ffn1_w_gate, ffn1_w_up, ffn1_w_down, mix_norm, w_in, w_out, diff_lambda_q1, diff_lambda_k1, diff_lambda_q2, diff_lambda_k2, diff_subln, swa_sink, ffn2_norm, ffn2_w_gate, ffn2_w_up, ffn2_w_down, final_norm):
    batch, seq, d_model = x.shape
    depth = w_in.shape[0]
    xf = x.reshape(batch * seq, d_model)
    gain3 = lambda a: a.reshape(a.shape[0], 1, a.shape[1])
    ffn1_g, mix_g, ffn2_g = gain3(ffn1_norm), gain3(mix_norm), gain3(ffn2_norm)
    final_g = final_norm.reshape(1, 1, d_model)
    slopes_a = jnp.asarray(SLOPES_A, F32)
    slopes_b = jnp.asarray(SLOPES_B, F32)
    slopes_c = jnp.asarray(SLOPES_C, F32)
    sink_flat = swa_sink.astype(F32).reshape(-1)

    h = _norm_call(xf, ffn1_g, 0)
    for l in range(depth):
        lambda_init = 0.8 - 0.6 * math.exp(-0.3 * l)
        xf, h = _ffn_call(h, xf, ffn1_w_gate, ffn1_w_up, ffn1_w_down, l, mix_g, l, final=False)
        qkv = _inproj_call(h, w_in, l)
        ya = _dilated_attention(qkv, slopes_a, batch, seq)
        yb = _diff_call(qkv, slopes_b, diff_lambda_q1, diff_lambda_k1, diff_lambda_q2, diff_lambda_k2,
                        diff_subln, l, lambda_init, batch, seq)
        yc = _swa_call(qkv, slopes_c, sink_flat, l, batch, seq)
        xf, h = _outproj_call(ya, yb, yc, xf, w_out, l, ffn2_g, l)
        if l + 1 < depth:
            xf, h = _ffn_call(h, xf, ffn2_w_gate, ffn2_w_up, ffn2_w_down, l, ffn1_g, l + 1, final=False)
        else:
            xf = _ffn_call(h, xf, ffn2_w_gate, ffn2_w_up, ffn2_w_down, l, final_g, 0, final=True)
    return xf.reshape(batch, seq, d_model)
```

```python
import functools
import math

import numpy as np
import jax
import jax.numpy as jnp
from jax import lax
from jax.experimental import pallas as pl
from jax.experimental.pallas import tpu as pltpu

F32 = jnp.float32
BF16 = jnp.bfloat16

HEAD_DIM = 128
DIL_HEADS = 6
DIL_BRANCHES = ((128, 1), (512, 4), (2048, 16))
DIL_RADIUS = 64
DIFF_HEADS = 4
DIFF_QK_DIM = HEAD_DIM // 2
SWA_Q_HEADS = 6
SWA_KV_HEADS = 2
SWA_GROUP = SWA_Q_HEADS // SWA_KV_HEADS
SWA_RADIUS = 128
N_ALIBI_HEADS = SWA_Q_HEADS + DIL_HEADS + DIFF_HEADS
RMS_EPS = 1e-6
NEG = -1e30
ATTN_SCALE = HEAD_DIM ** -0.5
LOG2E = math.log2(math.e)
ATTN_Q_SCALE = ATTN_SCALE * LOG2E
DIFF_Q_SCALE = DIFF_QK_DIM ** -0.5 * LOG2E

SLAB_AQ, SLAB_AK, SLAB_AV = 0, 6, 12
SLAB_BQ, SLAB_BK, SLAB_BV = 18, 22, 26
SLAB_CQ, SLAB_CK, SLAB_CV = 30, 36, 38
N_SLABS = 40

LANES = 128
VMEM_LIMIT_BYTES = 60000 * 1024

_SLOPES = [float(np.float32(2.0 ** (-8.0 * (n + 1) / N_ALIBI_HEADS))) for n in range(N_ALIBI_HEADS)]
SLOPES_C = _SLOPES[:SWA_Q_HEADS]
SLOPES_A = _SLOPES[SWA_Q_HEADS:SWA_Q_HEADS + DIL_HEADS]
SLOPES_B = _SLOPES[SWA_Q_HEADS + DIL_HEADS:]


def _params(n_grid):
    return pltpu.CompilerParams(dimension_semantics=("arbitrary",) * n_grid,
                                vmem_limit_bytes=VMEM_LIMIT_BYTES)


def _rms(x, g):
    ms = jnp.mean(x * x, axis=-1, keepdims=True)
    return x * lax.rsqrt(ms + RMS_EPS) * g


def _dot(a, b):
    return jnp.dot(a, b, preferred_element_type=F32)


def _dot_nt(a, b):
    return lax.dot_general(a, b, (((1,), (1,)), ((), ())), preferred_element_type=F32)


def _norm_kernel(x_ref, g_ref, o_ref):
    o_ref[...] = _rms(x_ref[...], g_ref[...]).astype(BF16)


def _norm_call(x, gains, layer):
    T, D = x.shape
    tm = 512
    return pl.pallas_call(
        _norm_kernel,
        grid=(T // tm,),
        in_specs=[pl.BlockSpec((tm, D), lambda i: (i, 0)),
                  pl.BlockSpec((None, 1, D), lambda i: (layer, 0, 0))],
        out_specs=pl.BlockSpec((tm, D), lambda i: (i, 0)),
        out_shape=jax.ShapeDtypeStruct((T, D), BF16),
        compiler_params=_params(1),
        name="rms_norm_bf16",
    )(x, gains)


def _ffn_kernel(h_ref, x_ref, wg_ref, wu_ref, wd_ref, wgt_ref, wut_ref, wdt_ref, gn_ref,
                *out_refs, n_full, n_chunk, rows, final):
    o_ref = out_refs[0]
    j = pl.program_id(1)

    @pl.when(j == 0)
    def _():
        o_ref[...] = jnp.zeros_like(o_ref)

    @pl.when(j < n_chunk)
    def _():
        r0 = pl.multiple_of(j * rows, rows)
        o_ref[pl.ds(r0, rows), :] += x_ref[...]

    def step(wg, wu, wd):
        h = h_ref[...]
        g = _dot(h, wg.astype(BF16))
        u = _dot(h, wu.astype(BF16))
        act = (0.5 * (g * jax.nn.sigmoid(g)) * u).astype(BF16)
        o_ref[...] += _dot(act, wd.astype(BF16))

    @pl.when(j < n_full)
    def _():
        step(wg_ref[...], wu_ref[...], wd_ref[...])

    @pl.when(j == n_full)
    def _():
        step(wgt_ref[...], wut_ref[...], wdt_ref[...])
        xn = o_ref[...]
        if final:
            o_ref[...] = _rms(xn, gn_ref[...])
        else:
            out_refs[1][...] = _rms(xn, gn_ref[...]).astype(BF16)


def _ffn_call(h, x, w_gate, w_up, w_down, layer, gains, gain_layer, final):
    T, D = x.shape
    d_ff = w_gate.shape[-1]
    tm, tf, tail = 1024, 256, 128
    n_full = d_ff // tf
    assert n_full * tf + tail == d_ff
    tail_blk = (n_full * tf) // tail
    n_chunk = 8
    rows = tm // n_chunk
    last = n_full - 1
    kern = functools.partial(_ffn_kernel, n_full=n_full, n_chunk=n_chunk, rows=rows, final=final)
    once = pl.Buffered(1)
    in_specs = [
        pl.BlockSpec((tm, D), lambda i, j: (i, 0)),
        pl.BlockSpec((rows, D), lambda i, j: (i * n_chunk + jnp.minimum(j, n_chunk - 1), 0)),
        pl.BlockSpec((None, D, tf), lambda i, j: (layer, 0, jnp.minimum(j, last))),
        pl.BlockSpec((None, D, tf), lambda i, j: (layer, 0, jnp.minimum(j, last))),
        pl.BlockSpec((None, tf, D), lambda i, j: (layer, jnp.minimum(j, last), 0)),
        pl.BlockSpec((None, D, tail), lambda i, j: (layer, 0, tail_blk), pipeline_mode=once),
        pl.BlockSpec((None, D, tail), lambda i, j: (layer, 0, tail_blk), pipeline_mode=once),
        pl.BlockSpec((None, tail, D), lambda i, j: (layer, tail_blk, 0), pipeline_mode=once),
        pl.BlockSpec((None, 1, D), lambda i, j: (gain_layer, 0, 0)),
    ]
    row_spec = pl.BlockSpec((tm, D), lambda i, j: (i, 0))
    if final:
        out_specs = row_spec
        out_shape = jax.ShapeDtypeStruct((T, D), F32)
    else:
        out_specs = [row_spec, row_spec]
        out_shape = [jax.ShapeDtypeStruct((T, D), F32), jax.ShapeDtypeStruct((T, D), BF16)]
    return pl.pallas_call(
        kern,
        grid=(T // tm, n_full + 1),
        in_specs=in_specs,
        out_specs=out_specs,
        out_shape=out_shape,
        compiler_params=_params(2),
        name="swiglu_half_step",
    )(h, x, w_gate, w_up, w_down, w_gate, w_up, w_down, gains)


def _inproj_kernel(scale_ref, h_ref, w_ref, o_ref, *, n_sub):
    j = pl.program_id(1)
    r = _dot(h_ref[...], w_ref[...].astype(BF16))
    for t in range(n_sub):
        o_ref[t] = (r[:, t * LANES:(t + 1) * LANES] * scale_ref[j * n_sub + t]).astype(BF16)


def _inproj_call(h, w_in, layer):
    T, D = h.shape
    tm, tn = 1024, 8 * LANES
    n_sub = tn // LANES
    n_blk = w_in.shape[-1] // tn
    assert n_blk * n_sub == N_SLABS
    scale = np.ones((N_SLABS,), np.float32)
    scale[SLAB_AQ:SLAB_AK] = ATTN_Q_SCALE
    scale[SLAB_BQ:SLAB_BK] = DIFF_Q_SCALE
    scale[SLAB_CQ:SLAB_CK] = ATTN_Q_SCALE
    return pl.pallas_call(
        functools.partial(_inproj_kernel, n_sub=n_sub),
        grid=(T // tm, n_blk),
        in_specs=[pl.BlockSpec(memory_space=pltpu.SMEM),
                  pl.BlockSpec((tm, D), lambda i, j: (i, 0)),
                  pl.BlockSpec((None, D, tn), lambda i, j: (layer, 0, j))],
        out_specs=pl.BlockSpec((n_sub, tm, LANES), lambda i, j: (j, i, 0)),
        out_shape=jax.ShapeDtypeStruct((N_SLABS, T, LANES), BF16),
        compiler_params=_params(2),
        name="input_projection",
    )(jnp.asarray(scale), h, w_in)


DIL_CLASSES = 4
DIL_OFFSETS = 3


def _dil_kernel(slopes_ref, q_ref, k_ref, v_ref, y_ref,
                stage, k4, v4, qf, q4, o1, l1, o4, l4, bias_a, bias_b, *, tq, seq, blk):
    h = pl.program_id(1)
    i = pl.program_id(2)
    slope2 = slopes_ref[h] * LOG2E
    (_, d1), (_, d4), (_, d16) = DIL_BRANCHES
    per4, per16 = tq // d4, tq // d16
    n1, n4, n16 = blk, min(blk, per4), min(blk, per16)
    win1 = min(n1 + 2 * DIL_RADIUS, seq // d1)
    win4 = min(n4 + 2 * DIL_RADIUS, seq // d4)
    win16 = min(n16 + 2 * DIL_RADIUS, seq // d16)

    def classes(ref, n):
        return [ref[pl.ds(r, n, stride=DIL_CLASSES), :] for r in range(DIL_CLASSES)]

    @pl.when(i == 0)
    def _():
        for src, dst in ((k_ref, k4), (v_ref, v4)):
            stage[...] = src[...].astype(F32)
            for r, rows in enumerate(classes(stage, seq // DIL_CLASSES)):
                dst[r] = rows
        for idx, (dil, n_q, win, ref) in enumerate(((d1, n1, win1, bias_a), (d4, n4, win4, bias_a),
                                                    (d16, n16, win16, bias_b))):
            row = lax.broadcasted_iota(jnp.int32, (n_q, win), 0)
            col = lax.broadcasted_iota(jnp.int32, (n_q, win), 1)
            for case in range(DIL_OFFSETS):
                absrel = jnp.abs(col - row - case * DIL_RADIUS)
                tile = jnp.where(absrel <= DIL_RADIUS, (-slope2 * dil) * absrel.astype(F32), NEG)
                if ref is bias_a:
                    ref[idx, case] = tile
                else:
                    ref[case] = tile

    qf[...] = q_ref[...].astype(F32)
    for r, rows in enumerate(classes(qf, tq // DIL_CLASSES)):
        q4[r] = rows

    def window(u0, sub_len, win):
        sub_start = jnp.clip(u0 - DIL_RADIUS, 0, sub_len - win)
        return sub_start, lax.div(u0 - sub_start, DIL_RADIUS)

    def softmax_block(s, v):
        m = jnp.max(s, axis=-1, keepdims=True)
        p = jnp.exp2(s - m)
        den = jnp.sum(p, axis=-1, keepdims=True)
        return _dot(p.astype(BF16), v) / den, m + jnp.log2(den)

    def merge(o, lse, o_prev, l_prev):
        lse = jnp.broadcast_to(lse, l_prev.shape)
        mx = jnp.maximum(l_prev, lse)
        e_prev, e_cur = jnp.exp2(l_prev - mx), jnp.exp2(lse - mx)
        tot = e_prev + e_cur
        return (e_prev * o_prev + e_cur * o) / tot, mx + jnp.log2(tot)

    blocks = []

    for b in range(tq // n1):
        def logits(b=b):
            start, case = window(i * tq + b * n1, seq // d1, win1)
            src = pl.ds(pl.multiple_of(start, DIL_RADIUS), win1)
            return _dot_nt(q_ref[b * n1:(b + 1) * n1, :], k_ref[src, :]) + bias_a[0, case], src

        def finish(s, src, b=b):
            o, lse = softmax_block(s, v_ref[src, :])
            o1[b * n1:(b + 1) * n1, :] = o
            l1[b * n1:(b + 1) * n1, :] = jnp.broadcast_to(lse, (n1, LANES))
            if b == tq // n1 - 1:
                for r in range(DIL_CLASSES):
                    o4[r] = o1[pl.ds(r, per4, stride=DIL_CLASSES), :]
                    l4[r] = l1[pl.ds(r, per4, stride=DIL_CLASSES), :]
        blocks.append((logits, finish))

    for r in range(d4):
        for b in range(per4 // n4):
            def logits(r=r, b=b):
                start, case = window(i * per4 + b * n4, seq // d4, win4)
                src = pl.ds(pl.multiple_of(start, DIL_RADIUS), win4)
                q = q4[r, b * n4:(b + 1) * n4, :].astype(BF16)
                return _dot_nt(q, k4[r, src, :].astype(BF16)) + bias_a[1, case], src

            def finish(s, src, r=r, b=b):
                dst = slice(b * n4, (b + 1) * n4)
                o, lse = softmax_block(s, v4[r, src, :].astype(BF16))
                o4[r, dst, :], l4[r, dst, :] = merge(o, lse, o4[r, dst, :], l4[r, dst, :])
            blocks.append((logits, finish))

    for r16 in range(d16):
        r, a = r16 % DIL_CLASSES, r16 // DIL_CLASSES
        for b in range(per16 // n16):
            def logits(r=r, a=a, b=b):
                start, case = window(i * per16 + b * n16, seq // d16, win16)
                src = pl.ds(a + DIL_CLASSES * start, win16, stride=DIL_CLASSES)
                q = q4[r, pl.ds(a + DIL_CLASSES * b * n16, n16, stride=DIL_CLASSES), :].astype(BF16)
                return _dot_nt(q, k4[r, src, :].astype(BF16)) + bias_b[case], src

            def finish(s, src, r=r, a=a, b=b):
                dst = pl.ds(a + DIL_CLASSES * b * n16, n16, stride=DIL_CLASSES)
                o, lse = softmax_block(s, v4[r, src, :].astype(BF16))
                o4[r, dst, :], l4[r, dst, :] = merge(o, lse, o4[r, dst, :], l4[r, dst, :])
            blocks.append((logits, finish))

    ahead = 6
    queue = [blocks[n][0]() for n in range(ahead)]
    for n in range(len(blocks)):
        if n + ahead < len(blocks):
            queue.append(blocks[n + ahead][0]())
        blocks[n][1](*queue.pop(0))

    for r in range(DIL_CLASSES):
        o1[pl.ds(r, per4, stride=DIL_CLASSES), :] = o4[r]
    y_ref[...] = o1[...].astype(BF16)


def _dilated_attention(qkv, slopes_a, batch, seq):
    T = batch * seq
    tq, blk = 1024, 128
    assert tuple(d for _, d in DIL_BRANCHES) == (1, DIL_CLASSES, DIL_CLASSES ** 2)
    assert tq % (8 * DIL_CLASSES ** 2) == 0 and seq % tq == 0
    per16 = tq // DIL_CLASSES ** 2
    qkv_v = qkv.reshape(N_SLABS, batch, seq, LANES)
    kern = functools.partial(_dil_kernel, tq=tq, seq=seq, blk=blk)
    cls = lambda n: pltpu.VMEM((DIL_CLASSES, n // DIL_CLASSES, LANES), F32)
    tile = lambda n: pltpu.VMEM((n, LANES), F32)
    win = blk + 2 * DIL_RADIUS
    out = pl.pallas_call(
        kern,
        grid=(batch, DIL_HEADS, seq // tq),
        in_specs=[
            pl.BlockSpec(memory_space=pltpu.SMEM),
            pl.BlockSpec((None, None, tq, LANES), lambda b, h, i: (SLAB_AQ + h, b, i, 0)),
            pl.BlockSpec((None, None, seq, LANES), lambda b, h, i: (SLAB_AK + h, b, 0, 0)),
            pl.BlockSpec((None, None, seq, LANES), lambda b, h, i: (SLAB_AV + h, b, 0, 0)),
        ],
        out_specs=pl.BlockSpec((None, None, tq, LANES), lambda b, h, i: (h, b, i, 0)),
        out_shape=jax.ShapeDtypeStruct((DIL_HEADS, batch, seq, LANES), BF16),
        scratch_shapes=[tile(seq), cls(seq), cls(seq), tile(tq), cls(tq), tile(tq), tile(tq), cls(tq), cls(tq),
                        pltpu.VMEM((2, DIL_OFFSETS, blk, win), F32),
                        pltpu.VMEM((DIL_OFFSETS, per16, min(per16 + 2 * DIL_RADIUS, seq // DIL_CLASSES ** 2)), F32)],
        compiler_params=_params(3),
        name="dilated_attention",
    )(slopes_a, qkv_v, qkv_v, qkv_v)
    return out.reshape(DIL_HEADS, T, LANES)


DIFF_FEAT = 3
DIFF_VT_ROWS = HEAD_DIM + 16


def _diff_kernel(slopes_ref, q_ref, k_ref, v_ref, lq1_ref, lk1_ref, lq2_ref, lk2_ref, sg_ref,
                 y_ref, ka_ref, vt_ref, qt_ref, *, tq, tk, seq, lambda_init):
    h = pl.program_id(1)
    i = pl.program_id(2)
    slope2 = slopes_ref[h] * LOG2E
    n_kb = seq // tk
    half = DIFF_QK_DIM

    @pl.when(i == 0)
    def _():
        lane = lax.broadcasted_iota(jnp.int32, (tk, LANES), 1)
        k_local = lax.broadcasted_iota(jnp.int32, (tk, LANES), 0)
        rem = slope2 * k_local.astype(F32)
        parts = []
        for _ in range(DIFF_FEAT):
            part = rem.astype(BF16).astype(F32)
            parts.append(part)
            rem = rem - part
        ones_row = (lax.broadcasted_iota(jnp.int32, (DIFF_VT_ROWS - HEAD_DIM, tk), 0) == 0)
        for c in range(n_kb):
            kf = k_ref[c * tk:(c + 1) * tk, :].astype(F32)
            for mp in range(2):
                lo, fo = mp * half, half - mp * half
                ka = jnp.where((lane >= lo) & (lane < lo + half), kf, 0.0)
                for t, part in enumerate(parts):
                    ka = jnp.where(lane == fo + t, part, ka)
                ka_ref[mp, c * tk:(c + 1) * tk, :] = ka.astype(BF16)
            vt_ref[c, :HEAD_DIM, :] = v_ref[c * tk:(c + 1) * tk, :].astype(F32).T.astype(BF16)
            vt_ref[c, HEAD_DIM:, :] = ones_row.astype(BF16)

    qt = q_ref[...].astype(F32).T
    row = lax.broadcasted_iota(jnp.int32, (LANES, tq), 0)
    for mp in range(2):
        lo, fo = mp * half, half - mp * half
        own = jnp.where((row >= lo) & (row < lo + half), qt, 0.0)
        feat = (row >= fo) & (row < fo + DIFF_FEAT)
        qt_ref[0, mp] = jnp.where(feat, 1.0, own).astype(BF16)
        qt_ref[1, mp] = jnp.where(feat, -1.0, own).astype(BF16)
        qt_ref[2, mp] = own.astype(BF16)

    q0 = i * tq
    kb_diag = q0 // tk
    q_local = lax.broadcasted_iota(jnp.int32, (1, tq), 1).astype(F32)

    def logits(j):
        kb = kb_diag + j
        kb = jnp.where(kb >= n_kb, kb - n_kb, kb)
        k0 = pl.multiple_of(kb * tk, tk)
        if j == 0:
            krow = lax.broadcasted_iota(jnp.int32, (tk, tq), 0)
            qcol = lax.broadcasted_iota(jnp.int32, (tk, tq), 1)
            tile = slope2 * jnp.abs((qcol - krow) + (q0 - k0)).astype(F32)
            return kb, 0.0, [_dot(ka_ref[mp, pl.ds(k0, tk), :], qt_ref[2, mp]) - tile for mp in range(2)]
        right = (kb > kb_diag).astype(jnp.int32)
        sgn = (1 - 2 * right).astype(F32)
        row_term = (-sgn * slope2) * ((q0 - k0).astype(F32) + q_local)
        return kb, row_term, [_dot(ka_ref[mp, pl.ds(k0, tk), :], qt_ref[right, mp]) for mp in range(2)]

    def local_softmax(kb, row_term, s_maps):
        vt = vt_ref[kb]
        out = []
        for s in s_maps:
            m = jnp.max(s, axis=0, keepdims=True)
            p = jnp.exp2(s - m).astype(BF16)
            out.append((m + row_term, _dot(vt, p)))
        return out

    state = None
    ahead = 3
    queue = [logits(j) for j in range(min(ahead, n_kb))]
    for j in range(n_kb):
        cur = queue.pop(0)
        if j + ahead < n_kb:
            queue.append(logits(j + ahead))
        part = local_softmax(*cur)
        if state is None:
            state = part
            continue
        merged = []
        for (m, acc), (m_b, acc_b) in zip(state, part):
            m_new = jnp.maximum(m, m_b)
            merged.append((m_new, jnp.exp2(m - m_new) * acc + jnp.exp2(m_b - m_new) * acc_b))
        state = merged
    (_, acc0), (_, acc1) = state

    lam = (jnp.exp(jnp.sum(lq1_ref[...] * lk1_ref[...], axis=-1, keepdims=True))
           - jnp.exp(jnp.sum(lq2_ref[...] * lk2_ref[...], axis=-1, keepdims=True))
           + lambda_init)
    o0 = acc0[:HEAD_DIM] / acc0[HEAD_DIM:HEAD_DIM + 1]
    o1 = acc1[:HEAD_DIM] / acc1[HEAD_DIM:HEAD_DIM + 1]
    o = (o0 - lam * o1).T
    y_ref[...] = (_rms(o, sg_ref[...]) * (1.0 - lambda_init)).astype(BF16)


def _diff_call(qkv, slopes_b, lq1, lk1, lq2, lk2, subln, layer, lambda_init, batch, seq):
    T = batch * seq
    tq, tk = 512, 512
    assert tk % tq == 0 and seq % tk == 0
    qkv_v = qkv.reshape(N_SLABS, batch, seq, LANES)
    vec = lambda a: a.reshape(a.shape[0], 1, a.shape[1])
    lam_spec = pl.BlockSpec((None, 1, DIFF_QK_DIM), lambda b, h, i: (layer, 0, 0))
    kern = functools.partial(_diff_kernel, tq=tq, tk=tk, seq=seq, lambda_init=lambda_init)
    out = pl.pallas_call(
        kern,
        grid=(batch, DIFF_HEADS, seq // tq),
        in_specs=[
            pl.BlockSpec(memory_space=pltpu.SMEM),
            pl.BlockSpec((None, None, tq, LANES), lambda b, h, i: (SLAB_BQ + h, b, i, 0)),
            pl.BlockSpec((None, None, seq, LANES), lambda b, h, i: (SLAB_BK + h, b, 0, 0)),
            pl.BlockSpec((None, None, seq, LANES), lambda b, h, i: (SLAB_BV + h, b, 0, 0)),
            lam_spec, lam_spec, lam_spec, lam_spec,
            pl.BlockSpec((None, 1, HEAD_DIM), lambda b, h, i: (layer, 0, 0)),
        ],
        out_specs=pl.BlockSpec((None, None, tq, LANES), lambda b, h, i: (h, b, i, 0)),
        out_shape=jax.ShapeDtypeStruct((DIFF_HEADS, batch, seq, LANES), BF16),
        scratch_shapes=[pltpu.VMEM((2, seq, LANES), BF16), pltpu.VMEM((seq // tk, DIFF_VT_ROWS, tk), BF16),
                        pltpu.VMEM((3, 2, LANES, tq), BF16)],
        compiler_params=_params(3),
        name="differential_attention",
    )(slopes_b, qkv_v, qkv_v, qkv_v, vec(lq1), vec(lk1), vec(lq2), vec(lk2), vec(subln))
    return out.reshape(DIFF_HEADS, T, LANES)


def _swa_kernel(slopes_ref, sink_ref, q_ref, k_ref, v_ref, y_ref, *, tq, win, seq, layer):
    g = pl.program_id(1)
    row = lax.broadcasted_iota(jnp.int32, (tq, win), 0)
    col = lax.broadcasted_iota(jnp.int32, (tq, win), 1)
    col_minus_row = col - row
    heads = [g * SWA_GROUP + r for r in range(SWA_GROUP)]
    slopes2 = [slopes_ref[hd] * LOG2E for hd in heads]
    sinks2 = [sink_ref[layer * SWA_Q_HEADS + hd] * LOG2E for hd in heads]

    def tile(i, carry):
        q0 = pl.multiple_of(i * tq, tq)
        start = pl.multiple_of(jnp.clip(q0 - SWA_RADIUS, 0, seq - win), SWA_RADIUS)
        absrel = jnp.abs((start - q0) + col_minus_row)
        valid = absrel <= SWA_RADIUS
        dist = absrel.astype(F32)
        k = k_ref[pl.ds(start, win), :]
        v = v_ref[pl.ds(start, win), :]
        logits = [_dot_nt(q_ref[r, pl.ds(q0, tq), :], k) for r in range(SWA_GROUP)]
        for r in range(SWA_GROUP):
            s = jnp.where(valid, logits[r] - slopes2[r] * dist, NEG)
            m = jnp.maximum(jnp.max(s, axis=-1, keepdims=True), sinks2[r])
            p = jnp.exp2(s - m)
            den = jnp.sum(p, axis=-1, keepdims=True) + jnp.exp2(sinks2[r] - m)
            y_ref[r, pl.ds(q0, tq), :] = (_dot(p.astype(BF16), v) / den).astype(BF16)
        return carry

    lax.fori_loop(0, seq // tq, tile, 0)


def _swa_call(qkv, slopes_c, sink_flat, layer, batch, seq):
    T = batch * seq
    tq = 256
    win = tq + 2 * SWA_RADIUS
    qkv_v = qkv.reshape(N_SLABS, batch, seq, LANES)
    kern = functools.partial(_swa_kernel, tq=tq, win=win, seq=seq, layer=layer)
    out = pl.pallas_call(
        kern,
        grid=(batch, SWA_KV_HEADS),
        in_specs=[
            pl.BlockSpec(memory_space=pltpu.SMEM),
            pl.BlockSpec(memory_space=pltpu.SMEM),
            pl.BlockSpec((SWA_GROUP, None, seq, LANES), lambda b, g: (SLAB_CQ // SWA_GROUP + g, b, 0, 0)),
            pl.BlockSpec((None, None, seq, LANES), lambda b, g: (SLAB_CK + g, b, 0, 0)),
            pl.BlockSpec((None, None, seq, LANES), lambda b, g: (SLAB_CV + g, b, 0, 0)),
        ],
        out_specs=pl.BlockSpec((SWA_GROUP, None, seq, LANES), lambda b, g: (g, b, 0, 0)),
        out_shape=jax.ShapeDtypeStruct((SWA_Q_HEADS, batch, seq, LANES), BF16),
        compiler_params=_params(2),
        name="window_gqa_attention",
    )(slopes_c, sink_flat, qkv_v, qkv_v, qkv_v)
    return out.reshape(SWA_Q_HEADS, T, LANES)


def _outproj_kernel(ya_ref, yb_ref, yc_ref, x_ref, w_ref, gn_ref, o_ref, hn_ref, wb_ref, *, cast_rows):
    @pl.when(pl.program_id(0) == 0)
    def _():
        for r0 in range(0, wb_ref.shape[0], cast_rows):
            wb_ref[r0:r0 + cast_rows, :] = w_ref[r0:r0 + cast_rows, :].astype(BF16)

    slabs = [y_ref[s] for y_ref in (ya_ref, yb_ref, yc_ref) for s in range(y_ref.shape[0])]
    y = jnp.concatenate(slabs, axis=-1)
    xn = x_ref[...] + _dot(y, wb_ref[...])
    o_ref[...] = xn
    hn_ref[...] = _rms(xn, gn_ref[...]).astype(BF16)


def _outproj_call(ya, yb, yc, x, w_out, layer, gains, gain_layer):
    T, D = x.shape
    tm = 512
    row_spec = pl.BlockSpec((tm, D), lambda i: (i, 0))
    slab_spec = lambda y: pl.BlockSpec((y.shape[0], tm, LANES), lambda i: (0, i, 0))
    return pl.pallas_call(
        functools.partial(_outproj_kernel, cast_rows=256),
        grid=(T // tm,),
        in_specs=[
            slab_spec(ya), slab_spec(yb), slab_spec(yc),
            row_spec,
            pl.BlockSpec((None,) + w_out.shape[1:], lambda i: (layer, 0, 0), pipeline_mode=pl.Buffered(1)),
            pl.BlockSpec((None, 1, D), lambda i: (gain_layer, 0, 0)),
        ],
        out_specs=[row_spec, row_spec],
        out_shape=[jax.ShapeDtypeStruct((T, D), F32), jax.ShapeDtypeStruct((T, D), BF16)],
        scratch_shapes=[pltpu.VMEM(w_out.shape[1:], BF16)],
        compiler_params=_params(1),
        name="output_projection",
    )(ya, yb, yc, x, w_out, gains)


def kernel(x, ffn1_norm, ffn1_w_gate, ffn1_w_up, ffn1_w_down, mix_norm, w_in, w_out, diff_lambda_q1, diff_lambda_k1, diff_lambda_q2, diff_lambda_k2, diff_subln, swa_sink, ffn2_norm, ffn2_w_gate, ffn2_w_up, ffn2_w_down, final_norm):
    batch, seq, d_model = x.shape
    depth = w_in.shape[0]
    xf = x.reshape(batch * seq, d_model)
    gain3 = lambda a: a.reshape(a.shape[0], 1, a.shape[1])
    ffn1_g, mix_g, ffn2_g = gain3(ffn1_norm), gain3(mix_norm), gain3(ffn2_norm)
    final_g = final_norm.reshape(1, 1, d_model)
    slopes_a = jnp.asarray(SLOPES_A, F32)
    slopes_b = jnp.asarray(SLOPES_B, F32)
    slopes_c = jnp.asarray(SLOPES_C, F32)
    sink_flat = swa_sink.astype(F32).reshape(-1)

    h = _norm_call(xf, ffn1_g, 0)
    for l in range(depth):
        lambda_init = 0.8 - 0.6 * math.exp(-0.3 * l)
        xf, h = _ffn_call(h, xf, ffn1_w_gate, ffn1_w_up, ffn1_w_down, l, mix_g, l, final=False)
        qkv = _inproj_call(h, w_in, l)
        ya = _dilated_attention(qkv, slopes_a, batch, seq)
        yb = _diff_call(qkv, slopes_b, diff_lambda_q1, diff_lambda_k1, diff_lambda_q2, diff_lambda_k2,
                        diff_subln, l, lambda_init, batch, seq)
        yc = _swa_call(qkv, slopes_c, sink_flat, l, batch, seq)
        xf, h = _outproj_call(ya, yb, yc, xf, w_out, l, ffn2_g, l)
        if l + 1 < depth:
            xf, h = _ffn_call(h, xf, ffn2_w_gate, ffn2_w_up, ffn2_w_down, l, ffn1_g, l + 1, final=False)
        else:
            xf = _ffn_call(h, xf, ffn2_w_gate, ffn2_w_up, ffn2_w_down, l, final_g, 0, final=True)
    return xf.reshape(batch, seq, d_model)
```

```python
import functools
import math

import numpy as np
import jax
import jax.numpy as jnp
from jax import lax
from jax.experimental import pallas as pl
from jax.experimental.pallas import tpu as pltpu

F32 = jnp.float32
BF16 = jnp.bfloat16

HEAD_DIM = 128
DIL_HEADS = 6
DIL_BRANCHES = ((128, 1), (512, 4), (2048, 16))
DIL_RADIUS = 64
DIFF_HEADS = 4
DIFF_QK_DIM = HEAD_DIM // 2
SWA_Q_HEADS = 6
SWA_KV_HEADS = 2
SWA_GROUP = SWA_Q_HEADS // SWA_KV_HEADS
SWA_RADIUS = 128
N_ALIBI_HEADS = SWA_Q_HEADS + DIL_HEADS + DIFF_HEADS
RMS_EPS = 1e-6
NEG = -1e30
ATTN_SCALE = HEAD_DIM ** -0.5
LOG2E = math.log2(math.e)
ATTN_Q_SCALE = ATTN_SCALE * LOG2E
DIFF_Q_SCALE = DIFF_QK_DIM ** -0.5 * LOG2E

SLAB_AQ, SLAB_AK, SLAB_AV = 0, 6, 12
SLAB_BQ, SLAB_BK, SLAB_BV = 18, 22, 26
SLAB_CQ, SLAB_CK, SLAB_CV = 30, 36, 38
N_SLABS = 40

LANES = 128
VMEM_LIMIT_BYTES = 60000 * 1024

_SLOPES = [float(np.float32(2.0 ** (-8.0 * (n + 1) / N_ALIBI_HEADS))) for n in range(N_ALIBI_HEADS)]
SLOPES_C = _SLOPES[:SWA_Q_HEADS]
SLOPES_A = _SLOPES[SWA_Q_HEADS:SWA_Q_HEADS + DIL_HEADS]
SLOPES_B = _SLOPES[SWA_Q_HEADS + DIL_HEADS:]


def _params(n_grid, flags=None):
    return pltpu.CompilerParams(dimension_semantics=("arbitrary",) * n_grid,
                                vmem_limit_bytes=VMEM_LIMIT_BYTES, flags=flags)


def _rms(x, g):
    ms = jnp.mean(x * x, axis=-1, keepdims=True)
    return x * lax.rsqrt(ms + RMS_EPS) * g


def _dot(a, b):
    return jnp.dot(a, b, preferred_element_type=F32)


def _dot_nt(a, b):
    return lax.dot_general(a, b, (((1,), (1,)), ((), ())), preferred_element_type=F32)


def _norm_kernel(x_ref, g_ref, o_ref):
    o_ref[...] = _rms(x_ref[...], g_ref[...]).astype(BF16)


def _norm_call(x, gains, layer):
    T, D = x.shape
    tm = 512
    return pl.pallas_call(
        _norm_kernel,
        grid=(T // tm,),
        in_specs=[pl.BlockSpec((tm, D), lambda i: (i, 0)),
                  pl.BlockSpec((None, 1, D), lambda i: (layer, 0, 0))],
        out_specs=pl.BlockSpec((tm, D), lambda i: (i, 0)),
        out_shape=jax.ShapeDtypeStruct((T, D), BF16),
        compiler_params=_params(1),
        name="rms_norm_bf16",
    )(x, gains)


def _ffn_kernel(h_ref, x_ref, wg_ref, wu_ref, wd_ref, wgut_ref, wdt_ref, gn_ref,
                *out_refs, n_full, n_chunk, rows, final):
    o_ref = out_refs[0]
    j = pl.program_id(1)

    def swiglu(g, u):
        return (0.5 * (g * jax.nn.sigmoid(g)) * u).astype(BF16)

    @pl.when(j == 0)
    def _():
        tail = wdt_ref.shape[0]
        w_gu, w_d = wgut_ref[...].astype(BF16), wdt_ref[...].astype(BF16)
        half = h_ref.shape[0] // 2
        gus = [_dot(h_ref[r0:r0 + half, :], w_gu) for r0 in (0, half)]
        for r0, gu in zip((0, half), gus):
            o_ref[r0:r0 + half, :] = _dot(swiglu(gu[:, :tail], gu[:, tail:]), w_d)

    @pl.when((j > 0) & (j < n_full))
    def _():
        h = h_ref[...]
        act = swiglu(_dot(h, wg_ref[...].astype(BF16)), _dot(h, wu_ref[...].astype(BF16)))
        o_ref[...] += _dot(act, wd_ref[...].astype(BF16))

    @pl.when(j < n_chunk)
    def _():
        r0 = pl.multiple_of(j * rows, rows)
        o_ref[pl.ds(r0, rows), :] += x_ref[...]

    @pl.when(j == n_full)
    def _():
        w_g, w_u, w_d = wg_ref[...].astype(BF16), wu_ref[...].astype(BF16), wd_ref[...].astype(BF16)
        half = h_ref.shape[0] // 2
        acts = [swiglu(_dot(h_ref[r0:r0 + half, :], w_g), _dot(h_ref[r0:r0 + half, :], w_u))
                for r0 in (0, half)]
        for r0, act in zip((0, half), acts):
            xn = o_ref[r0:r0 + half, :] + _dot(act, w_d)
            if final:
                o_ref[r0:r0 + half, :] = _rms(xn, gn_ref[...])
            else:
                o_ref[r0:r0 + half, :] = xn
                out_refs[1][r0:r0 + half, :] = _rms(xn, gn_ref[...]).astype(BF16)


def _ffn_call(h, x, w_gate, w_up, w_down, layer, gains, gain_layer, final):
    T, D = x.shape
    d_ff = w_gate.shape[-1]
    tm, tf, tail = 1024, 256, 128
    n_full = d_ff // tf
    assert n_full * tf + tail == d_ff
    tail_blk = (n_full * tf) // tail
    n_chunk = 8
    rows = tm // n_chunk
    kern = functools.partial(_ffn_kernel, n_full=n_full, n_chunk=n_chunk, rows=rows, final=final)
    once = pl.Buffered(1)
    full = lambda j: jnp.maximum(j - 1, 0)
    w_gu_tail = jnp.concatenate([w_gate[layer, :, n_full * tf:], w_up[layer, :, n_full * tf:]], axis=-1)
    in_specs = [
        pl.BlockSpec((tm, D), lambda i, j: (i, 0)),
        pl.BlockSpec((rows, D), lambda i, j: (i * n_chunk + jnp.minimum(j, n_chunk - 1), 0)),
        pl.BlockSpec((None, D, tf), lambda i, j: (layer, 0, full(j))),
        pl.BlockSpec((None, D, tf), lambda i, j: (layer, 0, full(j))),
        pl.BlockSpec((None, tf, D), lambda i, j: (layer, full(j), 0)),
        pl.BlockSpec((D, 2 * tail), lambda i, j: (0, 0), pipeline_mode=once),
        pl.BlockSpec((None, tail, D), lambda i, j: (layer, tail_blk, 0), pipeline_mode=once),
        pl.BlockSpec((None, 1, D), lambda i, j: (gain_layer, 0, 0)),
    ]
    row_spec = pl.BlockSpec((tm, D), lambda i, j: (i, 0))
    if final:
        out_specs = row_spec
        out_shape = jax.ShapeDtypeStruct((T, D), F32)
    else:
        out_specs = [row_spec, row_spec]
        out_shape = [jax.ShapeDtypeStruct((T, D), F32), jax.ShapeDtypeStruct((T, D), BF16)]
    return pl.pallas_call(
        kern,
        grid=(T // tm, n_full + 1),
        in_specs=in_specs,
        out_specs=out_specs,
        out_shape=out_shape,
        compiler_params=_params(2),
        name="swiglu_half_step",
    )(h, x, w_gate, w_up, w_down, w_gu_tail, w_down, gains)


def _inproj_kernel(scale_ref, h_ref, w_ref, o_ref, *, n_sub):
    j = pl.program_id(1)
    r = _dot(h_ref[...], w_ref[...].astype(BF16))
    for t in range(n_sub):
        o_ref[t] = (r[:, t * LANES:(t + 1) * LANES] * scale_ref[j * n_sub + t]).astype(BF16)


def _inproj_call(h, w_in, layer):
    T, D = h.shape
    tm, tn = 1024, 8 * LANES
    n_sub = tn // LANES
    n_blk = w_in.shape[-1] // tn
    assert n_blk * n_sub == N_SLABS
    scale = np.ones((N_SLABS,), np.float32)
    scale[SLAB_AQ:SLAB_AK] = ATTN_Q_SCALE
    scale[SLAB_BQ:SLAB_BK] = DIFF_Q_SCALE
    scale[SLAB_CQ:SLAB_CK] = ATTN_Q_SCALE
    return pl.pallas_call(
        functools.partial(_inproj_kernel, n_sub=n_sub),
        grid=(T // tm, n_blk),
        in_specs=[pl.BlockSpec(memory_space=pltpu.SMEM),
                  pl.BlockSpec((tm, D), lambda i, j: (i, 0)),
                  pl.BlockSpec((None, D, tn), lambda i, j: (layer, 0, j))],
        out_specs=pl.BlockSpec((n_sub, tm, LANES), lambda i, j: (j, i, 0)),
        out_shape=jax.ShapeDtypeStruct((N_SLABS, T, LANES), BF16),
        compiler_params=_params(2),
        name="input_projection",
    )(jnp.asarray(scale), h, w_in)


DIL_CLASSES = 4
DIL_OFFSETS = 3


def _dil_kernel(slopes_ref, q_ref, k_ref, v_ref, y_ref,
                stage, k4, v4, qf, q4, o1, l1, o4, l4, bias_a, bias_b, *, tq, seq, blk):
    h = pl.program_id(1)
    i = pl.program_id(2)
    slope2 = slopes_ref[h] * LOG2E
    (_, d1), (_, d4), (_, d16) = DIL_BRANCHES
    per4, per16 = tq // d4, tq // d16
    n1, n4, n16 = blk, min(blk, per4), min(blk, per16)
    win1 = min(n1 + 2 * DIL_RADIUS, seq // d1)
    win4 = min(n4 + 2 * DIL_RADIUS, seq // d4)
    win16 = min(n16 + 2 * DIL_RADIUS, seq // d16)

    def classes(ref, n):
        return [ref[pl.ds(r, n, stride=DIL_CLASSES), :] for r in range(DIL_CLASSES)]

    @pl.when(i == 0)
    def _():
        for src, dst in ((k_ref, k4), (v_ref, v4)):
            stage[...] = src[...].astype(F32)
            for r, rows in enumerate(classes(stage, seq // DIL_CLASSES)):
                dst[r] = rows
        for idx, (dil, n_q, win, ref) in enumerate(((d1, n1, win1, bias_a), (d4, n4, win4, bias_a),
                                                    (d16, n16, win16, bias_b))):
            row = lax.broadcasted_iota(jnp.int32, (n_q, win), 0)
            col = lax.broadcasted_iota(jnp.int32, (n_q, win), 1)
            for case in range(DIL_OFFSETS):
                absrel = jnp.abs(col - row - case * DIL_RADIUS)
                tile = jnp.where(absrel <= DIL_RADIUS, (-slope2 * dil) * absrel.astype(F32), NEG)
                if ref is bias_a:
                    ref[idx, case] = tile
                else:
                    ref[case] = tile

    qf[...] = q_ref[...].astype(F32)
    for r, rows in enumerate(classes(qf, tq // DIL_CLASSES)):
        q4[r] = rows

    def window(u0, sub_len, win):
        sub_start = jnp.clip(u0 - DIL_RADIUS, 0, sub_len - win)
        return sub_start, lax.div(u0 - sub_start, DIL_RADIUS)

    def softmax_block(s, v):
        m = jnp.max(s, axis=-1, keepdims=True)
        p = jnp.exp2(s - m)
        den = jnp.sum(p, axis=-1, keepdims=True)
        return _dot(p.astype(BF16), v) / den, m + jnp.log2(den)

    def merge(o, lse, o_prev, l_prev):
        lse = jnp.broadcast_to(lse, l_prev.shape)
        mx = jnp.maximum(l_prev, lse)
        e_prev, e_cur = jnp.exp2(l_prev - mx), jnp.exp2(lse - mx)
        tot = e_prev + e_cur
        return (e_prev * o_prev + e_cur * o) / tot, mx + jnp.log2(tot)

    blocks = []

    for b in range(tq // n1):
        def logits(b=b):
            start, case = window(i * tq + b * n1, seq // d1, win1)
            src = pl.ds(pl.multiple_of(start, DIL_RADIUS), win1)
            return _dot_nt(q_ref[b * n1:(b + 1) * n1, :], k_ref[src, :]) + bias_a[0, case], src

        def finish(s, src, b=b):
            o, lse = softmax_block(s, v_ref[src, :])
            o1[b * n1:(b + 1) * n1, :] = o
            l1[b * n1:(b + 1) * n1, :] = jnp.broadcast_to(lse, (n1, LANES))
            if b == tq // n1 - 1:
                for r in range(DIL_CLASSES):
                    o4[r] = o1[pl.ds(r, per4, stride=DIL_CLASSES), :]
                    l4[r] = l1[pl.ds(r, per4, stride=DIL_CLASSES), :]
        blocks.append((logits, finish))

    for r in range(d4):
        for b in range(per4 // n4):
            def logits(r=r, b=b):
                start, case = window(i * per4 + b * n4, seq // d4, win4)
                src = pl.ds(pl.multiple_of(start, DIL_RADIUS), win4)
                q = q4[r, b * n4:(b + 1) * n4, :].astype(BF16)
                return _dot_nt(q, k4[r, src, :].astype(BF16)) + bias_a[1, case], src

            def finish(s, src, r=r, b=b):
                dst = slice(b * n4, (b + 1) * n4)
                o, lse = softmax_block(s, v4[r, src, :].astype(BF16))
                o4[r, dst, :], l4[r, dst, :] = merge(o, lse, o4[r, dst, :], l4[r, dst, :])
            blocks.append((logits, finish))

    for r16 in range(d16):
        r, a = r16 % DIL_CLASSES, r16 // DIL_CLASSES
        for b in range(per16 // n16):
            def logits(r=r, a=a, b=b):
                start, case = window(i * per16 + b * n16, seq // d16, win16)
                src = pl.ds(a + DIL_CLASSES * start, win16, stride=DIL_CLASSES)
                q = q4[r, pl.ds(a + DIL_CLASSES * b * n16, n16, stride=DIL_CLASSES), :].astype(BF16)
                return _dot_nt(q, k4[r, src, :].astype(BF16)) + bias_b[case], src

            def finish(s, src, r=r, a=a, b=b):
                dst = pl.ds(a + DIL_CLASSES * b * n16, n16, stride=DIL_CLASSES)
                o, lse = softmax_block(s, v4[r, src, :].astype(BF16))
                o4[r, dst, :], l4[r, dst, :] = merge(o, lse, o4[r, dst, :], l4[r, dst, :])
            blocks.append((logits, finish))

    ahead = 6
    queue = [blocks[n][0]() for n in range(ahead)]
    for n in range(len(blocks)):
        if n + ahead < len(blocks):
            queue.append(blocks[n + ahead][0]())
        blocks[n][1](*queue.pop(0))

    for r in range(DIL_CLASSES):
        o1[pl.ds(r, per4, stride=DIL_CLASSES), :] = o4[r]
    y_ref[...] = o1[...].astype(BF16)


def _dilated_attention(qkv, slopes_a, batch, seq):
    T = batch * seq
    tq, blk = 1024, 128
    assert tuple(d for _, d in DIL_BRANCHES) == (1, DIL_CLASSES, DIL_CLASSES ** 2)
    assert tq % (8 * DIL_CLASSES ** 2) == 0 and seq % tq == 0
    per16 = tq // DIL_CLASSES ** 2
    qkv_v = qkv.reshape(N_SLABS, batch, seq, LANES)
    kern = functools.partial(_dil_kernel, tq=tq, seq=seq, blk=blk)
    cls = lambda n: pltpu.VMEM((DIL_CLASSES, n // DIL_CLASSES, LANES), F32)
    tile = lambda n: pltpu.VMEM((n, LANES), F32)
    win = blk + 2 * DIL_RADIUS
    out = pl.pallas_call(
        kern,
        grid=(batch, DIL_HEADS, seq // tq),
        in_specs=[
            pl.BlockSpec(memory_space=pltpu.SMEM),
            pl.BlockSpec((None, None, tq, LANES), lambda b, h, i: (SLAB_AQ + h, b, i, 0)),
            pl.BlockSpec((None, None, seq, LANES), lambda b, h, i: (SLAB_AK + h, b, 0, 0)),
            pl.BlockSpec((None, None, seq, LANES), lambda b, h, i: (SLAB_AV + h, b, 0, 0)),
        ],
        out_specs=pl.BlockSpec((None, None, tq, LANES), lambda b, h, i: (h, b, i, 0)),
        out_shape=jax.ShapeDtypeStruct((DIL_HEADS, batch, seq, LANES), BF16),
        scratch_shapes=[tile(seq), cls(seq), cls(seq), tile(tq), cls(tq), tile(tq), tile(tq), cls(tq), cls(tq),
                        pltpu.VMEM((2, DIL_OFFSETS, blk, win), F32),
                        pltpu.VMEM((DIL_OFFSETS, per16, min(per16 + 2 * DIL_RADIUS, seq // DIL_CLASSES ** 2)), F32)],
        compiler_params=_params(3),
        name="dilated_attention",
    )(slopes_a, qkv_v, qkv_v, qkv_v)
    return out.reshape(DIL_HEADS, T, LANES)


DIFF_FEAT = 3
DIFF_VT_ROWS = HEAD_DIM + 16


def _diff_kernel(slopes_ref, q_ref, k_ref, v_ref, lq1_ref, lk1_ref, lq2_ref, lk2_ref, sg_ref,
                 y_ref, ka_ref, vt_ref, qt_ref, *, tq, tk, seq, lambda_init):
    h = pl.program_id(1)
    i = pl.program_id(2)
    slope2 = slopes_ref[h] * LOG2E
    n_kb = seq // tk
    half = DIFF_QK_DIM

    @pl.when(i == 0)
    def _():
        lane = lax.broadcasted_iota(jnp.int32, (tk, LANES), 1)
        k_local = lax.broadcasted_iota(jnp.int32, (tk, LANES), 0)
        rem = slope2 * k_local.astype(F32)
        parts = []
        for _ in range(DIFF_FEAT):
            part = rem.astype(BF16).astype(F32)
            parts.append(part)
            rem = rem - part
        ones_row = (lax.broadcasted_iota(jnp.int32, (DIFF_VT_ROWS - HEAD_DIM, tk), 0) == 0)
        for c in range(n_kb):
            kf = k_ref[c * tk:(c + 1) * tk, :].astype(F32)
            for mp in range(2):
                lo, fo = mp * half, half - mp * half
                ka = jnp.where((lane >= lo) & (lane < lo + half), kf, 0.0)
                for t, part in enumerate(parts):
                    ka = jnp.where(lane == fo + t, part, ka)
                ka_ref[mp, c * tk:(c + 1) * tk, :] = ka.astype(BF16)
            vt_ref[c, :HEAD_DIM, :] = v_ref[c * tk:(c + 1) * tk, :].astype(F32).T.astype(BF16)
            vt_ref[c, HEAD_DIM:, :] = ones_row.astype(BF16)

    qt = q_ref[...].astype(F32).T
    row = lax.broadcasted_iota(jnp.int32, (LANES, tq), 0)
    for mp in range(2):
        lo, fo = mp * half, half - mp * half
        own = jnp.where((row >= lo) & (row < lo + half), qt, 0.0)
        feat = (row >= fo) & (row < fo + DIFF_FEAT)
        qt_ref[0, mp] = jnp.where(feat, 1.0, own).astype(BF16)
        qt_ref[1, mp] = jnp.where(feat, -1.0, own).astype(BF16)
        qt_ref[2, mp] = own.astype(BF16)

    q0 = i * tq
    kb_diag = q0 // tk
    q_local = lax.broadcasted_iota(jnp.int32, (1, tq), 1).astype(F32)

    def logits(j):
        kb = kb_diag + j
        kb = jnp.where(kb >= n_kb, kb - n_kb, kb)
        k0 = pl.multiple_of(kb * tk, tk)
        if j < max(1, tq // tk):
            krow = lax.broadcasted_iota(jnp.int32, (tk, tq), 0)
            qcol = lax.broadcasted_iota(jnp.int32, (tk, tq), 1)
            tile = slope2 * jnp.abs((qcol - krow) + (q0 - k0)).astype(F32)
            return kb, 0.0, [_dot(ka_ref[mp, pl.ds(k0, tk), :], qt_ref[2, mp]) - tile for mp in range(2)]
        right = (kb > kb_diag).astype(jnp.int32)
        sgn = (1 - 2 * right).astype(F32)
        row_term = (-sgn * slope2) * ((q0 - k0).astype(F32) + q_local)
        return kb, row_term, [_dot(ka_ref[mp, pl.ds(k0, tk), :], qt_ref[right, mp]) for mp in range(2)]

    def local_softmax(kb, row_term, s_maps):
        vt = vt_ref[kb]
        out = []
        for s in s_maps:
            m = jnp.max(s, axis=0, keepdims=True)
            p = jnp.exp2(s - m).astype(BF16)
            out.append((m + row_term, _dot(vt, p)))
        return out

    state = None
    ahead = 3
    queue = [logits(j) for j in range(min(ahead, n_kb))]
    for j in range(n_kb):
        cur = queue.pop(0)
        if j + ahead < n_kb:
            queue.append(logits(j + ahead))
        part = local_softmax(*cur)
        if state is None:
            state = part
            continue
        merged = []
        for (m, acc), (m_b, acc_b) in zip(state, part):
            m_new = jnp.maximum(m, m_b)
            merged.append((m_new, jnp.exp2(m - m_new) * acc + jnp.exp2(m_b - m_new) * acc_b))
        state = merged
    (_, acc0), (_, acc1) = state

    lam = (jnp.exp(jnp.sum(lq1_ref[...] * lk1_ref[...], axis=-1, keepdims=True))
           - jnp.exp(jnp.sum(lq2_ref[...] * lk2_ref[...], axis=-1, keepdims=True))
           + lambda_init)
    o0 = acc0[:HEAD_DIM] / acc0[HEAD_DIM:HEAD_DIM + 1]
    o1 = acc1[:HEAD_DIM] / acc1[HEAD_DIM:HEAD_DIM + 1]
    o = (o0 - lam * o1).T
    y_ref[...] = (_rms(o, sg_ref[...]) * (1.0 - lambda_init)).astype(BF16)


def _diff_call(qkv, slopes_b, lq1, lk1, lq2, lk2, subln, layer, lambda_init, batch, seq):
    T = batch * seq
    tq, tk = 512, 256
    assert (tk % tq == 0 or tq % tk == 0) and seq % tk == 0 and seq % tq == 0
    qkv_v = qkv.reshape(N_SLABS, batch, seq, LANES)
    vec = lambda a: a.reshape(a.shape[0], 1, a.shape[1])
    lam_spec = pl.BlockSpec((None, 1, DIFF_QK_DIM), lambda b, h, i: (layer, 0, 0))
    kern = functools.partial(_diff_kernel, tq=tq, tk=tk, seq=seq, lambda_init=lambda_init)
    out = pl.pallas_call(
        kern,
        grid=(batch, DIFF_HEADS, seq // tq),
        in_specs=[
            pl.BlockSpec(memory_space=pltpu.SMEM),
            pl.BlockSpec((None, None, tq, LANES), lambda b, h, i: (SLAB_BQ + h, b, i, 0)),
            pl.BlockSpec((None, None, seq, LANES), lambda b, h, i: (SLAB_BK + h, b, 0, 0)),
            pl.BlockSpec((None, None, seq, LANES), lambda b, h, i: (SLAB_BV + h, b, 0, 0)),
            lam_spec, lam_spec, lam_spec, lam_spec,
            pl.BlockSpec((None, 1, HEAD_DIM), lambda b, h, i: (layer, 0, 0)),
        ],
        out_specs=pl.BlockSpec((None, None, tq, LANES), lambda b, h, i: (h, b, i, 0)),
        out_shape=jax.ShapeDtypeStruct((DIFF_HEADS, batch, seq, LANES), BF16),
        scratch_shapes=[pltpu.VMEM((2, seq, LANES), BF16), pltpu.VMEM((seq // tk, DIFF_VT_ROWS, tk), BF16),
                        pltpu.VMEM((3, 2, LANES, tq), BF16)],
        compiler_params=_params(3),
        name="differential_attention",
    )(slopes_b, qkv_v, qkv_v, qkv_v, vec(lq1), vec(lk1), vec(lq2), vec(lk2), vec(subln))
    return out.reshape(DIFF_HEADS, T, LANES)


def _swa_kernel(slopes_ref, sink_ref, q_ref, k_ref, v_ref, y_ref, *, tq, win, seq, layer):
    g = pl.program_id(1)
    row = lax.broadcasted_iota(jnp.int32, (tq, win), 0)
    col = lax.broadcasted_iota(jnp.int32, (tq, win), 1)
    col_minus_row = col - row
    heads = [g * SWA_GROUP + r for r in range(SWA_GROUP)]
    slopes2 = [slopes_ref[hd] * LOG2E for hd in heads]
    sinks2 = [sink_ref[layer * SWA_Q_HEADS + hd] * LOG2E for hd in heads]

    def tile(i, carry):
        q0 = pl.multiple_of(i * tq, tq)
        start = pl.multiple_of(jnp.clip(q0 - SWA_RADIUS, 0, seq - win), SWA_RADIUS)
        absrel = jnp.abs((start - q0) + col_minus_row)
        valid = absrel <= SWA_RADIUS
        dist = absrel.astype(F32)
        k = k_ref[pl.ds(start, win), :]
        v = v_ref[pl.ds(start, win), :]
        logits = [_dot_nt(q_ref[r, pl.ds(q0, tq), :], k) for r in range(SWA_GROUP)]
        for r in range(SWA_GROUP):
            s = jnp.where(valid, logits[r] - slopes2[r] * dist, NEG)
            m = jnp.maximum(jnp.max(s, axis=-1, keepdims=True), sinks2[r])
            p = jnp.exp2(s - m)
            den = jnp.sum(p, axis=-1, keepdims=True) + jnp.exp2(sinks2[r] - m)
            y_ref[r, pl.ds(q0, tq), :] = (_dot(p.astype(BF16), v) / den).astype(BF16)
        return carry

    lax.fori_loop(0, seq // tq, tile, 0)


def _swa_call(qkv, slopes_c, sink_flat, layer, batch, seq):
    T = batch * seq
    tq = 256
    win = tq + 2 * SWA_RADIUS
    qkv_v = qkv.reshape(N_SLABS, batch, seq, LANES)
    kern = functools.partial(_swa_kernel, tq=tq, win=win, seq=seq, layer=layer)
    out = pl.pallas_call(
        kern,
        grid=(batch, SWA_KV_HEADS),
        in_specs=[
            pl.BlockSpec(memory_space=pltpu.SMEM),
            pl.BlockSpec(memory_space=pltpu.SMEM),
            pl.BlockSpec((SWA_GROUP, None, seq, LANES), lambda b, g: (SLAB_CQ // SWA_GROUP + g, b, 0, 0)),
            pl.BlockSpec((None, None, seq, LANES), lambda b, g: (SLAB_CK + g, b, 0, 0)),
            pl.BlockSpec((None, None, seq, LANES), lambda b, g: (SLAB_CV + g, b, 0, 0)),
        ],
        out_specs=pl.BlockSpec((SWA_GROUP, None, seq, LANES), lambda b, g: (g, b, 0, 0)),
        out_shape=jax.ShapeDtypeStruct((SWA_Q_HEADS, batch, seq, LANES), BF16),
        compiler_params=_params(2),
        name="window_gqa_attention",
    )(slopes_c, sink_flat, qkv_v, qkv_v, qkv_v)
    return out.reshape(SWA_Q_HEADS, T, LANES)


def _outproj_kernel(ya_ref, yb_ref, yc_ref, x_ref, w_ref, gn_ref, o_ref, hn_ref, wb_ref, *, cast_rows):
    @pl.when(pl.program_id(0) == 0)
    def _():
        for r0 in range(0, wb_ref.shape[0], cast_rows):
            wb_ref[r0:r0 + cast_rows, :] = w_ref[r0:r0 + cast_rows, :].astype(BF16)

    slabs = [y_ref[s] for y_ref in (ya_ref, yb_ref, yc_ref) for s in range(y_ref.shape[0])]
    y = jnp.concatenate(slabs, axis=-1)
    xn = x_ref[...] + _dot(y, wb_ref[...])
    o_ref[...] = xn
    hn_ref[...] = _rms(xn, gn_ref[...]).astype(BF16)


def _outproj_call(ya, yb, yc, x, w_out, layer, gains, gain_layer):
    T, D = x.shape
    tm = 512
    row_spec = pl.BlockSpec((tm, D), lambda i: (i, 0))
    slab_spec = lambda y: pl.BlockSpec((y.shape[0], tm, LANES), lambda i: (0, i, 0))
    return pl.pallas_call(
        functools.partial(_outproj_kernel, cast_rows=256),
        grid=(T // tm,),
        in_specs=[
            slab_spec(ya), slab_spec(yb), slab_spec(yc),
            row_spec,
            pl.BlockSpec((None,) + w_out.shape[1:], lambda i: (layer, 0, 0), pipeline_mode=pl.Buffered(1)),
            pl.BlockSpec((None, 1, D), lambda i: (gain_layer, 0, 0)),
        ],
        out_specs=[row_spec, row_spec],
        out_shape=[jax.ShapeDtypeStruct((T, D), F32), jax.ShapeDtypeStruct((T, D), BF16)],
        scratch_shapes=[pltpu.VMEM(w_out.shape[1:], BF16)],
        compiler_params=_params(1),
        name="output_projection",
    )(ya, yb, yc, x, w_out, gains)


def kernel(x, ffn1_norm, ffn1_w_gate, ffn1_w_up, ffn1_w_down, mix_norm, w_in, w_out, diff_lambda_q1, diff_lambda_k1, diff_lambda_q2, diff_lambda_k2, diff_subln, swa_sink, ffn2_norm, ffn2_w_gate, ffn2_w_up, ffn2_w_down, final_norm):
    batch, seq, d_model = x.shape
    depth = w_in.shape[0]
    xf = x.reshape(batch * seq, d_model)
    gain3 = lambda a: a.reshape(a.shape[0], 1, a.shape[1])
    ffn1_g, mix_g, ffn2_g = gain3(ffn1_norm), gain3(mix_norm), gain3(ffn2_norm)
    final_g = final_norm.reshape(1, 1, d_model)
    slopes_a = jnp.asarray(SLOPES_A, F32)
    slopes_b = jnp.asarray(SLOPES_B, F32)
    slopes_c = jnp.asarray(SLOPES_C, F32)
    sink_flat = swa_sink.astype(F32).reshape(-1)

    h = _norm_call(xf, ffn1_g, 0)
    for l in range(depth):
        lambda_init = 0.8 - 0.6 * math.exp(-0.3 * l)
        xf, h = _ffn_call(h, xf, ffn1_w_gate, ffn1_w_up, ffn1_w_down, l, mix_g, l, final=False)
        qkv = _inproj_call(h, w_in, l)
        ya = _dilated_attention(qkv, slopes_a, batch, seq)
        yb = _diff_call(qkv, slopes_b, diff_lambda_q1, diff_lambda_k1, diff_lambda_q2, diff_lambda_k2,
                        diff_subln, l, lambda_init, batch, seq)
        yc = _swa_call(qkv, slopes_c, sink_flat, l, batch, seq)
        xf, h = _outproj_call(ya, yb, yc, xf, w_out, l, ffn2_g, l)
        if l + 1 < depth:
            xf, h = _ffn_call(h, xf, ffn2_w_gate, ffn2_w_up, ffn2_w_down, l, ffn1_g, l + 1, final=False)
        else:
            xf = _ffn_call(h, xf, ffn2_w_gate, ffn2_w_up, ffn2_w_down, l, final_g, 0, final=True)
    return xf.reshape(batch, seq, d_model)
```

```python
import functools
import math

import numpy as np
import jax
import jax.numpy as jnp
from jax import lax
from jax.experimental import pallas as pl
from jax.experimental.pallas import tpu as pltpu

F32 = jnp.float32
BF16 = jnp.bfloat16

HEAD_DIM = 128
DIL_HEADS = 6
DIL_BRANCHES = ((128, 1), (512, 4), (2048, 16))
DIL_RADIUS = 64
DIFF_HEADS = 4
DIFF_QK_DIM = HEAD_DIM // 2
SWA_Q_HEADS = 6
SWA_KV_HEADS = 2
SWA_GROUP = SWA_Q_HEADS // SWA_KV_HEADS
SWA_RADIUS = 128
SWA_OFFSETS = 3
N_ALIBI_HEADS = SWA_Q_HEADS + DIL_HEADS + DIFF_HEADS
RMS_EPS = 1e-6
NEG = -1e30
ATTN_SCALE = HEAD_DIM ** -0.5
LOG2E = math.log2(math.e)
ATTN_Q_SCALE = ATTN_SCALE * LOG2E
DIFF_Q_SCALE = DIFF_QK_DIM ** -0.5 * LOG2E

SLAB_AQ, SLAB_AK, SLAB_AV = 0, 6, 12
SLAB_BQ, SLAB_BK, SLAB_BV = 18, 22, 26
SLAB_CQ, SLAB_CK, SLAB_CV = 30, 36, 38
N_SLABS = 40

LANES = 128
VMEM_LIMIT_BYTES = 60000 * 1024

_SLOPES = [float(np.float32(2.0 ** (-8.0 * (n + 1) / N_ALIBI_HEADS))) for n in range(N_ALIBI_HEADS)]
SLOPES_C = _SLOPES[:SWA_Q_HEADS]
SLOPES_A = _SLOPES[SWA_Q_HEADS:SWA_Q_HEADS + DIL_HEADS]
SLOPES_B = _SLOPES[SWA_Q_HEADS + DIL_HEADS:]


def _params(n_grid, flags=None):
    return pltpu.CompilerParams(dimension_semantics=("arbitrary",) * n_grid,
                                vmem_limit_bytes=VMEM_LIMIT_BYTES, flags=flags)


def _rms(x, g):
    ms = jnp.mean(x * x, axis=-1, keepdims=True)
    return x * lax.rsqrt(ms + RMS_EPS) * g


def _dot(a, b):
    return jnp.dot(a, b, preferred_element_type=F32)


def _dot_nt(a, b):
    return lax.dot_general(a, b, (((1,), (1,)), ((), ())), preferred_element_type=F32)


def _norm_kernel(x_ref, g_ref, o_ref):
    o_ref[...] = _rms(x_ref[...], g_ref[...]).astype(BF16)


def _norm_call(x, gains, layer):
    T, D = x.shape
    tm = 512
    return pl.pallas_call(
        _norm_kernel,
        grid=(T // tm,),
        in_specs=[pl.BlockSpec((tm, D), lambda i: (i, 0)),
                  pl.BlockSpec((None, 1, D), lambda i: (layer, 0, 0))],
        out_specs=pl.BlockSpec((tm, D), lambda i: (i, 0)),
        out_shape=jax.ShapeDtypeStruct((T, D), BF16),
        compiler_params=_params(1),
        name="rms_norm_bf16",
    )(x, gains)


def _ffn_kernel(h_ref, x_ref, wg_ref, wu_ref, wd_ref, wgut_ref, wdt_ref, gn_ref,
                *out_refs, n_full, n_chunk, rows, final):
    o_ref = out_refs[0]
    j = pl.program_id(1)

    def swiglu(g, u):
        return (0.5 * (g * jax.nn.sigmoid(g)) * u).astype(BF16)

    @pl.when(j == 0)
    def _():
        tail = wdt_ref.shape[0]
        w_gu, w_d = wgut_ref[...].astype(BF16), wdt_ref[...].astype(BF16)
        half = h_ref.shape[0] // 2
        gus = [_dot(h_ref[r0:r0 + half, :], w_gu) for r0 in (0, half)]
        for r0, gu in zip((0, half), gus):
            o_ref[r0:r0 + half, :] = _dot(swiglu(gu[:, :tail], gu[:, tail:]), w_d)

    @pl.when((j > 0) & (j < n_full))
    def _():
        h = h_ref[...]
        act = swiglu(_dot(h, wg_ref[...].astype(BF16)), _dot(h, wu_ref[...].astype(BF16)))
        o_ref[...] += _dot(act, wd_ref[...].astype(BF16))

    @pl.when(j < n_chunk)
    def _():
        r0 = pl.multiple_of(j * rows, rows)
        o_ref[pl.ds(r0, rows), :] += x_ref[...]

    @pl.when(j == n_full)
    def _():
        w_g, w_u, w_d = wg_ref[...].astype(BF16), wu_ref[...].astype(BF16), wd_ref[...].astype(BF16)
        half = h_ref.shape[0] // 2
        acts = [swiglu(_dot(h_ref[r0:r0 + half, :], w_g), _dot(h_ref[r0:r0 + half, :], w_u))
                for r0 in (0, half)]
        for r0, act in zip((0, half), acts):
            xn = o_ref[r0:r0 + half, :] + _dot(act, w_d)
            if final:
                o_ref[r0:r0 + half, :] = _rms(xn, gn_ref[...])
            else:
                o_ref[r0:r0 + half, :] = xn
                out_refs[1][r0:r0 + half, :] = _rms(xn, gn_ref[...]).astype(BF16)


def _ffn_call(h, x, w_gate, w_up, w_down, layer, gains, gain_layer, final):
    T, D = x.shape
    d_ff = w_gate.shape[-1]
    tm, tf, tail = 1024, 256, 128
    n_full = d_ff // tf
    assert n_full * tf + tail == d_ff
    tail_blk = (n_full * tf) // tail
    n_chunk = 8
    rows = tm // n_chunk
    kern = functools.partial(_ffn_kernel, n_full=n_full, n_chunk=n_chunk, rows=rows, final=final)
    once = pl.Buffered(1)
    full = lambda j: jnp.maximum(j - 1, 0)
    w_gu_tail = jnp.concatenate([w_gate[layer, :, n_full * tf:], w_up[layer, :, n_full * tf:]], axis=-1)
    in_specs = [
        pl.BlockSpec((tm, D), lambda i, j: (i, 0)),
        pl.BlockSpec((rows, D), lambda i, j: (i * n_chunk + jnp.minimum(j, n_chunk - 1), 0)),
        pl.BlockSpec((None, D, tf), lambda i, j: (layer, 0, full(j))),
        pl.BlockSpec((None, D, tf), lambda i, j: (layer, 0, full(j))),
        pl.BlockSpec((None, tf, D), lambda i, j: (layer, full(j), 0)),
        pl.BlockSpec((D, 2 * tail), lambda i, j: (0, 0), pipeline_mode=once),
        pl.BlockSpec((None, tail, D), lambda i, j: (layer, tail_blk, 0), pipeline_mode=once),
        pl.BlockSpec((None, 1, D), lambda i, j: (gain_layer, 0, 0)),
    ]
    row_spec = pl.BlockSpec((tm, D), lambda i, j: (i, 0))
    if final:
        out_specs = row_spec
        out_shape = jax.ShapeDtypeStruct((T, D), F32)
    else:
        out_specs = [row_spec, row_spec]
        out_shape = [jax.ShapeDtypeStruct((T, D), F32), jax.ShapeDtypeStruct((T, D), BF16)]
    return pl.pallas_call(
        kern,
        grid=(T // tm, n_full + 1),
        in_specs=in_specs,
        out_specs=out_specs,
        out_shape=out_shape,
        compiler_params=_params(2),
        name="swiglu_half_step",
    )(h, x, w_gate, w_up, w_down, w_gu_tail, w_down, gains)


def _inproj_kernel(scale_ref, h_ref, w_ref, o_ref, *, n_sub):
    j = pl.program_id(1)
    r = _dot(h_ref[...], w_ref[...].astype(BF16))
    for t in range(n_sub):
        o_ref[t] = (r[:, t * LANES:(t + 1) * LANES] * scale_ref[j * n_sub + t]).astype(BF16)


def _inproj_call(h, w_in, layer):
    T, D = h.shape
    tm, tn = 1024, 8 * LANES
    n_sub = tn // LANES
    n_blk = w_in.shape[-1] // tn
    assert n_blk * n_sub == N_SLABS
    scale = np.ones((N_SLABS,), np.float32)
    scale[SLAB_AQ:SLAB_AK] = ATTN_Q_SCALE
    scale[SLAB_BQ:SLAB_BK] = DIFF_Q_SCALE
    scale[SLAB_CQ:SLAB_CK] = ATTN_Q_SCALE
    return pl.pallas_call(
        functools.partial(_inproj_kernel, n_sub=n_sub),
        grid=(T // tm, n_blk),
        in_specs=[pl.BlockSpec(memory_space=pltpu.SMEM),
                  pl.BlockSpec((tm, D), lambda i, j: (i, 0)),
                  pl.BlockSpec((None, D, tn), lambda i, j: (layer, 0, j))],
        out_specs=pl.BlockSpec((n_sub, tm, LANES), lambda i, j: (j, i, 0)),
        out_shape=jax.ShapeDtypeStruct((N_SLABS, T, LANES), BF16),
        compiler_params=_params(2),
        name="input_projection",
    )(jnp.asarray(scale), h, w_in)


DIL_CLASSES = 4
DIL_OFFSETS = 3


def _dil_kernel(slopes_ref, q_ref, k_ref, v_ref, y_ref,
                stage, k4, v4, qf, q4, a1, m1, s1, a4, m4, s4, bias_a, bias_b, *, tq, seq, blk):
    h = pl.program_id(1)
    i = pl.program_id(2)
    slope2 = slopes_ref[h] * LOG2E
    (_, d1), (_, d4), (_, d16) = DIL_BRANCHES
    per4, per16 = tq // d4, tq // d16
    n1, n4, n16 = blk, min(blk, per4), min(blk, per16)
    win1 = min(n1 + 2 * DIL_RADIUS, seq // d1)
    win4 = min(n4 + 2 * DIL_RADIUS, seq // d4)
    win16 = min(n16 + 2 * DIL_RADIUS, seq // d16)

    def classes(ref, n):
        return [ref[pl.ds(r, n, stride=DIL_CLASSES), :] for r in range(DIL_CLASSES)]

    @pl.when(i == 0)
    def _():
        for src, dst in ((k_ref, k4), (v_ref, v4)):
            stage[...] = src[...].astype(F32)
            for r, rows in enumerate(classes(stage, seq // DIL_CLASSES)):
                dst[r] = rows
        for idx, (dil, n_q, win, ref) in enumerate(((d1, n1, win1, bias_a), (d4, n4, win4, bias_a),
                                                    (d16, n16, win16, bias_b))):
            row = lax.broadcasted_iota(jnp.int32, (n_q, win), 0)
            col = lax.broadcasted_iota(jnp.int32, (n_q, win), 1)
            for case in range(DIL_OFFSETS):
                absrel = jnp.abs(col - row - case * DIL_RADIUS)
                tile = jnp.where(absrel <= DIL_RADIUS, (-slope2 * dil) * absrel.astype(F32), NEG)
                if ref is bias_a:
                    ref[idx, case] = tile
                else:
                    ref[case] = tile

    qf[...] = q_ref[...].astype(F32)
    for r, rows in enumerate(classes(qf, tq // DIL_CLASSES)):
        q4[r] = rows

    def window(u0, sub_len, win):
        sub_start = jnp.clip(u0 - DIL_RADIUS, 0, sub_len - win)
        return sub_start, lax.div(u0 - sub_start, DIL_RADIUS)

    def softmax_block(s, v):
        m = jnp.max(s, axis=-1, keepdims=True)
        p = jnp.exp2(s - m)
        den = jnp.sum(p, axis=-1, keepdims=True)
        wide = (s.shape[0], LANES)
        return _dot(p.astype(BF16), v), jnp.broadcast_to(m, wide), jnp.broadcast_to(den, wide)

    def merge(block, prev):
        (acc, m, den), (acc_p, m_p, den_p) = block, prev
        m_new = jnp.maximum(m_p, m)
        w_p, w = jnp.exp2(m_p - m_new), jnp.exp2(m - m_new)
        return w_p * acc_p + w * acc, m_new, w_p * den_p + w * den

    blocks = []

    for b in range(tq // n1):
        def logits(b=b):
            start, case = window(i * tq + b * n1, seq // d1, win1)
            src = pl.ds(pl.multiple_of(start, DIL_RADIUS), win1)
            return _dot_nt(q_ref[b * n1:(b + 1) * n1, :], k_ref[src, :]) + bias_a[0, case], src

        def finish(s, src, b=b):
            dst = slice(b * n1, (b + 1) * n1)
            a1[dst, :], m1[dst, :], s1[dst, :] = softmax_block(s, v_ref[src, :])
            if b == tq // n1 - 1:
                for r in range(DIL_CLASSES):
                    for tok, cls in ((a1, a4), (m1, m4), (s1, s4)):
                        cls[r] = tok[pl.ds(r, per4, stride=DIL_CLASSES), :]
        blocks.append((logits, finish))

    for r in range(d4):
        for b in range(per4 // n4):
            def logits(r=r, b=b):
                start, case = window(i * per4 + b * n4, seq // d4, win4)
                src = pl.ds(pl.multiple_of(start, DIL_RADIUS), win4)
                q = q4[r, b * n4:(b + 1) * n4, :].astype(BF16)
                return _dot_nt(q, k4[r, src, :].astype(BF16)) + bias_a[1, case], src

            def finish(s, src, r=r, b=b):
                dst = slice(b * n4, (b + 1) * n4)
                a4[r, dst, :], m4[r, dst, :], s4[r, dst, :] = merge(
                    softmax_block(s, v4[r, src, :].astype(BF16)), (a4[r, dst, :], m4[r, dst, :], s4[r, dst, :]))
            blocks.append((logits, finish))

    for r16 in range(d16):
        r, a = r16 % DIL_CLASSES, r16 // DIL_CLASSES
        for b in range(per16 // n16):
            def logits(r=r, a=a, b=b):
                start, case = window(i * per16 + b * n16, seq // d16, win16)
                src = pl.ds(a + DIL_CLASSES * start, win16, stride=DIL_CLASSES)
                q = q4[r, pl.ds(a + DIL_CLASSES * b * n16, n16, stride=DIL_CLASSES), :].astype(BF16)
                return _dot_nt(q, k4[r, src, :].astype(BF16)) + bias_b[case], src

            def finish(s, src, r=r, a=a, b=b):
                dst = pl.ds(a + DIL_CLASSES * b * n16, n16, stride=DIL_CLASSES)
                a4[r, dst, :], m4[r, dst, :], s4[r, dst, :] = merge(
                    softmax_block(s, v4[r, src, :].astype(BF16)), (a4[r, dst, :], m4[r, dst, :], s4[r, dst, :]))
            blocks.append((logits, finish))

    ahead = 6
    queue = [blocks[n][0]() for n in range(ahead)]
    for n in range(len(blocks)):
        if n + ahead < len(blocks):
            queue.append(blocks[n + ahead][0]())
        blocks[n][1](*queue.pop(0))

    for r in range(DIL_CLASSES):
        a1[pl.ds(r, per4, stride=DIL_CLASSES), :] = a4[r] / s4[r]
    y_ref[...] = a1[...].astype(BF16)


def _dilated_attention(qkv, slopes_a, batch, seq):
    T = batch * seq
    tq, blk = 1024, 128
    assert tuple(d for _, d in DIL_BRANCHES) == (1, DIL_CLASSES, DIL_CLASSES ** 2)
    assert tq % (8 * DIL_CLASSES ** 2) == 0 and seq % tq == 0
    per16 = tq // DIL_CLASSES ** 2
    qkv_v = qkv.reshape(N_SLABS, batch, seq, LANES)
    kern = functools.partial(_dil_kernel, tq=tq, seq=seq, blk=blk)
    cls = lambda n: pltpu.VMEM((DIL_CLASSES, n // DIL_CLASSES, LANES), F32)
    tile = lambda n: pltpu.VMEM((n, LANES), F32)
    win = blk + 2 * DIL_RADIUS
    out = pl.pallas_call(
        kern,
        grid=(batch, DIL_HEADS, seq // tq),
        in_specs=[
            pl.BlockSpec(memory_space=pltpu.SMEM),
            pl.BlockSpec((None, None, tq, LANES), lambda b, h, i: (SLAB_AQ + h, b, i, 0)),
            pl.BlockSpec((None, None, seq, LANES), lambda b, h, i: (SLAB_AK + h, b, 0, 0)),
            pl.BlockSpec((None, None, seq, LANES), lambda b, h, i: (SLAB_AV + h, b, 0, 0)),
        ],
        out_specs=pl.BlockSpec((None, None, tq, LANES), lambda b, h, i: (h, b, i, 0)),
        out_shape=jax.ShapeDtypeStruct((DIL_HEADS, batch, seq, LANES), BF16),
        scratch_shapes=[tile(seq), cls(seq), cls(seq), tile(tq), cls(tq),
                        tile(tq), tile(tq), tile(tq), cls(tq), cls(tq), cls(tq),
                        pltpu.VMEM((2, DIL_OFFSETS, blk, win), F32),
                        pltpu.VMEM((DIL_OFFSETS, per16, min(per16 + 2 * DIL_RADIUS, seq // DIL_CLASSES ** 2)), F32)],
        compiler_params=_params(3),
        name="dilated_attention",
    )(slopes_a, qkv_v, qkv_v, qkv_v)
    return out.reshape(DIL_HEADS, T, LANES)


DIFF_FEAT = 3
DIFF_VT_ROWS = HEAD_DIM + 16


def _diff_kernel(slopes_ref, q_ref, k_ref, v_ref, lq1_ref, lk1_ref, lq2_ref, lk2_ref, sg_ref,
                 y_ref, ka_ref, vt_ref, qt_ref, *, tq, tk, seq, lambda_init):
    h = pl.program_id(1)
    i = pl.program_id(2)
    slope2 = slopes_ref[h] * LOG2E
    n_kb = seq // tk
    half = DIFF_QK_DIM

    @pl.when(i == 0)
    def _():
        lane = lax.broadcasted_iota(jnp.int32, (tk, LANES), 1)
        k_local = lax.broadcasted_iota(jnp.int32, (tk, LANES), 0)
        rem = slope2 * k_local.astype(F32)
        parts = []
        for _ in range(DIFF_FEAT):
            part = rem.astype(BF16).astype(F32)
            parts.append(part)
            rem = rem - part
        ones_row = (lax.broadcasted_iota(jnp.int32, (DIFF_VT_ROWS - HEAD_DIM, tk), 0) == 0)
        for c in range(n_kb):
            kf = k_ref[c * tk:(c + 1) * tk, :].astype(F32)
            for mp in range(2):
                lo, fo = mp * half, half - mp * half
                ka = jnp.where((lane >= lo) & (lane < lo + half), kf, 0.0)
                for t, part in enumerate(parts):
                    ka = jnp.where(lane == fo + t, part, ka)
                ka_ref[mp, c * tk:(c + 1) * tk, :] = ka.astype(BF16)
            vt_ref[c, :HEAD_DIM, :] = v_ref[c * tk:(c + 1) * tk, :].astype(F32).T.astype(BF16)
            vt_ref[c, HEAD_DIM:, :] = ones_row.astype(BF16)

    qt = q_ref[...].astype(F32).T
    row = lax.broadcasted_iota(jnp.int32, (LANES, tq), 0)
    for mp in range(2):
        lo, fo = mp * half, half - mp * half
        own = jnp.where((row >= lo) & (row < lo + half), qt, 0.0)
        feat = (row >= fo) & (row < fo + DIFF_FEAT)
        qt_ref[0, mp] = jnp.where(feat, 1.0, own).astype(BF16)
        qt_ref[1, mp] = jnp.where(feat, -1.0, own).astype(BF16)
        qt_ref[2, mp] = own.astype(BF16)

    q0 = i * tq
    kb_diag = q0 // tk
    q_local = lax.broadcasted_iota(jnp.int32, (1, tq), 1).astype(F32)

    def logits(j):
        kb = kb_diag + j
        kb = jnp.where(kb >= n_kb, kb - n_kb, kb)
        k0 = pl.multiple_of(kb * tk, tk)
        if j < max(1, tq // tk):
            krow = lax.broadcasted_iota(jnp.int32, (tk, tq), 0)
            qcol = lax.broadcasted_iota(jnp.int32, (tk, tq), 1)
            tile = slope2 * jnp.abs((qcol - krow) + (q0 - k0)).astype(F32)
            return kb, 0.0, [_dot(ka_ref[mp, pl.ds(k0, tk), :], qt_ref[2, mp]) - tile for mp in range(2)]
        right = (kb > kb_diag).astype(jnp.int32)
        sgn = (1 - 2 * right).astype(F32)
        row_term = (-sgn * slope2) * ((q0 - k0).astype(F32) + q_local)
        return kb, row_term, [_dot(ka_ref[mp, pl.ds(k0, tk), :], qt_ref[right, mp]) for mp in range(2)]

    def local_softmax(kb, row_term, s_maps):
        vt = vt_ref[kb]
        out = []
        for s in s_maps:
            m = jnp.max(s, axis=0, keepdims=True)
            p = jnp.exp2(s - m).astype(BF16)
            out.append((m + row_term, _dot(vt, p)))
        return out

    state = None
    ahead = 3
    queue = [logits(j) for j in range(min(ahead, n_kb))]
    for j in range(n_kb):
        cur = queue.pop(0)
        if j + ahead < n_kb:
            queue.append(logits(j + ahead))
        part = local_softmax(*cur)
        if state is None:
            state = part
            continue
        merged = []
        for (m, acc), (m_b, acc_b) in zip(state, part):
            m_new = jnp.maximum(m, m_b)
            merged.append((m_new, jnp.exp2(m - m_new) * acc + jnp.exp2(m_b - m_new) * acc_b))
        state = merged
    (_, acc0), (_, acc1) = state

    lam = (jnp.exp(jnp.sum(lq1_ref[...] * lk1_ref[...], axis=-1, keepdims=True))
           - jnp.exp(jnp.sum(lq2_ref[...] * lk2_ref[...], axis=-1, keepdims=True))
           + lambda_init)
    o0 = acc0[:HEAD_DIM] / acc0[HEAD_DIM:HEAD_DIM + 1]
    o1 = acc1[:HEAD_DIM] / acc1[HEAD_DIM:HEAD_DIM + 1]
    o = (o0 - lam * o1).T
    y_ref[...] = (_rms(o, sg_ref[...]) * (1.0 - lambda_init)).astype(BF16)


def _diff_call(qkv, slopes_b, lq1, lk1, lq2, lk2, subln, layer, lambda_init, batch, seq):
    T = batch * seq
    tq, tk = 512, 256
    assert (tk % tq == 0 or tq % tk == 0) and seq % tk == 0 and seq % tq == 0
    qkv_v = qkv.reshape(N_SLABS, batch, seq, LANES)
    vec = lambda a: a.reshape(a.shape[0], 1, a.shape[1])
    lam_spec = pl.BlockSpec((None, 1, DIFF_QK_DIM), lambda b, h, i: (layer, 0, 0))
    kern = functools.partial(_diff_kernel, tq=tq, tk=tk, seq=seq, lambda_init=lambda_init)
    out = pl.pallas_call(
        kern,
        grid=(batch, DIFF_HEADS, seq // tq),
        in_specs=[
            pl.BlockSpec(memory_space=pltpu.SMEM),
            pl.BlockSpec((None, None, tq, LANES), lambda b, h, i: (SLAB_BQ + h, b, i, 0)),
            pl.BlockSpec((None, None, seq, LANES), lambda b, h, i: (SLAB_BK + h, b, 0, 0)),
            pl.BlockSpec((None, None, seq, LANES), lambda b, h, i: (SLAB_BV + h, b, 0, 0)),
            lam_spec, lam_spec, lam_spec, lam_spec,
            pl.BlockSpec((None, 1, HEAD_DIM), lambda b, h, i: (layer, 0, 0)),
        ],
        out_specs=pl.BlockSpec((None, None, tq, LANES), lambda b, h, i: (h, b, i, 0)),
        out_shape=jax.ShapeDtypeStruct((DIFF_HEADS, batch, seq, LANES), BF16),
        scratch_shapes=[pltpu.VMEM((2, seq, LANES), BF16), pltpu.VMEM((seq // tk, DIFF_VT_ROWS, tk), BF16),
                        pltpu.VMEM((3, 2, LANES, tq), BF16)],
        compiler_params=_params(3),
        name="differential_attention",
    )(slopes_b, qkv_v, qkv_v, qkv_v, vec(lq1), vec(lk1), vec(lq2), vec(lk2), vec(subln))
    return out.reshape(DIFF_HEADS, T, LANES)


def _swa_kernel(slopes_ref, sink_ref, q_ref, k_ref, v_ref, y_ref, bias_ref, *, tq, win, seq, layer,
                tiles_per_iter):
    g = pl.program_id(1)
    heads = [g * SWA_GROUP + r for r in range(SWA_GROUP)]
    sinks2 = [sink_ref[layer * SWA_Q_HEADS + hd] * LOG2E for hd in heads]

    row = lax.broadcasted_iota(jnp.int32, (tq, win), 0)
    col = lax.broadcasted_iota(jnp.int32, (tq, win), 1)
    for case in range(SWA_OFFSETS):
        absrel = jnp.abs(col - row - case * SWA_RADIUS)
        for r, hd in enumerate(heads):
            slope2 = slopes_ref[hd] * LOG2E
            bias_ref[r, case] = jnp.where(absrel <= SWA_RADIUS, -slope2 * absrel.astype(F32), NEG)

    def tiles(it, carry):
        work = []
        for t in range(tiles_per_iter):
            q0 = pl.multiple_of((it * tiles_per_iter + t) * tq, tq)
            start = pl.multiple_of(jnp.clip(q0 - SWA_RADIUS, 0, seq - win), SWA_RADIUS)
            case = lax.div(q0 - start, SWA_RADIUS)
            k = k_ref[pl.ds(start, win), :]
            for r in range(SWA_GROUP):
                work.append((r, q0, start, _dot_nt(q_ref[r, pl.ds(q0, tq), :], k) + bias_ref[r, case]))
        for r, q0, start, s in work:
            m = jnp.maximum(jnp.max(s, axis=-1, keepdims=True), sinks2[r])
            p = jnp.exp2(s - m)
            den = jnp.sum(p, axis=-1, keepdims=True) + jnp.exp2(sinks2[r] - m)
            o = _dot(p.astype(BF16), v_ref[pl.ds(start, win), :]) / den
            y_ref[r, pl.ds(q0, tq), :] = o.astype(BF16)
        return carry

    lax.fori_loop(0, seq // (tq * tiles_per_iter), tiles, 0)


def _swa_call(qkv, slopes_c, sink_flat, layer, batch, seq):
    T = batch * seq
    tq = 256
    win = tq + 2 * SWA_RADIUS
    qkv_v = qkv.reshape(N_SLABS, batch, seq, LANES)
    tiles_per_iter = 2
    assert seq % (tq * tiles_per_iter) == 0
    kern = functools.partial(_swa_kernel, tq=tq, win=win, seq=seq, layer=layer, tiles_per_iter=tiles_per_iter)
    out = pl.pallas_call(
        kern,
        grid=(batch, SWA_KV_HEADS),
        in_specs=[
            pl.BlockSpec(memory_space=pltpu.SMEM),
            pl.BlockSpec(memory_space=pltpu.SMEM),
            pl.BlockSpec((SWA_GROUP, None, seq, LANES), lambda b, g: (SLAB_CQ // SWA_GROUP + g, b, 0, 0)),
            pl.BlockSpec((None, None, seq, LANES), lambda b, g: (SLAB_CK + g, b, 0, 0)),
            pl.BlockSpec((None, None, seq, LANES), lambda b, g: (SLAB_CV + g, b, 0, 0)),
        ],
        out_specs=pl.BlockSpec((SWA_GROUP, None, seq, LANES), lambda b, g: (g, b, 0, 0)),
        out_shape=jax.ShapeDtypeStruct((SWA_Q_HEADS, batch, seq, LANES), BF16),
        scratch_shapes=[pltpu.VMEM((SWA_GROUP, SWA_OFFSETS, tq, win), F32)],
        compiler_params=_params(2),
        name="window_gqa_attention",
    )(slopes_c, sink_flat, qkv_v, qkv_v, qkv_v)
    return out.reshape(SWA_Q_HEADS, T, LANES)


def _outproj_kernel(ya_ref, yb_ref, yc_ref, x_ref, w_ref, gn_ref, o_ref, hn_ref, wb_ref, *, cast_rows):
    @pl.when(pl.program_id(0) == 0)
    def _():
        for r0 in range(0, wb_ref.shape[0], cast_rows):
            wb_ref[r0:r0 + cast_rows, :] = w_ref[r0:r0 + cast_rows, :].astype(BF16)

    slabs = [y_ref[s] for y_ref in (ya_ref, yb_ref, yc_ref) for s in range(y_ref.shape[0])]
    y = jnp.concatenate(slabs, axis=-1)
    xn = x_ref[...] + _dot(y, wb_ref[...])
    o_ref[...] = xn
    hn_ref[...] = _rms(xn, gn_ref[...]).astype(BF16)


def _outproj_call(ya, yb, yc, x, w_out, layer, gains, gain_layer):
    T, D = x.shape
    tm = 512
    row_spec = pl.BlockSpec((tm, D), lambda i: (i, 0))
    slab_spec = lambda y: pl.BlockSpec((y.shape[0], tm, LANES), lambda i: (0, i, 0))
    return pl.pallas_call(
        functools.partial(_outproj_kernel, cast_rows=256),
        grid=(T // tm,),
        in_specs=[
            slab_spec(ya), slab_spec(yb), slab_spec(yc),
            row_spec,
            pl.BlockSpec((None,) + w_out.shape[1:], lambda i: (layer, 0, 0), pipeline_mode=pl.Buffered(1)),
            pl.BlockSpec((None, 1, D), lambda i: (gain_layer, 0, 0)),
        ],
        out_specs=[row_spec, row_spec],
        out_shape=[jax.ShapeDtypeStruct((T, D), F32), jax.ShapeDtypeStruct((T, D), BF16)],
        scratch_shapes=[pltpu.VMEM(w_out.shape[1:], BF16)],
        compiler_params=_params(1),
        name="output_projection",
    )(ya, yb, yc, x, w_out, gains)


def kernel(x, ffn1_norm, ffn1_w_gate, ffn1_w_up, ffn1_w_down, mix_norm, w_in, w_out, diff_lambda_q1, diff_lambda_k1, diff_lambda_q2, diff_lambda_k2, diff_subln, swa_sink, ffn2_norm, ffn2_w_gate, ffn2_w_up, ffn2_w_down, final_norm):
    batch, seq, d_model = x.shape
    depth = w_in.shape[0]
    xf = x.reshape(batch * seq, d_model)
    gain3 = lambda a: a.reshape(a.shape[0], 1, a.shape[1])
    ffn1_g, mix_g, ffn2_g = gain3(ffn1_norm), gain3(mix_norm), gain3(ffn2_norm)
    final_g = final_norm.reshape(1, 1, d_model)
    slopes_a = jnp.asarray(SLOPES_A, F32)
    slopes_b = jnp.asarray(SLOPES_B, F32)
    slopes_c = jnp.asarray(SLOPES_C, F32)
    sink_flat = swa_sink.astype(F32).reshape(-1)

    h = _norm_call(xf, ffn1_g, 0)
    for l in range(depth):
        lambda_init = 0.8 - 0.6 * math.exp(-0.3 * l)
        xf, h = _ffn_call(h, xf, ffn1_w_gate, ffn1_w_up, ffn1_w_down, l, mix_g, l, final=False)
        qkv = _inproj_call(h, w_in, l)
        ya = _dilated_attention(qkv, slopes_a, batch, seq)
        yb = _diff_call(qkv, slopes_b, diff_lambda_q1, diff_lambda_k1, diff_lambda_q2, diff_lambda_k2,
                        diff_subln, l, lambda_init, batch, seq)
        yc = _swa_call(qkv, slopes_c, sink_flat, l, batch, seq)
        xf, h = _outproj_call(ya, yb, yc, xf, w_out, l, ffn2_g, l)
        if l + 1 < depth:
            xf, h = _ffn_call(h, xf, ffn2_w_gate, ffn2_w_up, ffn2_w_down, l, ffn1_g, l + 1, final=False)
        else:
            xf = _ffn_call(h, xf, ffn2_w_gate, ffn2_w_up, ffn2_w_down, l, final_g, 0, final=True)
    return xf.reshape(batch, seq, d_model)
```

```python
import functools
import math

import numpy as np
import jax
import jax.numpy as jnp
from jax import lax
from jax.experimental import pallas as pl
from jax.experimental.pallas import tpu as pltpu

F32 = jnp.float32
BF16 = jnp.bfloat16

HEAD_DIM = 128
DIL_HEADS = 6
DIL_BRANCHES = ((128, 1), (512, 4), (2048, 16))
DIL_RADIUS = 64
DIFF_HEADS = 4
DIFF_QK_DIM = HEAD_DIM // 2
SWA_Q_HEADS = 6
SWA_KV_HEADS = 2
SWA_GROUP = SWA_Q_HEADS // SWA_KV_HEADS
SWA_RADIUS = 128
SWA_OFFSETS = 3
N_ALIBI_HEADS = SWA_Q_HEADS + DIL_HEADS + DIFF_HEADS
RMS_EPS = 1e-6
NEG = -1e30
ATTN_SCALE = HEAD_DIM ** -0.5
LOG2E = math.log2(math.e)
ATTN_Q_SCALE = ATTN_SCALE * LOG2E
DIFF_Q_SCALE = DIFF_QK_DIM ** -0.5 * LOG2E

SLAB_AQ, SLAB_AK, SLAB_AV = 0, 6, 12
SLAB_BQ, SLAB_BK, SLAB_BV = 18, 22, 26
SLAB_CQ, SLAB_CK, SLAB_CV = 30, 36, 38
N_SLABS = 40

LANES = 128
VMEM_LIMIT_BYTES = 60000 * 1024

_SLOPES = [float(np.float32(2.0 ** (-8.0 * (n + 1) / N_ALIBI_HEADS))) for n in range(N_ALIBI_HEADS)]
SLOPES_C = _SLOPES[:SWA_Q_HEADS]
SLOPES_A = _SLOPES[SWA_Q_HEADS:SWA_Q_HEADS + DIL_HEADS]
SLOPES_B = _SLOPES[SWA_Q_HEADS + DIL_HEADS:]


def _params(n_grid, flags=None):
    return pltpu.CompilerParams(dimension_semantics=("arbitrary",) * n_grid,
                                vmem_limit_bytes=VMEM_LIMIT_BYTES, flags=flags)


def _rms(x, g):
    ms = jnp.mean(x * x, axis=-1, keepdims=True)
    return x * lax.rsqrt(ms + RMS_EPS) * g


def _dot(a, b):
    return jnp.dot(a, b, preferred_element_type=F32)


def _dot_nt(a, b):
    return lax.dot_general(a, b, (((1,), (1,)), ((), ())), preferred_element_type=F32)


def _norm_kernel(x_ref, g_ref, o_ref):
    o_ref[...] = _rms(x_ref[...], g_ref[...]).astype(BF16)


def _norm_call(x, gains, layer):
    T, D = x.shape
    tm = 512
    return pl.pallas_call(
        _norm_kernel,
        grid=(T // tm,),
        in_specs=[pl.BlockSpec((tm, D), lambda i: (i, 0)),
                  pl.BlockSpec((None, 1, D), lambda i: (layer, 0, 0))],
        out_specs=pl.BlockSpec((tm, D), lambda i: (i, 0)),
        out_shape=jax.ShapeDtypeStruct((T, D), BF16),
        compiler_params=_params(1),
        name="rms_norm_bf16",
    )(x, gains)


def _ffn_kernel(h_ref, x_ref, wg_ref, wu_ref, wd_ref, wgut_ref, wdt_ref, gn_ref,
                *out_refs, n_full, n_chunk, rows, final):
    o_ref = out_refs[0]
    j = pl.program_id(1)

    def swiglu(g, u):
        return (0.5 * (g * jax.nn.sigmoid(g)) * u).astype(BF16)

    @pl.when(j == 0)
    def _():
        tail = wdt_ref.shape[0]
        w_gu, w_d = wgut_ref[...].astype(BF16), wdt_ref[...].astype(BF16)
        half = h_ref.shape[0] // 2
        gus = [_dot(h_ref[r0:r0 + half, :], w_gu) for r0 in (0, half)]
        for r0, gu in zip((0, half), gus):
            o_ref[r0:r0 + half, :] = _dot(swiglu(gu[:, :tail], gu[:, tail:]), w_d)

    @pl.when((j > 0) & (j < n_full))
    def _():
        h = h_ref[...]
        act = swiglu(_dot(h, wg_ref[...].astype(BF16)), _dot(h, wu_ref[...].astype(BF16)))
        o_ref[...] += _dot(act, wd_ref[...].astype(BF16))

    @pl.when(j < n_chunk)
    def _():
        r0 = pl.multiple_of(j * rows, rows)
        o_ref[pl.ds(r0, rows), :] += x_ref[...]

    @pl.when(j == n_full)
    def _():
        w_g, w_u, w_d = wg_ref[...].astype(BF16), wu_ref[...].astype(BF16), wd_ref[...].astype(BF16)
        half = h_ref.shape[0] // 2
        acts = [swiglu(_dot(h_ref[r0:r0 + half, :], w_g), _dot(h_ref[r0:r0 + half, :], w_u))
                for r0 in (0, half)]
        for r0, act in zip((0, half), acts):
            xn = o_ref[r0:r0 + half, :] + _dot(act, w_d)
            if final:
                o_ref[r0:r0 + half, :] = _rms(xn, gn_ref[...])
            else:
                o_ref[r0:r0 + half, :] = xn
                out_refs[1][r0:r0 + half, :] = _rms(xn, gn_ref[...]).astype(BF16)


def _ffn_call(h, x, w_gate, w_up, w_down, layer, gains, gain_layer, final):
    T, D = x.shape
    d_ff = w_gate.shape[-1]
    tm, tf, tail = 1024, 256, 128
    n_full = d_ff // tf
    assert n_full * tf + tail == d_ff
    tail_blk = (n_full * tf) // tail
    n_chunk = 8
    rows = tm // n_chunk
    kern = functools.partial(_ffn_kernel, n_full=n_full, n_chunk=n_chunk, rows=rows, final=final)
    once = pl.Buffered(1)
    full = lambda j: jnp.maximum(j - 1, 0)
    w_gu_tail = jnp.concatenate([w_gate[layer, :, n_full * tf:], w_up[layer, :, n_full * tf:]], axis=-1)
    in_specs = [
        pl.BlockSpec((tm, D), lambda i, j: (i, 0)),
        pl.BlockSpec((rows, D), lambda i, j: (i * n_chunk + jnp.minimum(j, n_chunk - 1), 0)),
        pl.BlockSpec((None, D, tf), lambda i, j: (layer, 0, full(j))),
        pl.BlockSpec((None, D, tf), lambda i, j: (layer, 0, full(j))),
        pl.BlockSpec((None, tf, D), lambda i, j: (layer, full(j), 0)),
        pl.BlockSpec((D, 2 * tail), lambda i, j: (0, 0), pipeline_mode=once),
        pl.BlockSpec((None, tail, D), lambda i, j: (layer, tail_blk, 0), pipeline_mode=once),
        pl.BlockSpec((None, 1, D), lambda i, j: (gain_layer, 0, 0)),
    ]
    row_spec = pl.BlockSpec((tm, D), lambda i, j: (i, 0))
    if final:
        out_specs = row_spec
        out_shape = jax.ShapeDtypeStruct((T, D), F32)
    else:
        out_specs = [row_spec, row_spec]
        out_shape = [jax.ShapeDtypeStruct((T, D), F32), jax.ShapeDtypeStruct((T, D), BF16)]
    return pl.pallas_call(
        kern,
        grid=(T // tm, n_full + 1),
        in_specs=in_specs,
        out_specs=out_specs,
        out_shape=out_shape,
        compiler_params=_params(2),
        name="swiglu_half_step",
    )(h, x, w_gate, w_up, w_down, w_gu_tail, w_down, gains)


def _inproj_kernel(scale_ref, h_ref, w_ref, o_ref, *, n_sub):
    j = pl.program_id(1)
    r = _dot(h_ref[...], w_ref[...].astype(BF16))
    for t in range(n_sub):
        o_ref[t] = (r[:, t * LANES:(t + 1) * LANES] * scale_ref[j * n_sub + t]).astype(BF16)


def _inproj_call(h, w_in, layer):
    T, D = h.shape
    tm, tn = 1024, 8 * LANES
    n_sub = tn // LANES
    n_blk = w_in.shape[-1] // tn
    assert n_blk * n_sub == N_SLABS
    scale = np.ones((N_SLABS,), np.float32)
    scale[SLAB_AQ:SLAB_AK] = ATTN_Q_SCALE
    scale[SLAB_BQ:SLAB_BK] = DIFF_Q_SCALE
    scale[SLAB_CQ:SLAB_CK] = ATTN_Q_SCALE
    return pl.pallas_call(
        functools.partial(_inproj_kernel, n_sub=n_sub),
        grid=(T // tm, n_blk),
        in_specs=[pl.BlockSpec(memory_space=pltpu.SMEM),
                  pl.BlockSpec((tm, D), lambda i, j: (i, 0)),
                  pl.BlockSpec((None, D, tn), lambda i, j: (layer, 0, j))],
        out_specs=pl.BlockSpec((n_sub, tm, LANES), lambda i, j: (j, i, 0)),
        out_shape=jax.ShapeDtypeStruct((N_SLABS, T, LANES), BF16),
        compiler_params=_params(2),
        name="input_projection",
    )(jnp.asarray(scale), h, w_in)


DIL_CLASSES = 4
DIL_OFFSETS = 3


def _dil_kernel(slopes_ref, q_ref, k_ref, v_ref, y_ref,
                stage, k4, v4, qf, q4, a1, m1, s1, a4, m4, s4, bias_a, bias_b, *, tq, seq, blk):
    h = pl.program_id(1)
    i = pl.program_id(2)
    slope2 = slopes_ref[h] * LOG2E
    (_, d1), (_, d4), (_, d16) = DIL_BRANCHES
    per4, per16 = tq // d4, tq // d16
    n1, n4, n16 = blk, min(blk, per4), min(blk, per16)
    win1 = min(n1 + 2 * DIL_RADIUS, seq // d1)
    win4 = min(n4 + 2 * DIL_RADIUS, seq // d4)
    win16 = min(n16 + 2 * DIL_RADIUS, seq // d16)

    def classes(ref, n):
        return [ref[pl.ds(r, n, stride=DIL_CLASSES), :] for r in range(DIL_CLASSES)]

    @pl.when(i == 0)
    def _():
        for src, dst in ((k_ref, k4), (v_ref, v4)):
            stage[...] = src[...].astype(F32)
            for r, rows in enumerate(classes(stage, seq // DIL_CLASSES)):
                dst[r] = rows
        for idx, (dil, n_q, win, ref) in enumerate(((d1, n1, win1, bias_a), (d4, n4, win4, bias_a),
                                                    (d16, n16, win16, bias_b))):
            row = lax.broadcasted_iota(jnp.int32, (n_q, win), 0)
            col = lax.broadcasted_iota(jnp.int32, (n_q, win), 1)
            for case in range(DIL_OFFSETS):
                absrel = jnp.abs(col - row - case * DIL_RADIUS)
                tile = jnp.where(absrel <= DIL_RADIUS, (-slope2 * dil) * absrel.astype(F32), NEG)
                if ref is bias_a:
                    ref[idx, case] = tile
                else:
                    ref[case] = tile

    qf[...] = q_ref[...].astype(F32)
    for r, rows in enumerate(classes(qf, tq // DIL_CLASSES)):
        q4[r] = rows

    def window(u0, sub_len, win):
        sub_start = jnp.clip(u0 - DIL_RADIUS, 0, sub_len - win)
        return sub_start, lax.div(u0 - sub_start, DIL_RADIUS)

    def softmax_block(s, v):
        m = jnp.max(s, axis=-1, keepdims=True)
        p = jnp.exp2(s - m)
        den = jnp.sum(p, axis=-1, keepdims=True)
        wide = (s.shape[0], LANES)
        return _dot(p.astype(BF16), v), jnp.broadcast_to(m, wide), jnp.broadcast_to(den, wide)

    def merge(block, prev):
        (acc, m, den), (acc_p, m_p, den_p) = block, prev
        m_new = jnp.maximum(m_p, m)
        w_p, w = jnp.exp2(m_p - m_new), jnp.exp2(m - m_new)
        return w_p * acc_p + w * acc, m_new, w_p * den_p + w * den

    blocks = []

    for b in range(tq // n1):
        def logits(b=b):
            start, case = window(i * tq + b * n1, seq // d1, win1)
            src = pl.ds(pl.multiple_of(start, DIL_RADIUS), win1)
            return _dot_nt(q_ref[b * n1:(b + 1) * n1, :], k_ref[src, :]) + bias_a[0, case], src

        def finish(s, src, b=b):
            dst = slice(b * n1, (b + 1) * n1)
            a1[dst, :], m1[dst, :], s1[dst, :] = softmax_block(s, v_ref[src, :])
            if b == tq // n1 - 1:
                for r in range(DIL_CLASSES):
                    for tok, cls in ((a1, a4), (m1, m4), (s1, s4)):
                        cls[r] = tok[pl.ds(r, per4, stride=DIL_CLASSES), :]
        blocks.append((logits, finish))

    for r in range(d4):
        for b in range(per4 // n4):
            def logits(r=r, b=b):
                start, case = window(i * per4 + b * n4, seq // d4, win4)
                src = pl.ds(pl.multiple_of(start, DIL_RADIUS), win4)
                q = q4[r, b * n4:(b + 1) * n4, :].astype(BF16)
                return _dot_nt(q, k4[r, src, :].astype(BF16)) + bias_a[1, case], src

            def finish(s, src, r=r, b=b):
                dst = slice(b * n4, (b + 1) * n4)
                a4[r, dst, :], m4[r, dst, :], s4[r, dst, :] = merge(
                    softmax_block(s, v4[r, src, :].astype(BF16)), (a4[r, dst, :], m4[r, dst, :], s4[r, dst, :]))
            blocks.append((logits, finish))

    for r16 in range(d16):
        r, a = r16 % DIL_CLASSES, r16 // DIL_CLASSES
        for b in range(per16 // n16):
            def logits(r=r, a=a, b=b):
                start, case = window(i * per16 + b * n16, seq // d16, win16)
                src = pl.ds(a + DIL_CLASSES * start, win16, stride=DIL_CLASSES)
                q = q4[r, pl.ds(a + DIL_CLASSES * b * n16, n16, stride=DIL_CLASSES), :].astype(BF16)
                return _dot_nt(q, k4[r, src, :].astype(BF16)) + bias_b[case], src

            def finish(s, src, r=r, a=a, b=b):
                dst = pl.ds(a + DIL_CLASSES * b * n16, n16, stride=DIL_CLASSES)
                a4[r, dst, :], m4[r, dst, :], s4[r, dst, :] = merge(
                    softmax_block(s, v4[r, src, :].astype(BF16)), (a4[r, dst, :], m4[r, dst, :], s4[r, dst, :]))
            blocks.append((logits, finish))

    ahead = 6
    queue = [blocks[n][0]() for n in range(ahead)]
    for n in range(len(blocks)):
        if n + ahead < len(blocks):
            queue.append(blocks[n + ahead][0]())
        blocks[n][1](*queue.pop(0))

    for r in range(DIL_CLASSES):
        a1[pl.ds(r, per4, stride=DIL_CLASSES), :] = a4[r] / s4[r]
    y_ref[...] = a1[...].astype(BF16)


def _dilated_attention(qkv, slopes_a, batch, seq):
    T = batch * seq
    tq, blk = 1024, 128
    assert tuple(d for _, d in DIL_BRANCHES) == (1, DIL_CLASSES, DIL_CLASSES ** 2)
    assert tq % (8 * DIL_CLASSES ** 2) == 0 and seq % tq == 0
    per16 = tq // DIL_CLASSES ** 2
    qkv_v = qkv.reshape(N_SLABS, batch, seq, LANES)
    kern = functools.partial(_dil_kernel, tq=tq, seq=seq, blk=blk)
    cls = lambda n: pltpu.VMEM((DIL_CLASSES, n // DIL_CLASSES, LANES), F32)
    tile = lambda n: pltpu.VMEM((n, LANES), F32)
    win = blk + 2 * DIL_RADIUS
    out = pl.pallas_call(
        kern,
        grid=(batch, DIL_HEADS, seq // tq),
        in_specs=[
            pl.BlockSpec(memory_space=pltpu.SMEM),
            pl.BlockSpec((None, None, tq, LANES), lambda b, h, i: (SLAB_AQ + h, b, i, 0)),
            pl.BlockSpec((None, None, seq, LANES), lambda b, h, i: (SLAB_AK + h, b, 0, 0)),
            pl.BlockSpec((None, None, seq, LANES), lambda b, h, i: (SLAB_AV + h, b, 0, 0)),
        ],
        out_specs=pl.BlockSpec((None, None, tq, LANES), lambda b, h, i: (h, b, i, 0)),
        out_shape=jax.ShapeDtypeStruct((DIL_HEADS, batch, seq, LANES), BF16),
        scratch_shapes=[tile(seq), cls(seq), cls(seq), tile(tq), cls(tq),
                        tile(tq), tile(tq), tile(tq), cls(tq), cls(tq), cls(tq),
                        pltpu.VMEM((2, DIL_OFFSETS, blk, win), F32),
                        pltpu.VMEM((DIL_OFFSETS, per16, min(per16 + 2 * DIL_RADIUS, seq // DIL_CLASSES ** 2)), F32)],
        compiler_params=_params(3),
        name="dilated_attention",
    )(slopes_a, qkv_v, qkv_v, qkv_v)
    return out.reshape(DIL_HEADS, T, LANES)


DIFF_FEAT = 3
DIFF_VT_ROWS = HEAD_DIM + 16
DIFF_SAFE_GAP = 100.0


def _diff_kernel(slopes_ref, q_ref, k_ref, v_ref, lq1_ref, lk1_ref, lq2_ref, lk2_ref, sg_ref,
                 y_ref, ka_ref, vt_ref, qt_ref, kn_ref, *, tq, tk, seq, lambda_init):
    h = pl.program_id(1)
    i = pl.program_id(2)
    slope2 = slopes_ref[h] * LOG2E
    n_kb = seq // tk
    half = DIFF_QK_DIM

    @pl.when(i == 0)
    def _():
        lane = lax.broadcasted_iota(jnp.int32, (tk, LANES), 1)
        k_local = lax.broadcasted_iota(jnp.int32, (tk, LANES), 0)
        rem = slope2 * k_local.astype(F32)
        parts = []
        for _ in range(DIFF_FEAT):
            part = rem.astype(BF16).astype(F32)
            parts.append(part)
            rem = rem - part
        ones_row = (lax.broadcasted_iota(jnp.int32, (DIFF_VT_ROWS - HEAD_DIM, tk), 0) == 0)
        k_sq = [jnp.zeros((1, 1), F32)] * 2
        for c in range(n_kb):
            kf = k_ref[c * tk:(c + 1) * tk, :].astype(F32)
            for mp in range(2):
                lo, fo = mp * half, half - mp * half
                ka = jnp.where((lane >= lo) & (lane < lo + half), kf, 0.0)
                k_sq[mp] = jnp.maximum(k_sq[mp], jnp.max(jnp.sum(ka * ka, axis=-1, keepdims=True),
                                                         axis=0, keepdims=True))
                for t, part in enumerate(parts):
                    ka = jnp.where(lane == fo + t, part, ka)
                ka_ref[mp, c * tk:(c + 1) * tk, :] = ka.astype(BF16)
            vt_ref[c, :HEAD_DIM, :] = v_ref[c * tk:(c + 1) * tk, :].astype(F32).T.astype(BF16)
            vt_ref[c, HEAD_DIM:, :] = ones_row.astype(BF16)
        for mp in range(2):
            kn_ref[mp] = jnp.broadcast_to(jnp.sqrt(k_sq[mp]), kn_ref.shape[1:])

    qt = q_ref[...].astype(F32).T
    row = lax.broadcasted_iota(jnp.int32, (LANES, tq), 0)
    for mp in range(2):
        lo, fo = mp * half, half - mp * half
        own = jnp.where((row >= lo) & (row < lo + half), qt, 0.0)
        feat = (row >= fo) & (row < fo + DIFF_FEAT)
        qt_ref[0, mp] = jnp.where(feat, 1.0, own).astype(BF16)
        qt_ref[1, mp] = jnp.where(feat, -1.0, own).astype(BF16)
        qt_ref[2, mp] = own.astype(BF16)

    q0 = i * tq
    kb_diag = q0 // tk
    q_local = lax.broadcasted_iota(jnp.int32, (1, tq), 1).astype(F32)

    def logits(j):
        kb = kb_diag + j
        kb = jnp.where(kb >= n_kb, kb - n_kb, kb)
        k0 = pl.multiple_of(kb * tk, tk)
        if j < max(1, tq // tk):
            krow = lax.broadcasted_iota(jnp.int32, (tk, tq), 0)
            qcol = lax.broadcasted_iota(jnp.int32, (tk, tq), 1)
            tile = slope2 * jnp.abs((qcol - krow) + (q0 - k0)).astype(F32)
            return kb, 0.0, [_dot(ka_ref[mp, pl.ds(k0, tk), :], qt_ref[2, mp]) - tile for mp in range(2)]
        right = (kb > kb_diag).astype(jnp.int32)
        sgn = (1 - 2 * right).astype(F32)
        row_term = (-sgn * slope2) * ((q0 - k0).astype(F32) + q_local)
        return kb, row_term, [_dot(ka_ref[mp, pl.ds(k0, tk), :], qt_ref[right, mp]) for mp in range(2)]

    def stream(consume):
        ahead = 3
        queue = [logits(j) for j in range(min(ahead, n_kb))]
        for j in range(n_kb):
            cur = queue.pop(0)
            if j + ahead < n_kb:
                queue.append(logits(j + ahead))
            consume(*cur)

    def finish(acc0, acc1):
        lam = (jnp.exp(jnp.sum(lq1_ref[...] * lk1_ref[...], axis=-1, keepdims=True))
               - jnp.exp(jnp.sum(lq2_ref[...] * lk2_ref[...], axis=-1, keepdims=True))
               + lambda_init)
        o0 = acc0[:HEAD_DIM] / acc0[HEAD_DIM:HEAD_DIM + 1]
        o1 = acc1[:HEAD_DIM] / acc1[HEAD_DIM:HEAD_DIM + 1]
        o = (o0 - lam * o1).T
        y_ref[...] = (_rms(o, sg_ref[...]) * (1.0 - lambda_init)).astype(BF16)

    kt = k_ref[pl.ds(pl.multiple_of(q0, tq), tq), :].astype(F32).T
    bounds, gap = [], None
    for mp in range(2):
        own = (row >= mp * half) & (row < mp * half + half)
        q_norm = jnp.sqrt(jnp.sum(jnp.where(own, qt * qt, 0.0), axis=0, keepdims=True))
        self_logit = jnp.sum(jnp.where(own, qt * kt, 0.0), axis=0, keepdims=True)
        bound = q_norm * kn_ref[mp]
        bounds.append(bound)
        worst = jnp.max(bound - self_logit)
        gap = worst if gap is None else jnp.maximum(gap, worst)
    safe = gap <= DIFF_SAFE_GAP

    @pl.when(safe)
    def _():
        accs = [jnp.zeros((DIFF_VT_ROWS, tq), F32) for _ in range(2)]

        def consume(kb, row_term, s_maps):
            vt = vt_ref[kb]
            for mp, s in enumerate(s_maps):
                accs[mp] = accs[mp] + _dot(vt, jnp.exp2(s - (bounds[mp] - row_term)).astype(BF16))

        stream(consume)
        finish(*accs)

    @pl.when(jnp.logical_not(safe))
    def _():
        state = []

        def consume(kb, row_term, s_maps):
            vt = vt_ref[kb]
            part = []
            for s in s_maps:
                m = jnp.max(s, axis=0, keepdims=True)
                part.append((m + row_term, _dot(vt, jnp.exp2(s - m).astype(BF16))))
            if not state:
                state.extend(part)
                return
            for mp, ((m, acc), (m_b, acc_b)) in enumerate(zip(list(state), part)):
                m_new = jnp.maximum(m, m_b)
                state[mp] = (m_new, jnp.exp2(m - m_new) * acc + jnp.exp2(m_b - m_new) * acc_b)

        stream(consume)
        finish(state[0][1], state[1][1])


def _diff_call(qkv, slopes_b, lq1, lk1, lq2, lk2, subln, layer, lambda_init, batch, seq):
    T = batch * seq
    tq, tk = 512, 256
    assert (tk % tq == 0 or tq % tk == 0) and seq % tk == 0 and seq % tq == 0
    qkv_v = qkv.reshape(N_SLABS, batch, seq, LANES)
    vec = lambda a: a.reshape(a.shape[0], 1, a.shape[1])
    lam_spec = pl.BlockSpec((None, 1, DIFF_QK_DIM), lambda b, h, i: (layer, 0, 0))
    kern = functools.partial(_diff_kernel, tq=tq, tk=tk, seq=seq, lambda_init=lambda_init)
    out = pl.pallas_call(
        kern,
        grid=(batch, DIFF_HEADS, seq // tq),
        in_specs=[
            pl.BlockSpec(memory_space=pltpu.SMEM),
            pl.BlockSpec((None, None, tq, LANES), lambda b, h, i: (SLAB_BQ + h, b, i, 0)),
            pl.BlockSpec((None, None, seq, LANES), lambda b, h, i: (SLAB_BK + h, b, 0, 0)),
            pl.BlockSpec((None, None, seq, LANES), lambda b, h, i: (SLAB_BV + h, b, 0, 0)),
            lam_spec, lam_spec, lam_spec, lam_spec,
            pl.BlockSpec((None, 1, HEAD_DIM), lambda b, h, i: (layer, 0, 0)),
        ],
        out_specs=pl.BlockSpec((None, None, tq, LANES), lambda b, h, i: (h, b, i, 0)),
        out_shape=jax.ShapeDtypeStruct((DIFF_HEADS, batch, seq, LANES), BF16),
        scratch_shapes=[pltpu.VMEM((2, seq, LANES), BF16), pltpu.VMEM((seq // tk, DIFF_VT_ROWS, tk), BF16),
                        pltpu.VMEM((3, 2, LANES, tq), BF16), pltpu.VMEM((2, 1, tq), F32)],
        compiler_params=_params(3),
        name="differential_attention",
    )(slopes_b, qkv_v, qkv_v, qkv_v, vec(lq1), vec(lk1), vec(lq2), vec(lk2), vec(subln))
    return out.reshape(DIFF_HEADS, T, LANES)


def _swa_kernel(slopes_ref, sink_ref, q_ref, k_ref, v_ref, y_ref, bias_ref, *, tq, win, seq, layer,
                tiles_per_iter):
    g = pl.program_id(1)
    heads = [g * SWA_GROUP + r for r in range(SWA_GROUP)]
    sinks2 = [sink_ref[layer * SWA_Q_HEADS + hd] * LOG2E for hd in heads]

    row = lax.broadcasted_iota(jnp.int32, (tq, win), 0)
    col = lax.broadcasted_iota(jnp.int32, (tq, win), 1)
    for case in range(SWA_OFFSETS):
        absrel = jnp.abs(col - row - case * SWA_RADIUS)
        for r, hd in enumerate(heads):
            slope2 = slopes_ref[hd] * LOG2E
            bias_ref[r, case] = jnp.where(absrel <= SWA_RADIUS, -slope2 * absrel.astype(F32), NEG)

    def tiles(it, carry):
        work = []
        for t in range(tiles_per_iter):
            q0 = pl.multiple_of((it * tiles_per_iter + t) * tq, tq)
            start = pl.multiple_of(jnp.clip(q0 - SWA_RADIUS, 0, seq - win), SWA_RADIUS)
            case = lax.div(q0 - start, SWA_RADIUS)
            k = k_ref[pl.ds(start, win), :]
            for r in range(SWA_GROUP):
                work.append((r, q0, start, _dot_nt(q_ref[r, pl.ds(q0, tq), :], k) + bias_ref[r, case]))
        for r, q0, start, s in work:
            m = jnp.maximum(jnp.max(s, axis=-1, keepdims=True), sinks2[r])
            p = jnp.exp2(s - m)
            den = jnp.sum(p, axis=-1, keepdims=True) + jnp.exp2(sinks2[r] - m)
            o = _dot(p.astype(BF16), v_ref[pl.ds(start, win), :]) / den
            y_ref[r, pl.ds(q0, tq), :] = o.astype(BF16)
        return carry

    lax.fori_loop(0, seq // (tq * tiles_per_iter), tiles, 0)


def _swa_call(qkv, slopes_c, sink_flat, layer, batch, seq):
    T = batch * seq
    tq = 256
    win = tq + 2 * SWA_RADIUS
    qkv_v = qkv.reshape(N_SLABS, batch, seq, LANES)
    tiles_per_iter = 2
    assert seq % (tq * tiles_per_iter) == 0
    kern = functools.partial(_swa_kernel, tq=tq, win=win, seq=seq, layer=layer, tiles_per_iter=tiles_per_iter)
    out = pl.pallas_call(
        kern,
        grid=(batch, SWA_KV_HEADS),
        in_specs=[
            pl.BlockSpec(memory_space=pltpu.SMEM),
            pl.BlockSpec(memory_space=pltpu.SMEM),
            pl.BlockSpec((SWA_GROUP, None, seq, LANES), lambda b, g: (SLAB_CQ // SWA_GROUP + g, b, 0, 0)),
            pl.BlockSpec((None, None, seq, LANES), lambda b, g: (SLAB_CK + g, b, 0, 0)),
            pl.BlockSpec((None, None, seq, LANES), lambda b, g: (SLAB_CV + g, b, 0, 0)),
        ],
        out_specs=pl.BlockSpec((SWA_GROUP, None, seq, LANES), lambda b, g: (g, b, 0, 0)),
        out_shape=jax.ShapeDtypeStruct((SWA_Q_HEADS, batch, seq, LANES), BF16),
        scratch_shapes=[pltpu.VMEM((SWA_GROUP, SWA_OFFSETS, tq, win), F32)],
        compiler_params=_params(2),
        name="window_gqa_attention",
    )(slopes_c, sink_flat, qkv_v, qkv_v, qkv_v)
    return out.reshape(SWA_Q_HEADS, T, LANES)


def _outproj_kernel(ya_ref, yb_ref, yc_ref, x_ref, w_ref, gn_ref, o_ref, hn_ref, wb_ref, *, cast_rows):
    @pl.when(pl.program_id(0) == 0)
    def _():
        for r0 in range(0, wb_ref.shape[0], cast_rows):
            wb_ref[r0:r0 + cast_rows, :] = w_ref[r0:r0 + cast_rows, :].astype(BF16)

    slabs = [y_ref[s] for y_ref in (ya_ref, yb_ref, yc_ref) for s in range(y_ref.shape[0])]
    y = jnp.concatenate(slabs, axis=-1)
    xn = x_ref[...] + _dot(y, wb_ref[...])
    o_ref[...] = xn
    hn_ref[...] = _rms(xn, gn_ref[...]).astype(BF16)


def _outproj_call(ya, yb, yc, x, w_out, layer, gains, gain_layer):
    T, D = x.shape
    tm = 512
    row_spec = pl.BlockSpec((tm, D), lambda i: (i, 0))
    slab_spec = lambda y: pl.BlockSpec((y.shape[0], tm, LANES), lambda i: (0, i, 0))
    return pl.pallas_call(
        functools.partial(_outproj_kernel, cast_rows=256),
        grid=(T // tm,),
        in_specs=[
            slab_spec(ya), slab_spec(yb), slab_spec(yc),
            row_spec,
            pl.BlockSpec((None,) + w_out.shape[1:], lambda i: (layer, 0, 0), pipeline_mode=pl.Buffered(1)),
            pl.BlockSpec((None, 1, D), lambda i: (gain_layer, 0, 0)),
        ],
        out_specs=[row_spec, row_spec],
        out_shape=[jax.ShapeDtypeStruct((T, D), F32), jax.ShapeDtypeStruct((T, D), BF16)],
        scratch_shapes=[pltpu.VMEM(w_out.shape[1:], BF16)],
        compiler_params=_params(1),
        name="output_projection",
    )(ya, yb, yc, x, w_out, gains)


def kernel(x, ffn1_norm, ffn1_w_gate, ffn1_w_up, ffn1_w_down, mix_norm, w_in, w_out, diff_lambda_q1, diff_lambda_k1, diff_lambda_q2, diff_lambda_k2, diff_subln, swa_sink, ffn2_norm, ffn2_w_gate, ffn2_w_up, ffn2_w_down, final_norm):
    batch, seq, d_model = x.shape
    depth = w_in.shape[0]
    xf = x.reshape(batch * seq, d_model)
    gain3 = lambda a: a.reshape(a.shape[0], 1, a.shape[1])
    ffn1_g, mix_g, ffn2_g = gain3(ffn1_norm), gain3(mix_norm), gain3(ffn2_norm)
    final_g = final_norm.reshape(1, 1, d_model)
    slopes_a = jnp.asarray(SLOPES_A, F32)
    slopes_b = jnp.asarray(SLOPES_B, F32)
    slopes_c = jnp.asarray(SLOPES_C, F32)
    sink_flat = swa_sink.astype(F32).reshape(-1)

    h = _norm_call(xf, ffn1_g, 0)
    for l in range(depth):
        lambda_init = 0.8 - 0.6 * math.exp(-0.3 * l)
        xf, h = _ffn_call(h, xf, ffn1_w_gate, ffn1_w_up, ffn1_w_down, l, mix_g, l, final=False)
        qkv = _inproj_call(h, w_in, l)
        ya = _dilated_attention(qkv, slopes_a, batch, seq)
        yb = _diff_call(qkv, slopes_b, diff_lambda_q1, diff_lambda_k1, diff_lambda_q2, diff_lambda_k2,
                        diff_subln, l, lambda_init, batch, seq)
        yc = _swa_call(qkv, slopes_c, sink_flat, l, batch, seq)
        xf, h = _outproj_call(ya, yb, yc, xf, w_out, l, ffn2_g, l)
        if l + 1 < depth:
            xf, h = _ffn_call(h, xf, ffn2_w_gate, ffn2_w_up, ffn2_w_down, l, ffn1_g, l + 1, final=False)
        else:
            xf = _ffn_call(h, xf, ffn2_w_gate, ffn2_w_up, ffn2_w_down, l, final_g, 0, final=True)
    return xf.reshape(batch, seq, d_model)
```

```python
import functools
import math

import numpy as np
import jax
import jax.numpy as jnp
from jax import lax
from jax.experimental import pallas as pl
from jax.experimental.pallas import tpu as pltpu

F32 = jnp.float32
BF16 = jnp.bfloat16

HEAD_DIM = 128
DIL_HEADS = 6
DIL_BRANCHES = ((128, 1), (512, 4), (2048, 16))
DIL_RADIUS = 64
DIFF_HEADS = 4
DIFF_QK_DIM = HEAD_DIM // 2
SWA_Q_HEADS = 6
SWA_KV_HEADS = 2
SWA_GROUP = SWA_Q_HEADS // SWA_KV_HEADS
SWA_RADIUS = 128
SWA_OFFSETS = 3
N_ALIBI_HEADS = SWA_Q_HEADS + DIL_HEADS + DIFF_HEADS
RMS_EPS = 1e-6
NEG = -1e30
ATTN_SCALE = HEAD_DIM ** -0.5
LOG2E = math.log2(math.e)
ATTN_Q_SCALE = ATTN_SCALE * LOG2E
DIFF_Q_SCALE = DIFF_QK_DIM ** -0.5 * LOG2E

SLAB_AQ, SLAB_AK, SLAB_AV = 0, 6, 12
SLAB_BQ, SLAB_BK, SLAB_BV = 18, 22, 26
SLAB_CQ, SLAB_CK, SLAB_CV = 30, 36, 38
N_SLABS = 40

LANES = 128
VMEM_LIMIT_BYTES = 60000 * 1024

_SLOPES = [float(np.float32(2.0 ** (-8.0 * (n + 1) / N_ALIBI_HEADS))) for n in range(N_ALIBI_HEADS)]
SLOPES_C = _SLOPES[:SWA_Q_HEADS]
SLOPES_A = _SLOPES[SWA_Q_HEADS:SWA_Q_HEADS + DIL_HEADS]
SLOPES_B = _SLOPES[SWA_Q_HEADS + DIL_HEADS:]


def _params(n_grid, flags=None):
    return pltpu.CompilerParams(dimension_semantics=("arbitrary",) * n_grid,
                                vmem_limit_bytes=VMEM_LIMIT_BYTES, flags=flags)


def _rms(x, g):
    ms = jnp.mean(x * x, axis=-1, keepdims=True)
    return x * lax.rsqrt(ms + RMS_EPS) * g


def _dot(a, b):
    return jnp.dot(a, b, preferred_element_type=F32)


def _dot_nt(a, b):
    return lax.dot_general(a, b, (((1,), (1,)), ((), ())), preferred_element_type=F32)


def _norm_kernel(x_ref, g_ref, o_ref):
    o_ref[...] = _rms(x_ref[...], g_ref[...]).astype(BF16)


def _norm_call(x, gains, layer):
    T, D = x.shape
    tm = 512
    return pl.pallas_call(
        _norm_kernel,
        grid=(T // tm,),
        in_specs=[pl.BlockSpec((tm, D), lambda i: (i, 0)),
                  pl.BlockSpec((None, 1, D), lambda i: (layer, 0, 0))],
        out_specs=pl.BlockSpec((tm, D), lambda i: (i, 0)),
        out_shape=jax.ShapeDtypeStruct((T, D), BF16),
        compiler_params=_params(1),
        name="rms_norm_bf16",
    )(x, gains)


def _ffn_kernel(h_ref, x_ref, wg_ref, wu_ref, wd_ref, wgut_ref, wdt_ref, gn_ref,
                *out_refs, n_full, n_chunk, rows, final):
    o_ref = out_refs[0]
    j = pl.program_id(1)

    def swiglu(g, u):
        return (0.5 * (g * jax.nn.sigmoid(g)) * u).astype(BF16)

    @pl.when(j == 0)
    def _():
        tail = wdt_ref.shape[0]
        w_gu, w_d = wgut_ref[...].astype(BF16), wdt_ref[...].astype(BF16)
        half = h_ref.shape[0] // 2
        gus = [_dot(h_ref[r0:r0 + half, :], w_gu) for r0 in (0, half)]
        for r0, gu in zip((0, half), gus):
            o_ref[r0:r0 + half, :] = _dot(swiglu(gu[:, :tail], gu[:, tail:]), w_d)

    @pl.when((j > 0) & (j < n_full))
    def _():
        h = h_ref[...]
        act = swiglu(_dot(h, wg_ref[...].astype(BF16)), _dot(h, wu_ref[...].astype(BF16)))
        o_ref[...] += _dot(act, wd_ref[...].astype(BF16))

    @pl.when(j < n_chunk)
    def _():
        r0 = pl.multiple_of(j * rows, rows)
        o_ref[pl.ds(r0, rows), :] += x_ref[...]

    @pl.when(j == n_full)
    def _():
        w_g, w_u, w_d = wg_ref[...].astype(BF16), wu_ref[...].astype(BF16), wd_ref[...].astype(BF16)
        half = h_ref.shape[0] // 2
        acts = [swiglu(_dot(h_ref[r0:r0 + half, :], w_g), _dot(h_ref[r0:r0 + half, :], w_u))
                for r0 in (0, half)]
        for r0, act in zip((0, half), acts):
            xn = o_ref[r0:r0 + half, :] + _dot(act, w_d)
            if final:
                o_ref[r0:r0 + half, :] = _rms(xn, gn_ref[...])
            else:
                o_ref[r0:r0 + half, :] = xn
                out_refs[1][r0:r0 + half, :] = _rms(xn, gn_ref[...]).astype(BF16)


def _ffn_call(h, x, w_gate, w_up, w_down, layer, gains, gain_layer, final):
    T, D = x.shape
    d_ff = w_gate.shape[-1]
    tm, tf, tail = 1024, 256, 128
    n_full = d_ff // tf
    assert n_full * tf + tail == d_ff
    tail_blk = (n_full * tf) // tail
    n_chunk = 8
    rows = tm // n_chunk
    kern = functools.partial(_ffn_kernel, n_full=n_full, n_chunk=n_chunk, rows=rows, final=final)
    once = pl.Buffered(1)
    full = lambda j: jnp.maximum(j - 1, 0)
    w_gu_tail = jnp.concatenate([w_gate[layer, :, n_full * tf:], w_up[layer, :, n_full * tf:]], axis=-1)
    in_specs = [
        pl.BlockSpec((tm, D), lambda i, j: (i, 0)),
        pl.BlockSpec((rows, D), lambda i, j: (i * n_chunk + jnp.minimum(j, n_chunk - 1), 0)),
        pl.BlockSpec((None, D, tf), lambda i, j: (layer, 0, full(j))),
        pl.BlockSpec((None, D, tf), lambda i, j: (layer, 0, full(j))),
        pl.BlockSpec((None, tf, D), lambda i, j: (layer, full(j), 0)),
        pl.BlockSpec((D, 2 * tail), lambda i, j: (0, 0), pipeline_mode=once),
        pl.BlockSpec((None, tail, D), lambda i, j: (layer, tail_blk, 0), pipeline_mode=once),
        pl.BlockSpec((None, 1, D), lambda i, j: (gain_layer, 0, 0)),
    ]
    row_spec = pl.BlockSpec((tm, D), lambda i, j: (i, 0))
    if final:
        out_specs = row_spec
        out_shape = jax.ShapeDtypeStruct((T, D), F32)
    else:
        out_specs = [row_spec, row_spec]
        out_shape = [jax.ShapeDtypeStruct((T, D), F32), jax.ShapeDtypeStruct((T, D), BF16)]
    return pl.pallas_call(
        kern,
        grid=(T // tm, n_full + 1),
        in_specs=in_specs,
        out_specs=out_specs,
        out_shape=out_shape,
        compiler_params=_params(2),
        name="swiglu_half_step",
    )(h, x, w_gate, w_up, w_down, w_gu_tail, w_down, gains)


def _inproj_kernel(scale_ref, h_ref, w_ref, o_ref, *, n_sub):
    j = pl.program_id(1)
    r = _dot(h_ref[...], w_ref[...].astype(BF16))
    for t in range(n_sub):
        o_ref[t] = (r[:, t * LANES:(t + 1) * LANES] * scale_ref[j * n_sub + t]).astype(BF16)


def _inproj_call(h, w_in, layer):
    T, D = h.shape
    tm, tn = 1024, 8 * LANES
    n_sub = tn // LANES
    n_blk = w_in.shape[-1] // tn
    assert n_blk * n_sub == N_SLABS
    scale = np.ones((N_SLABS,), np.float32)
    scale[SLAB_AQ:SLAB_AK] = ATTN_Q_SCALE
    scale[SLAB_BQ:SLAB_BK] = DIFF_Q_SCALE
    scale[SLAB_CQ:SLAB_CK] = ATTN_Q_SCALE
    return pl.pallas_call(
        functools.partial(_inproj_kernel, n_sub=n_sub),
        grid=(T // tm, n_blk),
        in_specs=[pl.BlockSpec(memory_space=pltpu.SMEM),
                  pl.BlockSpec((tm, D), lambda i, j: (i, 0)),
                  pl.BlockSpec((None, D, tn), lambda i, j: (layer, 0, j))],
        out_specs=pl.BlockSpec((n_sub, tm, LANES), lambda i, j: (j, i, 0)),
        out_shape=jax.ShapeDtypeStruct((N_SLABS, T, LANES), BF16),
        compiler_params=_params(2),
        name="input_projection",
    )(jnp.asarray(scale), h, w_in)


DIL_CLASSES = 4
DIL_OFFSETS = 3
DIL_SAFE_BOUND = 100.0


def _dil_kernel(slopes_ref, q_ref, k_ref, v_ref, y_ref,
                stage, k4, v4, qf, q4, a1, m1, s1, a4, m4, s4, kn, bias_a, bias_b, *, tq, seq, blk):
    h = pl.program_id(1)
    i = pl.program_id(2)
    slope2 = slopes_ref[h] * LOG2E
    (_, d1), (_, d4), (_, d16) = DIL_BRANCHES
    per4, per16 = tq // d4, tq // d16
    n1, n4, n16 = blk, min(blk, per4), min(blk, per16)
    win1 = min(n1 + 2 * DIL_RADIUS, seq // d1)
    win4 = min(n4 + 2 * DIL_RADIUS, seq // d4)
    win16 = min(n16 + 2 * DIL_RADIUS, seq // d16)

    def classes(ref, n):
        return [ref[pl.ds(r, n, stride=DIL_CLASSES), :] for r in range(DIL_CLASSES)]

    @pl.when(i == 0)
    def _():
        for src, dst in ((k_ref, k4), (v_ref, v4)):
            stage[...] = src[...].astype(F32)
            if src is k_ref:
                k_sq = jnp.sum(stage[...] * stage[...], axis=-1, keepdims=True)
                kn[...] = jnp.broadcast_to(jnp.sqrt(jnp.max(k_sq, axis=0, keepdims=True)), kn.shape)
            for r, rows in enumerate(classes(stage, seq // DIL_CLASSES)):
                dst[r] = rows
        for idx, (dil, n_q, win, ref) in enumerate(((d1, n1, win1, bias_a), (d4, n4, win4, bias_a),
                                                    (d16, n16, win16, bias_b))):
            row = lax.broadcasted_iota(jnp.int32, (n_q, win), 0)
            col = lax.broadcasted_iota(jnp.int32, (n_q, win), 1)
            for case in range(DIL_OFFSETS):
                absrel = jnp.abs(col - row - case * DIL_RADIUS)
                tile = jnp.where(absrel <= DIL_RADIUS, (-slope2 * dil) * absrel.astype(F32), NEG)
                if ref is bias_a:
                    ref[idx, case] = tile
                else:
                    ref[case] = tile

    qf[...] = q_ref[...].astype(F32)
    for r, rows in enumerate(classes(qf, tq // DIL_CLASSES)):
        q4[r] = rows

    def window(u0, sub_len, win):
        sub_start = jnp.clip(u0 - DIL_RADIUS, 0, sub_len - win)
        return sub_start, lax.div(u0 - sub_start, DIL_RADIUS)

    q_sq = qf[...] * qf[...]
    q_norm_sq = jnp.sum(jnp.max(q_sq, axis=0, keepdims=True), axis=-1, keepdims=True)
    safe = jnp.max(jnp.sqrt(q_norm_sq) * kn[0:1, 0:1]) <= DIL_SAFE_BOUND

    def run(bounded):
        def softmax_block(s, v):
            wide = (s.shape[0], LANES)
            if bounded:
                m, p = None, jnp.exp2(s)
            else:
                m = jnp.max(s, axis=-1, keepdims=True)
                p = jnp.exp2(s - m)
                m = jnp.broadcast_to(m, wide)
            den = jnp.sum(p, axis=-1, keepdims=True)
            return _dot(p.astype(BF16), v), m, jnp.broadcast_to(den, wide)

        def load(refs, idx):
            return tuple(None if (bounded and (ref is m1 or ref is m4)) else ref[idx] for ref in refs)

        def store(refs, idx, vals):
            for ref, val in zip(refs, vals):
                if val is not None:
                    ref[idx] = val

        def merge(block, prev):
            (acc, m, den), (acc_p, m_p, den_p) = block, prev
            if bounded:
                return acc_p + acc, None, den_p + den
            m_new = jnp.maximum(m_p, m)
            w_p, w = jnp.exp2(m_p - m_new), jnp.exp2(m - m_new)
            return w_p * acc_p + w * acc, m_new, w_p * den_p + w * den

        blocks = []

        for b in range(tq // n1):
            def logits(b=b):
                start, case = window(i * tq + b * n1, seq // d1, win1)
                src = pl.ds(pl.multiple_of(start, DIL_RADIUS), win1)
                return _dot_nt(q_ref[b * n1:(b + 1) * n1, :], k_ref[src, :]) + bias_a[0, case], src

            def finish(s, src, b=b):
                dst = (slice(b * n1, (b + 1) * n1), slice(None))
                store((a1, m1, s1), dst, softmax_block(s, v_ref[src, :]))
                if b == tq // n1 - 1:
                    for r in range(DIL_CLASSES):
                        rows = (pl.ds(r, per4, stride=DIL_CLASSES), slice(None))
                        store((a4, m4, s4), r, load((a1, m1, s1), rows))
            blocks.append((logits, finish))

        for r in range(d4):
            for b in range(per4 // n4):
                def logits(r=r, b=b):
                    start, case = window(i * per4 + b * n4, seq // d4, win4)
                    src = pl.ds(pl.multiple_of(start, DIL_RADIUS), win4)
                    q = q4[r, b * n4:(b + 1) * n4, :].astype(BF16)
                    return _dot_nt(q, k4[r, src, :].astype(BF16)) + bias_a[1, case], src

                def finish(s, src, r=r, b=b):
                    dst = (r, slice(b * n4, (b + 1) * n4), slice(None))
                    store((a4, m4, s4), dst, merge(softmax_block(s, v4[r, src, :].astype(BF16)),
                                                   load((a4, m4, s4), dst)))
                blocks.append((logits, finish))

        for r16 in range(d16):
            r, a = r16 % DIL_CLASSES, r16 // DIL_CLASSES
            for b in range(per16 // n16):
                def logits(r=r, a=a, b=b):
                    start, case = window(i * per16 + b * n16, seq // d16, win16)
                    src = pl.ds(a + DIL_CLASSES * start, win16, stride=DIL_CLASSES)
                    q = q4[r, pl.ds(a + DIL_CLASSES * b * n16, n16, stride=DIL_CLASSES), :].astype(BF16)
                    return _dot_nt(q, k4[r, src, :].astype(BF16)) + bias_b[case], src

                def finish(s, src, r=r, a=a, b=b):
                    dst = (r, pl.ds(a + DIL_CLASSES * b * n16, n16, stride=DIL_CLASSES), slice(None))
                    store((a4, m4, s4), dst, merge(softmax_block(s, v4[r, src, :].astype(BF16)),
                                                   load((a4, m4, s4), dst)))
                blocks.append((logits, finish))

        ahead = 6
        queue = [blocks[n][0]() for n in range(ahead)]
        for n in range(len(blocks)):
            if n + ahead < len(blocks):
                queue.append(blocks[n + ahead][0]())
            blocks[n][1](*queue.pop(0))

        for r in range(DIL_CLASSES):
            a1[pl.ds(r, per4, stride=DIL_CLASSES), :] = a4[r] / s4[r]
        y_ref[...] = a1[...].astype(BF16)

    pl.when(safe)(lambda: run(True))
    pl.when(jnp.logical_not(safe))(lambda: run(False))


def _dilated_attention(qkv, slopes_a, batch, seq):
    T = batch * seq
    tq, blk = 1024, 128
    assert tuple(d for _, d in DIL_BRANCHES) == (1, DIL_CLASSES, DIL_CLASSES ** 2)
    assert tq % (8 * DIL_CLASSES ** 2) == 0 and seq % tq == 0
    per16 = tq // DIL_CLASSES ** 2
    qkv_v = qkv.reshape(N_SLABS, batch, seq, LANES)
    kern = functools.partial(_dil_kernel, tq=tq, seq=seq, blk=blk)
    cls = lambda n: pltpu.VMEM((DIL_CLASSES, n // DIL_CLASSES, LANES), F32)
    tile = lambda n: pltpu.VMEM((n, LANES), F32)
    win = blk + 2 * DIL_RADIUS
    out = pl.pallas_call(
        kern,
        grid=(batch, DIL_HEADS, seq // tq),
        in_specs=[
            pl.BlockSpec(memory_space=pltpu.SMEM),
            pl.BlockSpec((None, None, tq, LANES), lambda b, h, i: (SLAB_AQ + h, b, i, 0)),
            pl.BlockSpec((None, None, seq, LANES), lambda b, h, i: (SLAB_AK + h, b, 0, 0)),
            pl.BlockSpec((None, None, seq, LANES), lambda b, h, i: (SLAB_AV + h, b, 0, 0)),
        ],
        out_specs=pl.BlockSpec((None, None, tq, LANES), lambda b, h, i: (h, b, i, 0)),
        out_shape=jax.ShapeDtypeStruct((DIL_HEADS, batch, seq, LANES), BF16),
        scratch_shapes=[tile(seq), cls(seq), cls(seq), tile(tq), cls(tq),
                        tile(tq), tile(tq), tile(tq), cls(tq), cls(tq), cls(tq),
                        tile(8),
                        pltpu.VMEM((2, DIL_OFFSETS, blk, win), F32),
                        pltpu.VMEM((DIL_OFFSETS, per16, min(per16 + 2 * DIL_RADIUS, seq // DIL_CLASSES ** 2)), F32)],
        compiler_params=_params(3),
        name="dilated_attention",
    )(slopes_a, qkv_v, qkv_v, qkv_v)
    return out.reshape(DIL_HEADS, T, LANES)


DIFF_FEAT = 3
DIFF_VT_ROWS = HEAD_DIM + 16
DIFF_SAFE_GAP = 100.0


def _diff_kernel(slopes_ref, q_ref, k_ref, v_ref, lq1_ref, lk1_ref, lq2_ref, lk2_ref, sg_ref,
                 y_ref, ka_ref, vt_ref, qt_ref, kn_ref, *, tq, tk, seq, lambda_init):
    h = pl.program_id(1)
    i = pl.program_id(2)
    slope2 = slopes_ref[h] * LOG2E
    n_kb = seq // tk
    half = DIFF_QK_DIM

    @pl.when(i == 0)
    def _():
        lane = lax.broadcasted_iota(jnp.int32, (tk, LANES), 1)
        k_local = lax.broadcasted_iota(jnp.int32, (tk, LANES), 0)
        rem = slope2 * k_local.astype(F32)
        parts = []
        for _ in range(DIFF_FEAT):
            part = rem.astype(BF16).astype(F32)
            parts.append(part)
            rem = rem - part
        ones_row = (lax.broadcasted_iota(jnp.int32, (DIFF_VT_ROWS - HEAD_DIM, tk), 0) == 0)
        k_sq = [jnp.zeros((1, 1), F32)] * 2
        for c in range(n_kb):
            kf = k_ref[c * tk:(c + 1) * tk, :].astype(F32)
            for mp in range(2):
                lo, fo = mp * half, half - mp * half
                ka = jnp.where((lane >= lo) & (lane < lo + half), kf, 0.0)
                k_sq[mp] = jnp.maximum(k_sq[mp], jnp.max(jnp.sum(ka * ka, axis=-1, keepdims=True),
                                                         axis=0, keepdims=True))
                for t, part in enumerate(parts):
                    ka = jnp.where(lane == fo + t, part, ka)
                ka_ref[mp, c * tk:(c + 1) * tk, :] = ka.astype(BF16)
            vt_ref[c, :HEAD_DIM, :] = v_ref[c * tk:(c + 1) * tk, :].astype(F32).T.astype(BF16)
            vt_ref[c, HEAD_DIM:, :] = ones_row.astype(BF16)
        for mp in range(2):
            kn_ref[mp] = jnp.broadcast_to(jnp.sqrt(k_sq[mp]), kn_ref.shape[1:])

    qt = q_ref[...].astype(F32).T
    row = lax.broadcasted_iota(jnp.int32, (LANES, tq), 0)
    for mp in range(2):
        lo, fo = mp * half, half - mp * half
        own = jnp.where((row >= lo) & (row < lo + half), qt, 0.0)
        feat = (row >= fo) & (row < fo + DIFF_FEAT)
        qt_ref[0, mp] = jnp.where(feat, 1.0, own).astype(BF16)
        qt_ref[1, mp] = jnp.where(feat, -1.0, own).astype(BF16)
        qt_ref[2, mp] = own.astype(BF16)

    q0 = i * tq
    kb_diag = q0 // tk
    q_local = lax.broadcasted_iota(jnp.int32, (1, tq), 1).astype(F32)

    def logits(j):
        kb = kb_diag + j
        kb = jnp.where(kb >= n_kb, kb - n_kb, kb)
        k0 = pl.multiple_of(kb * tk, tk)
        if j < max(1, tq // tk):
            krow = lax.broadcasted_iota(jnp.int32, (tk, tq), 0)
            qcol = lax.broadcasted_iota(jnp.int32, (tk, tq), 1)
            tile = slope2 * jnp.abs((qcol - krow) + (q0 - k0)).astype(F32)
            return kb, 0.0, [_dot(ka_ref[mp, pl.ds(k0, tk), :], qt_ref[2, mp]) - tile for mp in range(2)]
        right = (kb > kb_diag).astype(jnp.int32)
        sgn = (1 - 2 * right).astype(F32)
        row_term = (-sgn * slope2) * ((q0 - k0).astype(F32) + q_local)
        return kb, row_term, [_dot(ka_ref[mp, pl.ds(k0, tk), :], qt_ref[right, mp]) for mp in range(2)]

    def stream(consume):
        ahead = 3
        queue = [logits(j) for j in range(min(ahead, n_kb))]
        for j in range(n_kb):
            cur = queue.pop(0)
            if j + ahead < n_kb:
                queue.append(logits(j + ahead))
            consume(*cur)

    def finish(acc0, acc1):
        lam = (jnp.exp(jnp.sum(lq1_ref[...] * lk1_ref[...], axis=-1, keepdims=True))
               - jnp.exp(jnp.sum(lq2_ref[...] * lk2_ref[...], axis=-1, keepdims=True))
               + lambda_init)
        o0 = acc0[:HEAD_DIM] / acc0[HEAD_DIM:HEAD_DIM + 1]
        o1 = acc1[:HEAD_DIM] / acc1[HEAD_DIM:HEAD_DIM + 1]
        o = (o0 - lam * o1).T
        y_ref[...] = (_rms(o, sg_ref[...]) * (1.0 - lambda_init)).astype(BF16)

    kt = k_ref[pl.ds(pl.multiple_of(q0, tq), tq), :].astype(F32).T
    bounds, gap = [], None
    for mp in range(2):
        own = (row >= mp * half) & (row < mp * half + half)
        q_norm = jnp.sqrt(jnp.sum(jnp.where(own, qt * qt, 0.0), axis=0, keepdims=True))
        self_logit = jnp.sum(jnp.where(own, qt * kt, 0.0), axis=0, keepdims=True)
        bound = q_norm * kn_ref[mp]
        bounds.append(bound)
        worst = jnp.max(bound - self_logit)
        gap = worst if gap is None else jnp.maximum(gap, worst)
    safe = gap <= DIFF_SAFE_GAP

    @pl.when(safe)
    def _():
        accs = [jnp.zeros((DIFF_VT_ROWS, tq), F32) for _ in range(2)]

        def consume(kb, row_term, s_maps):
            vt = vt_ref[kb]
            for mp, s in enumerate(s_maps):
                accs[mp] = accs[mp] + _dot(vt, jnp.exp2(s - (bounds[mp] - row_term)).astype(BF16))

        stream(consume)
        finish(*accs)

    @pl.when(jnp.logical_not(safe))
    def _():
        state = []

        def consume(kb, row_term, s_maps):
            vt = vt_ref[kb]
            part = []
            for s in s_maps:
                m = jnp.max(s, axis=0, keepdims=True)
                part.append((m + row_term, _dot(vt, jnp.exp2(s - m).astype(BF16))))
            if not state:
                state.extend(part)
                return
            for mp, ((m, acc), (m_b, acc_b)) in enumerate(zip(list(state), part)):
                m_new = jnp.maximum(m, m_b)
                state[mp] = (m_new, jnp.exp2(m - m_new) * acc + jnp.exp2(m_b - m_new) * acc_b)

        stream(consume)
        finish(state[0][1], state[1][1])


def _diff_call(qkv, slopes_b, lq1, lk1, lq2, lk2, subln, layer, lambda_init, batch, seq):
    T = batch * seq
    tq, tk = 512, 256
    assert (tk % tq == 0 or tq % tk == 0) and seq % tk == 0 and seq % tq == 0
    qkv_v = qkv.reshape(N_SLABS, batch, seq, LANES)
    vec = lambda a: a.reshape(a.shape[0], 1, a.shape[1])
    lam_spec = pl.BlockSpec((None, 1, DIFF_QK_DIM), lambda b, h, i: (layer, 0, 0))
    kern = functools.partial(_diff_kernel, tq=tq, tk=tk, seq=seq, lambda_init=lambda_init)
    out = pl.pallas_call(
        kern,
        grid=(batch, DIFF_HEADS, seq // tq),
        in_specs=[
            pl.BlockSpec(memory_space=pltpu.SMEM),
            pl.BlockSpec((None, None, tq, LANES), lambda b, h, i: (SLAB_BQ + h, b, i, 0)),
            pl.BlockSpec((None, None, seq, LANES), lambda b, h, i: (SLAB_BK + h, b, 0, 0)),
            pl.BlockSpec((None, None, seq, LANES), lambda b, h, i: (SLAB_BV + h, b, 0, 0)),
            lam_spec, lam_spec, lam_spec, lam_spec,
            pl.BlockSpec((None, 1, HEAD_DIM), lambda b, h, i: (layer, 0, 0)),
        ],
        out_specs=pl.BlockSpec((None, None, tq, LANES), lambda b, h, i: (h, b, i, 0)),
        out_shape=jax.ShapeDtypeStruct((DIFF_HEADS, batch, seq, LANES), BF16),
        scratch_shapes=[pltpu.VMEM((2, seq, LANES), BF16), pltpu.VMEM((seq // tk, DIFF_VT_ROWS, tk), BF16),
                        pltpu.VMEM((3, 2, LANES, tq), BF16), pltpu.VMEM((2, 1, tq), F32)],
        compiler_params=_params(3),
        name="differential_attention",
    )(slopes_b, qkv_v, qkv_v, qkv_v, vec(lq1), vec(lk1), vec(lq2), vec(lk2), vec(subln))
    return out.reshape(DIFF_HEADS, T, LANES)


def _swa_kernel(slopes_ref, sink_ref, q_ref, k_ref, v_ref, y_ref, bias_ref, *, tq, win, seq, layer,
                tiles_per_iter):
    g = pl.program_id(1)
    heads = [g * SWA_GROUP + r for r in range(SWA_GROUP)]
    sinks2 = [sink_ref[layer * SWA_Q_HEADS + hd] * LOG2E for hd in heads]

    row = lax.broadcasted_iota(jnp.int32, (tq, win), 0)
    col = lax.broadcasted_iota(jnp.int32, (tq, win), 1)
    for case in range(SWA_OFFSETS):
        absrel = jnp.abs(col - row - case * SWA_RADIUS)
        for r, hd in enumerate(heads):
            slope2 = slopes_ref[hd] * LOG2E
            bias_ref[r, case] = jnp.where(absrel <= SWA_RADIUS, -slope2 * absrel.astype(F32), NEG)

    def tiles(it, carry):
        work = []
        for t in range(tiles_per_iter):
            q0 = pl.multiple_of((it * tiles_per_iter + t) * tq, tq)
            start = pl.multiple_of(jnp.clip(q0 - SWA_RADIUS, 0, seq - win), SWA_RADIUS)
            case = lax.div(q0 - start, SWA_RADIUS)
            k = k_ref[pl.ds(start, win), :]
            for r in range(SWA_GROUP):
                work.append((r, q0, start, _dot_nt(q_ref[r, pl.ds(q0, tq), :], k) + bias_ref[r, case]))
        for r, q0, start, s in work:
            m = jnp.maximum(jnp.max(s, axis=-1, keepdims=True), sinks2[r])
            p = jnp.exp2(s - m)
            den = jnp.sum(p, axis=-1, keepdims=True) + jnp.exp2(sinks2[r] - m)
            o = _dot(p.astype(BF16), v_ref[pl.ds(start, win), :]) / den
            y_ref[r, pl.ds(q0, tq), :] = o.astype(BF16)
        return carry

    lax.fori_loop(0, seq // (tq * tiles_per_iter), tiles, 0)


def _swa_call(qkv, slopes_c, sink_flat, layer, batch, seq):
    T = batch * seq
    tq = 256
    win = tq + 2 * SWA_RADIUS
    qkv_v = qkv.reshape(N_SLABS, batch, seq, LANES)
    tiles_per_iter = 2
    assert seq % (tq * tiles_per_iter) == 0
    kern = functools.partial(_swa_kernel, tq=tq, win=win, seq=seq, layer=layer, tiles_per_iter=tiles_per_iter)
    out = pl.pallas_call(
        kern,
        grid=(batch, SWA_KV_HEADS),
        in_specs=[
            pl.BlockSpec(memory_space=pltpu.SMEM),
            pl.BlockSpec(memory_space=pltpu.SMEM),
            pl.BlockSpec((SWA_GROUP, None, seq, LANES), lambda b, g: (SLAB_CQ // SWA_GROUP + g, b, 0, 0)),
            pl.BlockSpec((None, None, seq, LANES), lambda b, g: (SLAB_CK + g, b, 0, 0)),
            pl.BlockSpec((None, None, seq, LANES), lambda b, g: (SLAB_CV + g, b, 0, 0)),
        ],
        out_specs=pl.BlockSpec((SWA_GROUP, None, seq, LANES), lambda b, g: (g, b, 0, 0)),
        out_shape=jax.ShapeDtypeStruct((SWA_Q_HEADS, batch, seq, LANES), BF16),
        scratch_shapes=[pltpu.VMEM((SWA_GROUP, SWA_OFFSETS, tq, win), F32)],
        compiler_params=_params(2),
        name="window_gqa_attention",
    )(slopes_c, sink_flat, qkv_v, qkv_v, qkv_v)
    return out.reshape(SWA_Q_HEADS, T, LANES)


def _outproj_kernel(ya_ref, yb_ref, yc_ref, x_ref, w_ref, gn_ref, o_ref, hn_ref, wb_ref, *, cast_rows):
    @pl.when(pl.program_id(0) == 0)
    def _():
        for r0 in range(0, wb_ref.shape[0], cast_rows):
            wb_ref[r0:r0 + cast_rows, :] = w_ref[r0:r0 + cast_rows, :].astype(BF16)

    slabs = [y_ref[s] for y_ref in (ya_ref, yb_ref, yc_ref) for s in range(y_ref.shape[0])]
    y = jnp.concatenate(slabs, axis=-1)
    xn = x_ref[...] + _dot(y, wb_ref[...])
    o_ref[...] = xn
    hn_ref[...] = _rms(xn, gn_ref[...]).astype(BF16)


def _outproj_call(ya, yb, yc, x, w_out, layer, gains, gain_layer):
    T, D = x.shape
    tm = 512
    row_spec = pl.BlockSpec((tm, D), lambda i: (i, 0))
    slab_spec = lambda y: pl.BlockSpec((y.shape[0], tm, LANES), lambda i: (0, i, 0))
    return pl.pallas_call(
        functools.partial(_outproj_kernel, cast_rows=256),
        grid=(T // tm,),
        in_specs=[
            slab_spec(ya), slab_spec(yb), slab_spec(yc),
            row_spec,
            pl.BlockSpec((None,) + w_out.shape[1:], lambda i: (layer, 0, 0), pipeline_mode=pl.Buffered(1)),
            pl.BlockSpec((None, 1, D), lambda i: (gain_layer, 0, 0)),
        ],
        out_specs=[row_spec, row_spec],
        out_shape=[jax.ShapeDtypeStruct((T, D), F32), jax.ShapeDtypeStruct((T, D), BF16)],
        scratch_shapes=[pltpu.VMEM(w_out.shape[1:], BF16)],
        compiler_params=_params(1),
        name="output_projection",
    )(ya, yb, yc, x, w_out, gains)


def kernel(x, ffn1_norm, ffn1_w_gate, ffn1_w_up, ffn1_w_down, mix_norm, w_in, w_out, diff_lambda_q1, diff_lambda_k1, diff_lambda_q2, diff_lambda_k2, diff_subln, swa_sink, ffn2_norm, ffn2_w_gate, ffn2_w_up, ffn2_w_down, final_norm):
    batch, seq, d_model = x.shape
    depth = w_in.shape[0]
    xf = x.reshape(batch * seq, d_model)
    gain3 = lambda a: a.reshape(a.shape[0], 1, a.shape[1])
    ffn1_g, mix_g, ffn2_g = gain3(ffn1_norm), gain3(mix_norm), gain3(ffn2_norm)
    final_g = final_norm.reshape(1, 1, d_model)
    slopes_a = jnp.asarray(SLOPES_A, F32)
    slopes_b = jnp.asarray(SLOPES_B, F32)
    slopes_c = jnp.asarray(SLOPES_C, F32)
    sink_flat = swa_sink.astype(F32).reshape(-1)

    h = _norm_call(xf, ffn1_g, 0)
    for l in range(depth):
        lambda_init = 0.8 - 0.6 * math.exp(-0.3 * l)
        xf, h = _ffn_call(h, xf, ffn1_w_gate, ffn1_w_up, ffn1_w_down, l, mix_g, l, final=False)
        qkv = _inproj_call(h, w_in, l)
        ya = _dilated_attention(qkv, slopes_a, batch, seq)
        yb = _diff_call(qkv, slopes_b, diff_lambda_q1, diff_lambda_k1, diff_lambda_q2, diff_lambda_k2,
                        diff_subln, l, lambda_init, batch, seq)
        yc = _swa_call(qkv, slopes_c, sink_flat, l, batch, seq)
        xf, h = _outproj_call(ya, yb, yc, xf, w_out, l, ffn2_g, l)
        if l + 1 < depth:
            xf, h = _ffn_call(h, xf, ffn2_w_gate, ffn2_w_up, ffn2_w_down, l, ffn1_g, l + 1, final=False)
        else:
            xf = _ffn_call(h, xf, ffn2_w_gate, ffn2_w_up, ffn2_w_down, l, final_g, 0, final=True)
    return xf.reshape(batch, seq, d_model)
```

```python
import functools
import math

import numpy as np
import jax
import jax.numpy as jnp
from jax import lax
from jax.experimental import pallas as pl
from jax.experimental.pallas import tpu as pltpu

F32 = jnp.float32
BF16 = jnp.bfloat16

HEAD_DIM = 128
DIL_HEADS = 6
DIL_BRANCHES = ((128, 1), (512, 4), (2048, 16))
DIL_RADIUS = 64
DIFF_HEADS = 4
DIFF_QK_DIM = HEAD_DIM // 2
SWA_Q_HEADS = 6
SWA_KV_HEADS = 2
SWA_GROUP = SWA_Q_HEADS // SWA_KV_HEADS
SWA_RADIUS = 128
SWA_OFFSETS = 3
N_ALIBI_HEADS = SWA_Q_HEADS + DIL_HEADS + DIFF_HEADS
RMS_EPS = 1e-6
NEG = -1e30
ATTN_SCALE = HEAD_DIM ** -0.5
LOG2E = math.log2(math.e)
ATTN_Q_SCALE = ATTN_SCALE * LOG2E
DIFF_Q_SCALE = DIFF_QK_DIM ** -0.5 * LOG2E

SLAB_AQ, SLAB_AK, SLAB_AV = 0, 6, 12
SLAB_BQ, SLAB_BK, SLAB_BV = 18, 22, 26
SLAB_CQ, SLAB_CK, SLAB_CV = 30, 36, 38
N_SLABS = 40

LANES = 128
VMEM_LIMIT_BYTES = 60000 * 1024

_SLOPES = [float(np.float32(2.0 ** (-8.0 * (n + 1) / N_ALIBI_HEADS))) for n in range(N_ALIBI_HEADS)]
SLOPES_C = _SLOPES[:SWA_Q_HEADS]
SLOPES_A = _SLOPES[SWA_Q_HEADS:SWA_Q_HEADS + DIL_HEADS]
SLOPES_B = _SLOPES[SWA_Q_HEADS + DIL_HEADS:]


def _params(n_grid):
    return pltpu.CompilerParams(dimension_semantics=("arbitrary",) * n_grid,
                                vmem_limit_bytes=VMEM_LIMIT_BYTES)


def _rms(x, g):
    ms = jnp.mean(x * x, axis=-1, keepdims=True)
    return x * lax.rsqrt(ms + RMS_EPS) * g


def _dot(a, b):
    return jnp.dot(a, b, preferred_element_type=F32)


def _dot_nt(a, b):
    return lax.dot_general(a, b, (((1,), (1,)), ((), ())), preferred_element_type=F32)


def _norm_kernel(x_ref, g_ref, o_ref):
    o_ref[...] = _rms(x_ref[...], g_ref[...]).astype(BF16)


def _norm_call(x, gains, layer):
    T, D = x.shape
    tm = 512
    return pl.pallas_call(
        _norm_kernel,
        grid=(T // tm,),
        in_specs=[pl.BlockSpec((tm, D), lambda i: (i, 0)),
                  pl.BlockSpec((None, 1, D), lambda i: (layer, 0, 0))],
        out_specs=pl.BlockSpec((tm, D), lambda i: (i, 0)),
        out_shape=jax.ShapeDtypeStruct((T, D), BF16),
        compiler_params=_params(1),
        name="rms_norm_bf16",
    )(x, gains)


def _ffn_kernel(h_ref, x_ref, wg_ref, wu_ref, wd_ref, wgut_ref, wdt_ref, gn_ref,
                *out_refs, n_full, n_chunk, rows, final):
    o_ref = out_refs[0]
    j = pl.program_id(1)

    def swiglu(g, u):
        return (0.5 * (g * jax.nn.sigmoid(g)) * u).astype(BF16)

    @pl.when(j == 0)
    def _():
        tail = wdt_ref.shape[0]
        w_gu, w_d = wgut_ref[...].astype(BF16), wdt_ref[...].astype(BF16)
        half = h_ref.shape[0] // 2
        gus = [_dot(h_ref[r0:r0 + half, :], w_gu) for r0 in (0, half)]
        for r0, gu in zip((0, half), gus):
            o_ref[r0:r0 + half, :] = _dot(swiglu(gu[:, :tail], gu[:, tail:]), w_d)

    @pl.when((j > 0) & (j < n_full))
    def _():
        h = h_ref[...]
        act = swiglu(_dot(h, wg_ref[...].astype(BF16)), _dot(h, wu_ref[...].astype(BF16)))
        o_ref[...] += _dot(act, wd_ref[...].astype(BF16))

    @pl.when(j < n_chunk)
    def _():
        r0 = pl.multiple_of(j * rows, rows)
        o_ref[pl.ds(r0, rows), :] += x_ref[...]

    @pl.when(j == n_full)
    def _():
        w_g, w_u, w_d = wg_ref[...].astype(BF16), wu_ref[...].astype(BF16), wd_ref[...].astype(BF16)
        half = h_ref.shape[0] // 2
        acts = [swiglu(_dot(h_ref[r0:r0 + half, :], w_g), _dot(h_ref[r0:r0 + half, :], w_u))
                for r0 in (0, half)]
        for r0, act in zip((0, half), acts):
            xn = o_ref[r0:r0 + half, :] + _dot(act, w_d)
            if final:
                o_ref[r0:r0 + half, :] = _rms(xn, gn_ref[...])
            else:
                o_ref[r0:r0 + half, :] = xn
                out_refs[1][r0:r0 + half, :] = _rms(xn, gn_ref[...]).astype(BF16)


def _ffn_call(h, x, w_gate, w_up, w_down, layer, gains, gain_layer, final):
    T, D = x.shape
    d_ff = w_gate.shape[-1]
    tm, tf, tail = 1024, 256, 128
    n_full = d_ff // tf
    assert n_full * tf + tail == d_ff
    tail_blk = (n_full * tf) // tail
    n_chunk = 8
    rows = tm // n_chunk
    kern = functools.partial(_ffn_kernel, n_full=n_full, n_chunk=n_chunk, rows=rows, final=final)
    once = pl.Buffered(1)
    full = lambda j: jnp.maximum(j - 1, 0)
    w_gu_tail = jnp.concatenate([w_gate[layer, :, n_full * tf:], w_up[layer, :, n_full * tf:]], axis=-1)
    in_specs = [
        pl.BlockSpec((tm, D), lambda i, j: (i, 0)),
        pl.BlockSpec((rows, D), lambda i, j: (i * n_chunk + jnp.minimum(j, n_chunk - 1), 0)),
        pl.BlockSpec((None, D, tf), lambda i, j: (layer, 0, full(j))),
        pl.BlockSpec((None, D, tf), lambda i, j: (layer, 0, full(j))),
        pl.BlockSpec((None, tf, D), lambda i, j: (layer, full(j), 0)),
        pl.BlockSpec((D, 2 * tail), lambda i, j: (0, 0), pipeline_mode=once),
        pl.BlockSpec((None, tail, D), lambda i, j: (layer, tail_blk, 0), pipeline_mode=once),
        pl.BlockSpec((None, 1, D), lambda i, j: (gain_layer, 0, 0)),
    ]
    row_spec = pl.BlockSpec((tm, D), lambda i, j: (i, 0))
    if final:
        out_specs = row_spec
        out_shape = jax.ShapeDtypeStruct((T, D), F32)
    else:
        out_specs = [row_spec, row_spec]
        out_shape = [jax.ShapeDtypeStruct((T, D), F32), jax.ShapeDtypeStruct((T, D), BF16)]
    return pl.pallas_call(
        kern,
        grid=(T // tm, n_full + 1),
        in_specs=in_specs,
        out_specs=out_specs,
        out_shape=out_shape,
        compiler_params=_params(2),
        name="swiglu_half_step",
    )(h, x, w_gate, w_up, w_down, w_gu_tail, w_down, gains)


def _inproj_kernel(scale_ref, h_ref, w_ref, o_ref, *, n_sub):
    j = pl.program_id(1)
    r = _dot(h_ref[...], w_ref[...].astype(BF16))
    for t in range(n_sub):
        o_ref[t] = (r[:, t * LANES:(t + 1) * LANES] * scale_ref[j * n_sub + t]).astype(BF16)


def _inproj_call(h, w_in, layer):
    T, D = h.shape
    tm, tn = 1024, 8 * LANES
    n_sub = tn // LANES
    n_blk = w_in.shape[-1] // tn
    assert n_blk * n_sub == N_SLABS
    scale = np.ones((N_SLABS,), np.float32)
    scale[SLAB_AQ:SLAB_AK] = ATTN_Q_SCALE
    scale[SLAB_BQ:SLAB_BK] = DIFF_Q_SCALE
    scale[SLAB_CQ:SLAB_CK] = ATTN_Q_SCALE
    return pl.pallas_call(
        functools.partial(_inproj_kernel, n_sub=n_sub),
        grid=(T // tm, n_blk),
        in_specs=[pl.BlockSpec(memory_space=pltpu.SMEM),
                  pl.BlockSpec((tm, D), lambda i, j: (i, 0)),
                  pl.BlockSpec((None, D, tn), lambda i, j: (layer, 0, j))],
        out_specs=pl.BlockSpec((n_sub, tm, LANES), lambda i, j: (j, i, 0)),
        out_shape=jax.ShapeDtypeStruct((N_SLABS, T, LANES), BF16),
        compiler_params=_params(2),
        name="input_projection",
    )(jnp.asarray(scale), h, w_in)


DIL_CLASSES = 4
DIL_OFFSETS = 3
DIL_SAFE_BOUND = 100.0


def _dil_kernel(slopes_ref, q_ref, k_ref, v_ref, y_ref,
                stage, k4, v4, qf, q4, a1, m1, s1, a4, m4, s4, kn, bias_a, bias_b, *, tq, seq, blk):
    h = pl.program_id(1)
    i = pl.program_id(2)
    slope2 = slopes_ref[h] * LOG2E
    (_, d1), (_, d4), (_, d16) = DIL_BRANCHES
    per4, per16 = tq // d4, tq // d16
    n1, n4, n16 = blk, min(blk, per4), min(blk, per16)
    win1 = min(n1 + 2 * DIL_RADIUS, seq // d1)
    win4 = min(n4 + 2 * DIL_RADIUS, seq // d4)
    win16 = min(n16 + 2 * DIL_RADIUS, seq // d16)

    def classes(ref, n):
        return [ref[pl.ds(r, n, stride=DIL_CLASSES), :] for r in range(DIL_CLASSES)]

    @pl.when(i == 0)
    def _():
        for src, dst in ((k_ref, k4), (v_ref, v4)):
            stage[...] = src[...].astype(F32)
            if src is k_ref:
                k_sq = jnp.sum(stage[...] * stage[...], axis=-1, keepdims=True)
                kn[...] = jnp.broadcast_to(jnp.sqrt(jnp.max(k_sq, axis=0, keepdims=True)), kn.shape)
            for r, rows in enumerate(classes(stage, seq // DIL_CLASSES)):
                dst[r] = rows
        for idx, (dil, n_q, win, ref) in enumerate(((d1, n1, win1, bias_a), (d4, n4, win4, bias_a),
                                                    (d16, n16, win16, bias_b))):
            row = lax.broadcasted_iota(jnp.int32, (n_q, win), 0)
            col = lax.broadcasted_iota(jnp.int32, (n_q, win), 1)
            for case in range(DIL_OFFSETS):
                absrel = jnp.abs(col - row - case * DIL_RADIUS)
                tile = jnp.where(absrel <= DIL_RADIUS, (-slope2 * dil) * absrel.astype(F32), NEG)
                if ref is bias_a:
                    ref[idx, case] = tile
                else:
                    ref[case] = tile

    qf[...] = q_ref[...].astype(F32)
    for r, rows in enumerate(classes(qf, tq // DIL_CLASSES)):
        q4[r] = rows

    def window(u0, sub_len, win):
        sub_start = jnp.clip(u0 - DIL_RADIUS, 0, sub_len - win)
        return sub_start, lax.div(u0 - sub_start, DIL_RADIUS)

    q_sq = qf[...] * qf[...]
    q_norm_sq = jnp.sum(jnp.max(q_sq, axis=0, keepdims=True), axis=-1, keepdims=True)
    safe = jnp.max(jnp.sqrt(q_norm_sq) * kn[0:1, 0:1]) <= DIL_SAFE_BOUND

    def run(bounded):
        def softmax_block(s, v):
            wide = (s.shape[0], LANES)
            if bounded:
                m, p = None, jnp.exp2(s)
            else:
                m = jnp.max(s, axis=-1, keepdims=True)
                p = jnp.exp2(s - m)
                m = jnp.broadcast_to(m, wide)
            den = jnp.sum(p, axis=-1, keepdims=True)
            return _dot(p.astype(BF16), v), m, jnp.broadcast_to(den, wide)

        def load(refs, idx):
            return tuple(None if (bounded and (ref is m1 or ref is m4)) else ref[idx] for ref in refs)

        def store(refs, idx, vals):
            for ref, val in zip(refs, vals):
                if val is not None:
                    ref[idx] = val

        def merge(block, prev):
            (acc, m, den), (acc_p, m_p, den_p) = block, prev
            if bounded:
                return acc_p + acc, None, den_p + den
            m_new = jnp.maximum(m_p, m)
            w_p, w = jnp.exp2(m_p - m_new), jnp.exp2(m - m_new)
            return w_p * acc_p + w * acc, m_new, w_p * den_p + w * den

        blocks = []

        for b in range(tq // n1):
            def logits(b=b):
                start, case = window(i * tq + b * n1, seq // d1, win1)
                src = pl.ds(pl.multiple_of(start, DIL_RADIUS), win1)
                return _dot_nt(q_ref[b * n1:(b + 1) * n1, :], k_ref[src, :]) + bias_a[0, case], src

            def finish(s, src, b=b):
                dst = (slice(b * n1, (b + 1) * n1), slice(None))
                store((a1, m1, s1), dst, softmax_block(s, v_ref[src, :]))
                if b == tq // n1 - 1:
                    for r in range(DIL_CLASSES):
                        rows = (pl.ds(r, per4, stride=DIL_CLASSES), slice(None))
                        store((a4, m4, s4), r, load((a1, m1, s1), rows))
            blocks.append((logits, finish))

        for r in range(d4):
            for b in range(per4 // n4):
                def logits(r=r, b=b):
                    start, case = window(i * per4 + b * n4, seq // d4, win4)
                    src = pl.ds(pl.multiple_of(start, DIL_RADIUS), win4)
                    q = q4[r, b * n4:(b + 1) * n4, :].astype(BF16)
                    return _dot_nt(q, k4[r, src, :].astype(BF16)) + bias_a[1, case], src

                def finish(s, src, r=r, b=b):
                    dst = (r, slice(b * n4, (b + 1) * n4), slice(None))
                    store((a4, m4, s4), dst, merge(softmax_block(s, v4[r, src, :].astype(BF16)),
                                                   load((a4, m4, s4), dst)))
                blocks.append((logits, finish))

        for r16 in range(d16):
            r, a = r16 % DIL_CLASSES, r16 // DIL_CLASSES
            for b in range(per16 // n16):
                def logits(r=r, a=a, b=b):
                    start, case = window(i * per16 + b * n16, seq // d16, win16)
                    src = pl.ds(a + DIL_CLASSES * start, win16, stride=DIL_CLASSES)
                    q = q4[r, pl.ds(a + DIL_CLASSES * b * n16, n16, stride=DIL_CLASSES), :].astype(BF16)
                    return _dot_nt(q, k4[r, src, :].astype(BF16)) + bias_b[case], src

                def finish(s, src, r=r, a=a, b=b):
                    dst = (r, pl.ds(a + DIL_CLASSES * b * n16, n16, stride=DIL_CLASSES), slice(None))
                    store((a4, m4, s4), dst, merge(softmax_block(s, v4[r, src, :].astype(BF16)),
                                                   load((a4, m4, s4), dst)))
                blocks.append((logits, finish))

        ahead = 6
        queue = [blocks[n][0]() for n in range(ahead)]
        for n in range(len(blocks)):
            if n + ahead < len(blocks):
                queue.append(blocks[n + ahead][0]())
            blocks[n][1](*queue.pop(0))

        for r in range(DIL_CLASSES):
            a1[pl.ds(r, per4, stride=DIL_CLASSES), :] = a4[r] / s4[r]
        y_ref[...] = a1[...].astype(BF16)

    pl.when(safe)(lambda: run(True))
    pl.when(jnp.logical_not(safe))(lambda: run(False))


def _dilated_attention(qkv, slopes_a, batch, seq):
    T = batch * seq
    tq, blk = 1024, 128
    assert tuple(d for _, d in DIL_BRANCHES) == (1, DIL_CLASSES, DIL_CLASSES ** 2)
    assert tq % (8 * DIL_CLASSES ** 2) == 0 and seq % tq == 0
    per16 = tq // DIL_CLASSES ** 2
    qkv_v = qkv.reshape(N_SLABS, batch, seq, LANES)
    kern = functools.partial(_dil_kernel, tq=tq, seq=seq, blk=blk)
    cls = lambda n: pltpu.VMEM((DIL_CLASSES, n // DIL_CLASSES, LANES), F32)
    tile = lambda n: pltpu.VMEM((n, LANES), F32)
    win = blk + 2 * DIL_RADIUS
    out = pl.pallas_call(
        kern,
        grid=(batch, DIL_HEADS, seq // tq),
        in_specs=[
            pl.BlockSpec(memory_space=pltpu.SMEM),
            pl.BlockSpec((None, None, tq, LANES), lambda b, h, i: (SLAB_AQ + h, b, i, 0)),
            pl.BlockSpec((None, None, seq, LANES), lambda b, h, i: (SLAB_AK + h, b, 0, 0)),
            pl.BlockSpec((None, None, seq, LANES), lambda b, h, i: (SLAB_AV + h, b, 0, 0)),
        ],
        out_specs=pl.BlockSpec((None, None, tq, LANES), lambda b, h, i: (h, b, i, 0)),
        out_shape=jax.ShapeDtypeStruct((DIL_HEADS, batch, seq, LANES), BF16),
        scratch_shapes=[tile(seq), cls(seq), cls(seq), tile(tq), cls(tq),
                        tile(tq), tile(tq), tile(tq), cls(tq), cls(tq), cls(tq),
                        tile(8),
                        pltpu.VMEM((2, DIL_OFFSETS, blk, win), F32),
                        pltpu.VMEM((DIL_OFFSETS, per16, min(per16 + 2 * DIL_RADIUS, seq // DIL_CLASSES ** 2)), F32)],
        compiler_params=_params(3),
        name="dilated_attention",
    )(slopes_a, qkv_v, qkv_v, qkv_v)
    return out.reshape(DIL_HEADS, T, LANES)


DIFF_FEAT = 3
DIFF_VT_ROWS = HEAD_DIM + 16
DIFF_SAFE_BOUND = 100.0


def _diff_kernel(slopes_ref, q_ref, k_ref, v_ref, lq1_ref, lk1_ref, lq2_ref, lk2_ref, sg_ref,
                 y_ref, ka_ref, vt_ref, qt_ref, kn_ref, *, tq, tk, seq, lambda_init):
    h = pl.program_id(1)
    i = pl.program_id(2)
    slope2 = slopes_ref[h] * LOG2E
    n_kb = seq // tk
    half = DIFF_QK_DIM

    @pl.when(i == 0)
    def _():
        lane = lax.broadcasted_iota(jnp.int32, (tk, LANES), 1)
        k_local = lax.broadcasted_iota(jnp.int32, (tk, LANES), 0)
        rem = slope2 * k_local.astype(F32)
        parts = []
        for _ in range(DIFF_FEAT):
            part = rem.astype(BF16).astype(F32)
            parts.append(part)
            rem = rem - part
        ones_row = (lax.broadcasted_iota(jnp.int32, (DIFF_VT_ROWS - HEAD_DIM, tk), 0) == 0)
        k_sq = [jnp.zeros((1, 1), F32)] * 2
        for c in range(n_kb):
            kf = k_ref[c * tk:(c + 1) * tk, :].astype(F32)
            for mp in range(2):
                lo, fo = mp * half, half - mp * half
                ka = jnp.where((lane >= lo) & (lane < lo + half), kf, 0.0)
                k_sq[mp] = jnp.maximum(k_sq[mp], jnp.max(jnp.sum(ka * ka, axis=-1, keepdims=True),
                                                         axis=0, keepdims=True))
                for t, part in enumerate(parts):
                    ka = jnp.where(lane == fo + t, part, ka)
                ka_ref[mp, c * tk:(c + 1) * tk, :] = ka.astype(BF16)
            vt_ref[c, :HEAD_DIM, :] = v_ref[c * tk:(c + 1) * tk, :].astype(F32).T.astype(BF16)
            vt_ref[c, HEAD_DIM:, :] = ones_row.astype(BF16)
        for mp in range(2):
            kn_ref[mp] = jnp.broadcast_to(jnp.sqrt(k_sq[mp]), kn_ref.shape[1:])

    qt = q_ref[...].astype(F32).T
    row = lax.broadcasted_iota(jnp.int32, (LANES, tq), 0)
    for mp in range(2):
        lo, fo = mp * half, half - mp * half
        own = jnp.where((row >= lo) & (row < lo + half), qt, 0.0)
        feat = (row >= fo) & (row < fo + DIFF_FEAT)
        qt_ref[0, mp] = jnp.where(feat, 1.0, own).astype(BF16)
        qt_ref[1, mp] = jnp.where(feat, -1.0, own).astype(BF16)
        qt_ref[2, mp] = own.astype(BF16)

    q0 = i * tq
    kb_diag = q0 // tk
    q_local = lax.broadcasted_iota(jnp.int32, (1, tq), 1).astype(F32)

    def logits(j, complete):
        kb = kb_diag + j
        kb = jnp.where(kb >= n_kb, kb - n_kb, kb)
        k0 = pl.multiple_of(kb * tk, tk)
        if j < max(1, tq // tk):
            krow = lax.broadcasted_iota(jnp.int32, (tk, tq), 0)
            qcol = lax.broadcasted_iota(jnp.int32, (tk, tq), 1)
            tile = slope2 * jnp.abs((qcol - krow) + (q0 - k0)).astype(F32)
            return kb, 0.0, [_dot(ka_ref[mp, pl.ds(k0, tk), :], qt_ref[2, mp]) - tile for mp in range(2)]
        right = (kb > kb_diag).astype(jnp.int32)
        sgn = (1 - 2 * right).astype(F32)
        row_term = (-sgn * slope2) * ((q0 - k0).astype(F32) + q_local)
        s_maps = [_dot(ka_ref[mp, pl.ds(k0, tk), :], qt_ref[right, mp]) for mp in range(2)]
        if complete:
            return kb, 0.0, [s + row_term for s in s_maps]
        return kb, row_term, s_maps

    def stream(consume, complete):
        ahead = 3
        queue = [logits(j, complete) for j in range(min(ahead, n_kb))]
        for j in range(n_kb):
            cur = queue.pop(0)
            if j + ahead < n_kb:
                queue.append(logits(j + ahead, complete))
            consume(*cur)

    def finish(acc0, acc1):
        lam = (jnp.exp(jnp.sum(lq1_ref[...] * lk1_ref[...], axis=-1, keepdims=True))
               - jnp.exp(jnp.sum(lq2_ref[...] * lk2_ref[...], axis=-1, keepdims=True))
               + lambda_init)
        o0 = acc0[:HEAD_DIM] / acc0[HEAD_DIM:HEAD_DIM + 1]
        o1 = acc1[:HEAD_DIM] / acc1[HEAD_DIM:HEAD_DIM + 1]
        o = (o0 - lam * o1).T
        y_ref[...] = (_rms(o, sg_ref[...]) * (1.0 - lambda_init)).astype(BF16)

    bound = None
    for mp in range(2):
        own = (row >= mp * half) & (row < mp * half + half)
        q_norm = jnp.sqrt(jnp.sum(jnp.where(own, qt * qt, 0.0), axis=0, keepdims=True))
        worst = jnp.max(q_norm * kn_ref[mp])
        bound = worst if bound is None else jnp.maximum(bound, worst)
    safe = bound <= DIFF_SAFE_BOUND

    @pl.when(safe)
    def _():
        accs = [jnp.zeros((DIFF_VT_ROWS, tq), F32) for _ in range(2)]

        def consume(kb, row_term, s_maps):
            vt = vt_ref[kb]
            for mp, s in enumerate(s_maps):
                accs[mp] = accs[mp] + _dot(vt, jnp.exp2(s).astype(BF16))

        stream(consume, True)
        finish(*accs)

    @pl.when(jnp.logical_not(safe))
    def _():
        state = []

        def consume(kb, row_term, s_maps):
            vt = vt_ref[kb]
            part = []
            for s in s_maps:
                m = jnp.max(s, axis=0, keepdims=True)
                part.append((m + row_term, _dot(vt, jnp.exp2(s - m).astype(BF16))))
            if not state:
                state.extend(part)
                return
            for mp, ((m, acc), (m_b, acc_b)) in enumerate(zip(list(state), part)):
                m_new = jnp.maximum(m, m_b)
                state[mp] = (m_new, jnp.exp2(m - m_new) * acc + jnp.exp2(m_b - m_new) * acc_b)

        stream(consume, False)
        finish(state[0][1], state[1][1])


def _diff_call(qkv, slopes_b, lq1, lk1, lq2, lk2, subln, layer, lambda_init, batch, seq):
    T = batch * seq
    tq, tk = 512, 256
    assert (tk % tq == 0 or tq % tk == 0) and seq % tk == 0 and seq % tq == 0
    qkv_v = qkv.reshape(N_SLABS, batch, seq, LANES)
    vec = lambda a: a.reshape(a.shape[0], 1, a.shape[1])
    lam_spec = pl.BlockSpec((None, 1, DIFF_QK_DIM), lambda b, h, i: (layer, 0, 0))
    kern = functools.partial(_diff_kernel, tq=tq, tk=tk, seq=seq, lambda_init=lambda_init)
    out = pl.pallas_call(
        kern,
        grid=(batch, DIFF_HEADS, seq // tq),
        in_specs=[
            pl.BlockSpec(memory_space=pltpu.SMEM),
            pl.BlockSpec((None, None, tq, LANES), lambda b, h, i: (SLAB_BQ + h, b, i, 0)),
            pl.BlockSpec((None, None, seq, LANES), lambda b, h, i: (SLAB_BK + h, b, 0, 0)),
            pl.BlockSpec((None, None, seq, LANES), lambda b, h, i: (SLAB_BV + h, b, 0, 0)),
            lam_spec, lam_spec, lam_spec, lam_spec,
            pl.BlockSpec((None, 1, HEAD_DIM), lambda b, h, i: (layer, 0, 0)),
        ],
        out_specs=pl.BlockSpec((None, None, tq, LANES), lambda b, h, i: (h, b, i, 0)),
        out_shape=jax.ShapeDtypeStruct((DIFF_HEADS, batch, seq, LANES), BF16),
        scratch_shapes=[pltpu.VMEM((2, seq, LANES), BF16), pltpu.VMEM((seq // tk, DIFF_VT_ROWS, tk), BF16),
                        pltpu.VMEM((3, 2, LANES, tq), BF16), pltpu.VMEM((2, 1, tq), F32)],
        compiler_params=_params(3),
        name="differential_attention",
    )(slopes_b, qkv_v, qkv_v, qkv_v, vec(lq1), vec(lk1), vec(lq2), vec(lk2), vec(subln))
    return out.reshape(DIFF_HEADS, T, LANES)


def _swa_kernel(slopes_ref, sink_ref, q_ref, k_ref, v_ref, y_ref, bias_ref, *, tq, win, seq, layer,
                tiles_per_iter):
    g = pl.program_id(1)
    heads = [g * SWA_GROUP + r for r in range(SWA_GROUP)]
    sinks2 = [sink_ref[layer * SWA_Q_HEADS + hd] * LOG2E for hd in heads]

    row = lax.broadcasted_iota(jnp.int32, (tq, win), 0)
    col = lax.broadcasted_iota(jnp.int32, (tq, win), 1)
    for case in range(SWA_OFFSETS):
        absrel = jnp.abs(col - row - case * SWA_RADIUS)
        for r, hd in enumerate(heads):
            slope2 = slopes_ref[hd] * LOG2E
            bias_ref[r, case] = jnp.where(absrel <= SWA_RADIUS, -slope2 * absrel.astype(F32), NEG)

    def tiles(it, carry):
        work = []
        for t in range(tiles_per_iter):
            q0 = pl.multiple_of((it * tiles_per_iter + t) * tq, tq)
            start = pl.multiple_of(jnp.clip(q0 - SWA_RADIUS, 0, seq - win), SWA_RADIUS)
            case = lax.div(q0 - start, SWA_RADIUS)
            k = k_ref[pl.ds(start, win), :]
            for r in range(SWA_GROUP):
                work.append((r, q0, start, _dot_nt(q_ref[r, pl.ds(q0, tq), :], k) + bias_ref[r, case]))
        for r, q0, start, s in work:
            m = jnp.maximum(jnp.max(s, axis=-1, keepdims=True), sinks2[r])
            p = jnp.exp2(s - m)
            den = jnp.sum(p, axis=-1, keepdims=True) + jnp.exp2(sinks2[r] - m)
            o = _dot(p.astype(BF16), v_ref[pl.ds(start, win), :]) / den
            y_ref[r, pl.ds(q0, tq), :] = o.astype(BF16)
        return carry

    lax.fori_loop(0, seq // (tq * tiles_per_iter), tiles, 0)


def _swa_call(qkv, slopes_c, sink_flat, layer, batch, seq):
    T = batch * seq
    tq = 256
    win = tq + 2 * SWA_RADIUS
    qkv_v = qkv.reshape(N_SLABS, batch, seq, LANES)
    tiles_per_iter = 2
    assert seq % (tq * tiles_per_iter) == 0
    kern = functools.partial(_swa_kernel, tq=tq, win=win, seq=seq, layer=layer, tiles_per_iter=tiles_per_iter)
    out = pl.pallas_call(
        kern,
        grid=(batch, SWA_KV_HEADS),
        in_specs=[
            pl.BlockSpec(memory_space=pltpu.SMEM),
            pl.BlockSpec(memory_space=pltpu.SMEM),
            pl.BlockSpec((SWA_GROUP, None, seq, LANES), lambda b, g: (SLAB_CQ // SWA_GROUP + g, b, 0, 0)),
            pl.BlockSpec((None, None, seq, LANES), lambda b, g: (SLAB_CK + g, b, 0, 0)),
            pl.BlockSpec((None, None, seq, LANES), lambda b, g: (SLAB_CV + g, b, 0, 0)),
        ],
        out_specs=pl.BlockSpec((SWA_GROUP, None, seq, LANES), lambda b, g: (g, b, 0, 0)),
        out_shape=jax.ShapeDtypeStruct((SWA_Q_HEADS, batch, seq, LANES), BF16),
        scratch_shapes=[pltpu.VMEM((SWA_GROUP, SWA_OFFSETS, tq, win), F32)],
        compiler_params=_params(2),
        name="window_gqa_attention",
    )(slopes_c, sink_flat, qkv_v, qkv_v, qkv_v)
    return out.reshape(SWA_Q_HEADS, T, LANES)


def _outproj_kernel(ya_ref, yb_ref, yc_ref, x_ref, w_ref, gn_ref, o_ref, hn_ref, wb_ref, *, cast_rows):
    @pl.when(pl.program_id(0) == 0)
    def _():
        for r0 in range(0, wb_ref.shape[0], cast_rows):
            wb_ref[r0:r0 + cast_rows, :] = w_ref[r0:r0 + cast_rows, :].astype(BF16)

    slabs = [y_ref[s] for y_ref in (ya_ref, yb_ref, yc_ref) for s in range(y_ref.shape[0])]
    y = jnp.concatenate(slabs, axis=-1)
    xn = x_ref[...] + _dot(y, wb_ref[...])
    o_ref[...] = xn
    hn_ref[...] = _rms(xn, gn_ref[...]).astype(BF16)


def _outproj_call(ya, yb, yc, x, w_out, layer, gains, gain_layer):
    T, D = x.shape
    tm = 512
    row_spec = pl.BlockSpec((tm, D), lambda i: (i, 0))
    slab_spec = lambda y: pl.BlockSpec((y.shape[0], tm, LANES), lambda i: (0, i, 0))
    return pl.pallas_call(
        functools.partial(_outproj_kernel, cast_rows=256),
        grid=(T // tm,),
        in_specs=[
            slab_spec(ya), slab_spec(yb), slab_spec(yc),
            row_spec,
            pl.BlockSpec((None,) + w_out.shape[1:], lambda i: (layer, 0, 0), pipeline_mode=pl.Buffered(1)),
            pl.BlockSpec((None, 1, D), lambda i: (gain_layer, 0, 0)),
        ],
        out_specs=[row_spec, row_spec],
        out_shape=[jax.ShapeDtypeStruct((T, D), F32), jax.ShapeDtypeStruct((T, D), BF16)],
        scratch_shapes=[pltpu.VMEM(w_out.shape[1:], BF16)],
        compiler_params=_params(1),
        name="output_projection",
    )(ya, yb, yc, x, w_out, gains)


def kernel(x, ffn1_norm, ffn1_w_gate, ffn1_w_up, ffn1_w_down, mix_norm, w_in, w_out, diff_lambda_q1, diff_lambda_k1, diff_lambda_q2, diff_lambda_k2, diff_subln, swa_sink, ffn2_norm, ffn2_w_gate, ffn2_w_up, ffn2_w_down, final_norm):
    batch, seq, d_model = x.shape
    depth = w_in.shape[0]
    xf = x.reshape(batch * seq, d_model)
    gain3 = lambda a: a.reshape(a.shape[0], 1, a.shape[1])
    ffn1_g, mix_g, ffn2_g = gain3(ffn1_norm), gain3(mix_norm), gain3(ffn2_norm)
    final_g = final_norm.reshape(1, 1, d_model)
    slopes_a = jnp.asarray(SLOPES_A, F32)
    slopes_b = jnp.asarray(SLOPES_B, F32)
    slopes_c = jnp.asarray(SLOPES_C, F32)
    sink_flat = swa_sink.astype(F32).reshape(-1)

    h = _norm_call(xf, ffn1_g, 0)
    for l in range(depth):
        lambda_init = 0.8 - 0.6 * math.exp(-0.3 * l)
        xf, h = _ffn_call(h, xf, ffn1_w_gate, ffn1_w_up, ffn1_w_down, l, mix_g, l, final=False)
        qkv = _inproj_call(h, w_in, l)
        ya = _dilated_attention(qkv, slopes_a, batch, seq)
        yb = _diff_call(qkv, slopes_b, diff_lambda_q1, diff_lambda_k1, diff_lambda_q2, diff_lambda_k2,
                        diff_subln, l, lambda_init, batch, seq)
        yc = _swa_call(qkv, slopes_c, sink_flat, l, batch, seq)
        xf, h = _outproj_call(ya, yb, yc, xf, w_out, l, ffn2_g, l)
        if l + 1 < depth:
            xf, h = _ffn_call(h, xf, ffn2_w_gate, ffn2_w_up, ffn2_w_down, l, ffn1_g, l + 1, final=False)
        else:
            xf = _ffn_call(h, xf, ffn2_w_gate, ffn2_w_up, ffn2_w_down, l, final_g, 0, final=True)
    return xf.reshape(batch, seq, d_model)
```

```python
import functools
import math

import numpy as np
import jax
import jax.numpy as jnp
from jax import lax
from jax.experimental import pallas as pl
from jax.experimental.pallas import tpu as pltpu

F32 = jnp.float32
BF16 = jnp.bfloat16

HEAD_DIM = 128
DIL_HEADS = 6
DIL_BRANCHES = ((128, 1), (512, 4), (2048, 16))
DIL_RADIUS = 64
DIFF_HEADS = 4
DIFF_QK_DIM = HEAD_DIM // 2
SWA_Q_HEADS = 6
SWA_KV_HEADS = 2
SWA_GROUP = SWA_Q_HEADS // SWA_KV_HEADS
SWA_RADIUS = 128
SWA_OFFSETS = 3
N_ALIBI_HEADS = SWA_Q_HEADS + DIL_HEADS + DIFF_HEADS
RMS_EPS = 1e-6
NEG = -1e30
ATTN_SCALE = HEAD_DIM ** -0.5
LOG2E = math.log2(math.e)
ATTN_Q_SCALE = ATTN_SCALE * LOG2E
DIFF_Q_SCALE = DIFF_QK_DIM ** -0.5 * LOG2E

SLAB_AQ, SLAB_AK, SLAB_AV = 0, 6, 12
SLAB_BQ, SLAB_BK, SLAB_BV = 18, 22, 26
SLAB_CQ, SLAB_CK, SLAB_CV = 30, 36, 38
N_SLABS = 40

LANES = 128
VMEM_LIMIT_BYTES = 60000 * 1024

_SLOPES = [float(np.float32(2.0 ** (-8.0 * (n + 1) / N_ALIBI_HEADS))) for n in range(N_ALIBI_HEADS)]
SLOPES_C = _SLOPES[:SWA_Q_HEADS]
SLOPES_A = _SLOPES[SWA_Q_HEADS:SWA_Q_HEADS + DIL_HEADS]
SLOPES_B = _SLOPES[SWA_Q_HEADS + DIL_HEADS:]


def _params(n_grid):
    return pltpu.CompilerParams(dimension_semantics=("arbitrary",) * n_grid,
                                vmem_limit_bytes=VMEM_LIMIT_BYTES)


def _rms(x, g):
    ms = jnp.mean(x * x, axis=-1, keepdims=True)
    return x * lax.rsqrt(ms + RMS_EPS) * g


def _dot(a, b):
    return jnp.dot(a, b, preferred_element_type=F32)


def _dot_nt(a, b):
    return lax.dot_general(a, b, (((1,), (1,)), ((), ())), preferred_element_type=F32)


def _norm_kernel(x_ref, g_ref, o_ref):
    o_ref[...] = _rms(x_ref[...], g_ref[...]).astype(BF16)


def _norm_call(x, gains, layer):
    T, D = x.shape
    tm = 512
    return pl.pallas_call(
        _norm_kernel,
        grid=(T // tm,),
        in_specs=[pl.BlockSpec((tm, D), lambda i: (i, 0)),
                  pl.BlockSpec((None, 1, D), lambda i: (layer, 0, 0))],
        out_specs=pl.BlockSpec((tm, D), lambda i: (i, 0)),
        out_shape=jax.ShapeDtypeStruct((T, D), BF16),
        compiler_params=_params(1),
        name="rms_norm_bf16",
    )(x, gains)


def _ffn_kernel(h_ref, x_ref, wg_ref, wu_ref, wd_ref, wgut_ref, wdt_ref, gn_ref,
                *out_refs, n_full, n_chunk, rows, final):
    o_ref = out_refs[0]
    j = pl.program_id(1)

    def swiglu(g, u):
        return (0.5 * (g * jax.nn.sigmoid(g)) * u).astype(BF16)

    @pl.when(j == 0)
    def _():
        tail = wdt_ref.shape[0]
        w_gu, w_d = wgut_ref[...].astype(BF16), wdt_ref[...].astype(BF16)
        half = h_ref.shape[0] // 2
        gus = [_dot(h_ref[r0:r0 + half, :], w_gu) for r0 in (0, half)]
        for r0, gu in zip((0, half), gus):
            o_ref[r0:r0 + half, :] = _dot(swiglu(gu[:, :tail], gu[:, tail:]), w_d)

    @pl.when((j > 0) & (j < n_full))
    def _():
        h = h_ref[...]
        act = swiglu(_dot(h, wg_ref[...].astype(BF16)), _dot(h, wu_ref[...].astype(BF16)))
        o_ref[...] += _dot(act, wd_ref[...].astype(BF16))

    @pl.when(j < n_chunk)
    def _():
        r0 = pl.multiple_of(j * rows, rows)
        o_ref[pl.ds(r0, rows), :] += x_ref[...]

    @pl.when(j == n_full)
    def _():
        w_g, w_u, w_d = wg_ref[...].astype(BF16), wu_ref[...].astype(BF16), wd_ref[...].astype(BF16)
        half = h_ref.shape[0] // 2
        acts = [swiglu(_dot(h_ref[r0:r0 + half, :], w_g), _dot(h_ref[r0:r0 + half, :], w_u))
                for r0 in (0, half)]
        for r0, act in zip((0, half), acts):
            xn = o_ref[r0:r0 + half, :] + _dot(act, w_d)
            if final:
                o_ref[r0:r0 + half, :] = _rms(xn, gn_ref[...])
            else:
                o_ref[r0:r0 + half, :] = xn
                out_refs[1][r0:r0 + half, :] = _rms(xn, gn_ref[...]).astype(BF16)


def _ffn_call(h, x, w_gate, w_up, w_down, layer, gains, gain_layer, final):
    T, D = x.shape
    d_ff = w_gate.shape[-1]
    tm, tf, tail = 1024, 256, 128
    n_full = d_ff // tf
    assert n_full * tf + tail == d_ff
    tail_blk = (n_full * tf) // tail
    n_chunk = 8
    rows = tm // n_chunk
    kern = functools.partial(_ffn_kernel, n_full=n_full, n_chunk=n_chunk, rows=rows, final=final)
    once = pl.Buffered(1)
    full = lambda j: jnp.maximum(j - 1, 0)
    w_gu_tail = jnp.concatenate([w_gate[layer, :, n_full * tf:], w_up[layer, :, n_full * tf:]], axis=-1)
    in_specs = [
        pl.BlockSpec((tm, D), lambda i, j: (i, 0)),
        pl.BlockSpec((rows, D), lambda i, j: (i * n_chunk + jnp.minimum(j, n_chunk - 1), 0)),
        pl.BlockSpec((None, D, tf), lambda i, j: (layer, 0, full(j))),
        pl.BlockSpec((None, D, tf), lambda i, j: (layer, 0, full(j))),
        pl.BlockSpec((None, tf, D), lambda i, j: (layer, full(j), 0)),
        pl.BlockSpec((D, 2 * tail), lambda i, j: (0, 0), pipeline_mode=once),
        pl.BlockSpec((None, tail, D), lambda i, j: (layer, tail_blk, 0), pipeline_mode=once),
        pl.BlockSpec((None, 1, D), lambda i, j: (gain_layer, 0, 0)),
    ]
    row_spec = pl.BlockSpec((tm, D), lambda i, j: (i, 0))
    if final:
        out_specs = row_spec
        out_shape = jax.ShapeDtypeStruct((T, D), F32)
    else:
        out_specs = [row_spec, row_spec]
        out_shape = [jax.ShapeDtypeStruct((T, D), F32), jax.ShapeDtypeStruct((T, D), BF16)]
    return pl.pallas_call(
        kern,
        grid=(T // tm, n_full + 1),
        in_specs=in_specs,
        out_specs=out_specs,
        out_shape=out_shape,
        compiler_params=_params(2),
        name="swiglu_half_step",
    )(h, x, w_gate, w_up, w_down, w_gu_tail, w_down, gains)


def _inproj_kernel(scale_ref, h_ref, w_ref, o_ref, *, n_sub):
    j = pl.program_id(1)
    r = _dot(h_ref[...], w_ref[...].astype(BF16))
    for t in range(n_sub):
        o_ref[t] = (r[:, t * LANES:(t + 1) * LANES] * scale_ref[j * n_sub + t]).astype(BF16)


def _inproj_call(h, w_in, layer):
    T, D = h.shape
    tm, tn = 1024, 8 * LANES
    n_sub = tn // LANES
    n_blk = w_in.shape[-1] // tn
    assert n_blk * n_sub == N_SLABS
    scale = np.ones((N_SLABS,), np.float32)
    scale[SLAB_AQ:SLAB_AK] = ATTN_Q_SCALE
    scale[SLAB_BQ:SLAB_BK] = DIFF_Q_SCALE
    scale[SLAB_CQ:SLAB_CK] = ATTN_Q_SCALE
    return pl.pallas_call(
        functools.partial(_inproj_kernel, n_sub=n_sub),
        grid=(T // tm, n_blk),
        in_specs=[pl.BlockSpec(memory_space=pltpu.SMEM),
                  pl.BlockSpec((tm, D), lambda i, j: (i, 0)),
                  pl.BlockSpec((None, D, tn), lambda i, j: (layer, 0, j))],
        out_specs=pl.BlockSpec((n_sub, tm, LANES), lambda i, j: (j, i, 0)),
        out_shape=jax.ShapeDtypeStruct((N_SLABS, T, LANES), BF16),
        compiler_params=_params(2),
        name="input_projection",
    )(jnp.asarray(scale), h, w_in)


DIL_CLASSES = 4
DIL_OFFSETS = 3
DIL_SAFE_BOUND = 100.0


def _dil_kernel(slopes_ref, q_ref, k_ref, v_ref, y_ref,
                stage, k4, v4, qf, q4, a1, m1, s1, a4, m4, s4, kn, bias_a, bias_b, *, tq, seq, blk):
    h = pl.program_id(1)
    i = pl.program_id(2)
    slope2 = slopes_ref[h] * LOG2E
    (_, d1), (_, d4), (_, d16) = DIL_BRANCHES
    per4, per16 = tq // d4, tq // d16
    n1, n4, n16 = blk, min(blk, per4), min(blk, per16)
    win1 = min(n1 + 2 * DIL_RADIUS, seq // d1)
    win4 = min(n4 + 2 * DIL_RADIUS, seq // d4)
    win16 = min(n16 + 2 * DIL_RADIUS, seq // d16)

    def classes(ref, n):
        return [ref[pl.ds(r, n, stride=DIL_CLASSES), :] for r in range(DIL_CLASSES)]

    @pl.when(i == 0)
    def _():
        for src, dst in ((k_ref, k4), (v_ref, v4)):
            stage[...] = src[...].astype(F32)
            if src is k_ref:
                k_sq = jnp.sum(stage[...] * stage[...], axis=-1, keepdims=True)
                kn[...] = jnp.broadcast_to(jnp.sqrt(jnp.max(k_sq, axis=0, keepdims=True)), kn.shape)
            for r, rows in enumerate(classes(stage, seq // DIL_CLASSES)):
                dst[r] = rows
        for idx, (dil, n_q, win, ref) in enumerate(((d1, n1, win1, bias_a), (d4, n4, win4, bias_a),
                                                    (d16, n16, win16, bias_b))):
            row = lax.broadcasted_iota(jnp.int32, (n_q, win), 0)
            col = lax.broadcasted_iota(jnp.int32, (n_q, win), 1)
            for case in range(DIL_OFFSETS):
                absrel = jnp.abs(col - row - case * DIL_RADIUS)
                tile = jnp.where(absrel <= DIL_RADIUS, (-slope2 * dil) * absrel.astype(F32), NEG)
                if ref is bias_a:
                    ref[idx, case] = tile
                else:
                    ref[case] = tile

    qf[...] = q_ref[...].astype(F32)
    for r, rows in enumerate(classes(qf, tq // DIL_CLASSES)):
        q4[r] = rows

    def window(u0, sub_len, win):
        sub_start = jnp.clip(u0 - DIL_RADIUS, 0, sub_len - win)
        return sub_start, lax.div(u0 - sub_start, DIL_RADIUS)

    q_sq = qf[...] * qf[...]
    q_norm_sq = jnp.sum(jnp.max(q_sq, axis=0, keepdims=True), axis=-1, keepdims=True)
    safe = jnp.max(jnp.sqrt(q_norm_sq) * kn[0:1, 0:1]) <= DIL_SAFE_BOUND

    def run(bounded):
        def softmax_block(s, v):
            wide = (s.shape[0], LANES)
            if bounded:
                m, p = None, jnp.exp2(s)
            else:
                m = jnp.max(s, axis=-1, keepdims=True)
                p = jnp.exp2(s - m)
                m = jnp.broadcast_to(m, wide)
            den = jnp.sum(p, axis=-1, keepdims=True)
            return _dot(p.astype(BF16), v), m, jnp.broadcast_to(den, wide)

        def load(refs, idx):
            return tuple(None if (bounded and (ref is m1 or ref is m4)) else ref[idx] for ref in refs)

        def store(refs, idx, vals):
            for ref, val in zip(refs, vals):
                if val is not None:
                    ref[idx] = val

        def merge(block, prev):
            (acc, m, den), (acc_p, m_p, den_p) = block, prev
            if bounded:
                return acc_p + acc, None, den_p + den
            m_new = jnp.maximum(m_p, m)
            w_p, w = jnp.exp2(m_p - m_new), jnp.exp2(m - m_new)
            return w_p * acc_p + w * acc, m_new, w_p * den_p + w * den

        blocks = []

        for b in range(tq // n1):
            def logits(b=b):
                start, case = window(i * tq + b * n1, seq // d1, win1)
                src = pl.ds(pl.multiple_of(start, DIL_RADIUS), win1)
                return _dot_nt(q_ref[b * n1:(b + 1) * n1, :], k_ref[src, :]) + bias_a[0, case], src

            def finish(s, src, b=b):
                dst = (slice(b * n1, (b + 1) * n1), slice(None))
                store((a1, m1, s1), dst, softmax_block(s, v_ref[src, :]))
                if b == tq // n1 - 1:
                    for r in range(DIL_CLASSES):
                        rows = (pl.ds(r, per4, stride=DIL_CLASSES), slice(None))
                        store((a4, m4, s4), r, load((a1, m1, s1), rows))
            blocks.append((logits, finish))

        for r in range(d4):
            for b in range(per4 // n4):
                def logits(r=r, b=b):
                    start, case = window(i * per4 + b * n4, seq // d4, win4)
                    src = pl.ds(pl.multiple_of(start, DIL_RADIUS), win4)
                    q = q4[r, b * n4:(b + 1) * n4, :].astype(BF16)
                    return _dot_nt(q, k4[r, src, :].astype(BF16)) + bias_a[1, case], src

                def finish(s, src, r=r, b=b):
                    dst = (r, slice(b * n4, (b + 1) * n4), slice(None))
                    store((a4, m4, s4), dst, merge(softmax_block(s, v4[r, src, :].astype(BF16)),
                                                   load((a4, m4, s4), dst)))
                blocks.append((logits, finish))

        for r16 in range(d16):
            r, a = r16 % DIL_CLASSES, r16 // DIL_CLASSES
            for b in range(per16 // n16):
                def logits(r=r, a=a, b=b):
                    start, case = window(i * per16 + b * n16, seq // d16, win16)
                    src = pl.ds(a + DIL_CLASSES * start, win16, stride=DIL_CLASSES)
                    q = q4[r, pl.ds(a + DIL_CLASSES * b * n16, n16, stride=DIL_CLASSES), :].astype(BF16)
                    return _dot_nt(q, k4[r, src, :].astype(BF16)) + bias_b[case], src

                def finish(s, src, r=r, a=a, b=b):
                    dst = (r, pl.ds(a + DIL_CLASSES * b * n16, n16, stride=DIL_CLASSES), slice(None))
                    store((a4, m4, s4), dst, merge(softmax_block(s, v4[r, src, :].astype(BF16)),
                                                   load((a4, m4, s4), dst)))
                blocks.append((logits, finish))

        ahead = 6
        queue = [blocks[n][0]() for n in range(ahead)]
        for n in range(len(blocks)):
            if n + ahead < len(blocks):
                queue.append(blocks[n + ahead][0]())
            blocks[n][1](*queue.pop(0))

        for r in range(DIL_CLASSES):
            a1[pl.ds(r, per4, stride=DIL_CLASSES), :] = a4[r] / s4[r]
        y_ref[...] = a1[...].astype(BF16)

    pl.when(safe)(lambda: run(True))
    pl.when(jnp.logical_not(safe))(lambda: run(False))


def _dilated_attention(qkv, slopes_a, batch, seq):
    T = batch * seq
    tq, blk = 1024, 128
    assert tuple(d for _, d in DIL_BRANCHES) == (1, DIL_CLASSES, DIL_CLASSES ** 2)
    assert tq % (8 * DIL_CLASSES ** 2) == 0 and seq % tq == 0
    per16 = tq // DIL_CLASSES ** 2
    qkv_v = qkv.reshape(N_SLABS, batch, seq, LANES)
    kern = functools.partial(_dil_kernel, tq=tq, seq=seq, blk=blk)
    cls = lambda n: pltpu.VMEM((DIL_CLASSES, n // DIL_CLASSES, LANES), F32)
    tile = lambda n: pltpu.VMEM((n, LANES), F32)
    win = blk + 2 * DIL_RADIUS
    out = pl.pallas_call(
        kern,
        grid=(batch, DIL_HEADS, seq // tq),
        in_specs=[
            pl.BlockSpec(memory_space=pltpu.SMEM),
            pl.BlockSpec((None, None, tq, LANES), lambda b, h, i: (SLAB_AQ + h, b, i, 0)),
            pl.BlockSpec((None, None, seq, LANES), lambda b, h, i: (SLAB_AK + h, b, 0, 0)),
            pl.BlockSpec((None, None, seq, LANES), lambda b, h, i: (SLAB_AV + h, b, 0, 0)),
        ],
        out_specs=pl.BlockSpec((None, None, tq, LANES), lambda b, h, i: (h, b, i, 0)),
        out_shape=jax.ShapeDtypeStruct((DIL_HEADS, batch, seq, LANES), BF16),
        scratch_shapes=[tile(seq), cls(seq), cls(seq), tile(tq), cls(tq),
                        tile(tq), tile(tq), tile(tq), cls(tq), cls(tq), cls(tq),
                        tile(8),
                        pltpu.VMEM((2, DIL_OFFSETS, blk, win), F32),
                        pltpu.VMEM((DIL_OFFSETS, per16, min(per16 + 2 * DIL_RADIUS, seq // DIL_CLASSES ** 2)), F32)],
        compiler_params=_params(3),
        name="dilated_attention",
    )(slopes_a, qkv_v, qkv_v, qkv_v)
    return out.reshape(DIL_HEADS, T, LANES)


DIFF_FEAT = 3
DIFF_VT_ROWS = HEAD_DIM + 16
DIFF_SAFE_BOUND = 100.0


def _diff_kernel(slopes_ref, q_ref, k_ref, v_ref, lq1_ref, lk1_ref, lq2_ref, lk2_ref, sg_ref,
                 y_ref, ka_ref, vt_ref, qt_ref, kn_ref, *, tq, tk, seq, n_sub, lambda_init):
    h = pl.program_id(1)
    i = pl.program_id(2)
    slope2 = slopes_ref[h] * LOG2E
    n_kb = seq // tk
    half = DIFF_QK_DIM

    @pl.when(i == 0)
    def _():
        lane = lax.broadcasted_iota(jnp.int32, (tk, LANES), 1)
        k_local = lax.broadcasted_iota(jnp.int32, (tk, LANES), 0)
        rem = slope2 * k_local.astype(F32)
        parts = []
        for _ in range(DIFF_FEAT):
            part = rem.astype(BF16).astype(F32)
            parts.append(part)
            rem = rem - part
        ones_row = (lax.broadcasted_iota(jnp.int32, (DIFF_VT_ROWS - HEAD_DIM, tk), 0) == 0)
        k_sq = [jnp.zeros((1, 1), F32)] * 2
        for c in range(n_kb):
            kf = k_ref[c * tk:(c + 1) * tk, :].astype(F32)
            for mp in range(2):
                lo, fo = mp * half, half - mp * half
                ka = jnp.where((lane >= lo) & (lane < lo + half), kf, 0.0)
                k_sq[mp] = jnp.maximum(k_sq[mp], jnp.max(jnp.sum(ka * ka, axis=-1, keepdims=True),
                                                         axis=0, keepdims=True))
                for t, part in enumerate(parts):
                    ka = jnp.where(lane == fo + t, part, ka)
                ka_ref[mp, c * tk:(c + 1) * tk, :] = ka.astype(BF16)
            vt_ref[c, :HEAD_DIM, :] = v_ref[c * tk:(c + 1) * tk, :].astype(F32).T.astype(BF16)
            vt_ref[c, HEAD_DIM:, :] = ones_row.astype(BF16)
        for mp in range(2):
            kn_ref[mp] = jnp.broadcast_to(jnp.sqrt(k_sq[mp]), kn_ref.shape[1:])

    def prepare(t):
        qt = q_ref[t * tq:(t + 1) * tq, :].astype(F32).T
        row = lax.broadcasted_iota(jnp.int32, (LANES, tq), 0)
        bound = None
        for mp in range(2):
            lo, fo = mp * half, half - mp * half
            is_own = (row >= lo) & (row < lo + half)
            own = jnp.where(is_own, qt, 0.0)
            feat = (row >= fo) & (row < fo + DIFF_FEAT)
            qt_ref[t, 0, mp] = jnp.where(feat, 1.0, own).astype(BF16)
            qt_ref[t, 1, mp] = jnp.where(feat, -1.0, own).astype(BF16)
            qt_ref[t, 2, mp] = own.astype(BF16)
            q_norm = jnp.sqrt(jnp.sum(own * own, axis=0, keepdims=True))
            worst = jnp.max(q_norm * kn_ref[mp])
            bound = worst if bound is None else jnp.maximum(bound, worst)
        return (i * n_sub + t) * tq, bound

    q_local = lax.broadcasted_iota(jnp.int32, (1, tq), 1).astype(F32)

    def logits(t, q0, j, complete):
        kb_diag = q0 // tk
        kb = kb_diag + j
        kb = jnp.where(kb >= n_kb, kb - n_kb, kb)
        k0 = pl.multiple_of(kb * tk, tk)
        if j < max(1, tq // tk):
            krow = lax.broadcasted_iota(jnp.int32, (tk, tq), 0)
            qcol = lax.broadcasted_iota(jnp.int32, (tk, tq), 1)
            tile = slope2 * jnp.abs((qcol - krow) + (q0 - k0)).astype(F32)
            return kb, 0.0, [_dot(ka_ref[mp, pl.ds(k0, tk), :], qt_ref[t, 2, mp]) - tile for mp in range(2)]
        right = (kb > kb_diag).astype(jnp.int32)
        sgn = (1 - 2 * right).astype(F32)
        row_term = (-sgn * slope2) * ((q0 - k0).astype(F32) + q_local)
        s_maps = [_dot(ka_ref[mp, pl.ds(k0, tk), :], qt_ref[t, right, mp]) for mp in range(2)]
        if complete:
            return kb, 0.0, [s + row_term for s in s_maps]
        return kb, row_term, s_maps

    def stream(t, q0, consume, complete):
        ahead = 3
        queue = [logits(t, q0, j, complete) for j in range(min(ahead, n_kb))]
        for j in range(n_kb):
            cur = queue.pop(0)
            if j + ahead < n_kb:
                queue.append(logits(t, q0, j + ahead, complete))
            consume(*cur)

    def finish(t, acc0, acc1):
        lam = (jnp.exp(jnp.sum(lq1_ref[...] * lk1_ref[...], axis=-1, keepdims=True))
               - jnp.exp(jnp.sum(lq2_ref[...] * lk2_ref[...], axis=-1, keepdims=True))
               + lambda_init)
        o0 = acc0[:HEAD_DIM] / acc0[HEAD_DIM:HEAD_DIM + 1]
        o1 = acc1[:HEAD_DIM] / acc1[HEAD_DIM:HEAD_DIM + 1]
        o = (o0 - lam * o1).T
        y_ref[t * tq:(t + 1) * tq, :] = (_rms(o, sg_ref[...]) * (1.0 - lambda_init)).astype(BF16)

    tiles = [prepare(t) for t in range(n_sub)]
    bound = functools.reduce(jnp.maximum, [b for _, b in tiles])
    safe = bound <= DIFF_SAFE_BOUND

    @pl.when(safe)
    def _():
        for t, (q0, _) in enumerate(tiles):
            accs = [jnp.zeros((DIFF_VT_ROWS, tq), F32) for _ in range(2)]

            def consume(kb, row_term, s_maps, accs=accs):
                vt = vt_ref[kb]
                for mp, s in enumerate(s_maps):
                    accs[mp] = accs[mp] + _dot(vt, jnp.exp2(s).astype(BF16))

            stream(t, q0, consume, True)
            finish(t, *accs)

    @pl.when(jnp.logical_not(safe))
    def _():
        for t, (q0, _) in enumerate(tiles):
            state = []

            def consume(kb, row_term, s_maps, state=state):
                vt = vt_ref[kb]
                part = []
                for s in s_maps:
                    m = jnp.max(s, axis=0, keepdims=True)
                    part.append((m + row_term, _dot(vt, jnp.exp2(s - m).astype(BF16))))
                if not state:
                    state.extend(part)
                    return
                for mp, ((m, acc), (m_b, acc_b)) in enumerate(zip(list(state), part)):
                    m_new = jnp.maximum(m, m_b)
                    state[mp] = (m_new, jnp.exp2(m - m_new) * acc + jnp.exp2(m_b - m_new) * acc_b)

            stream(t, q0, consume, False)
            finish(t, state[0][1], state[1][1])


def _diff_call(qkv, slopes_b, lq1, lk1, lq2, lk2, subln, layer, lambda_init, batch, seq):
    T = batch * seq
    tq, tk = 512, 256
    assert (tk % tq == 0 or tq % tk == 0) and seq % tk == 0
    qkv_v = qkv.reshape(N_SLABS, batch, seq, LANES)
    vec = lambda a: a.reshape(a.shape[0], 1, a.shape[1])
    lam_spec = pl.BlockSpec((None, 1, DIFF_QK_DIM), lambda b, h, i: (layer, 0, 0))
    n_sub = 2
    assert seq % (tq * n_sub) == 0
    kern = functools.partial(_diff_kernel, tq=tq, tk=tk, seq=seq, n_sub=n_sub, lambda_init=lambda_init)
    out = pl.pallas_call(
        kern,
        grid=(batch, DIFF_HEADS, seq // (tq * n_sub)),
        in_specs=[
            pl.BlockSpec(memory_space=pltpu.SMEM),
            pl.BlockSpec((None, None, tq * n_sub, LANES), lambda b, h, i: (SLAB_BQ + h, b, i, 0)),
            pl.BlockSpec((None, None, seq, LANES), lambda b, h, i: (SLAB_BK + h, b, 0, 0)),
            pl.BlockSpec((None, None, seq, LANES), lambda b, h, i: (SLAB_BV + h, b, 0, 0)),
            lam_spec, lam_spec, lam_spec, lam_spec,
            pl.BlockSpec((None, 1, HEAD_DIM), lambda b, h, i: (layer, 0, 0)),
        ],
        out_specs=pl.BlockSpec((None, None, tq * n_sub, LANES), lambda b, h, i: (h, b, i, 0)),
        out_shape=jax.ShapeDtypeStruct((DIFF_HEADS, batch, seq, LANES), BF16),
        scratch_shapes=[pltpu.VMEM((2, seq, LANES), BF16), pltpu.VMEM((seq // tk, DIFF_VT_ROWS, tk), BF16),
                        pltpu.VMEM((n_sub, 3, 2, LANES, tq), BF16), pltpu.VMEM((2, 1, tq), F32)],
        compiler_params=_params(3),
        name="differential_attention",
    )(slopes_b, qkv_v, qkv_v, qkv_v, vec(lq1), vec(lk1), vec(lq2), vec(lk2), vec(subln))
    return out.reshape(DIFF_HEADS, T, LANES)


def _swa_kernel(slopes_ref, sink_ref, q_ref, k_ref, v_ref, y_ref, bias_ref, *, tq, win, seq, layer,
                tiles_per_iter):
    g = pl.program_id(1)
    heads = [g * SWA_GROUP + r for r in range(SWA_GROUP)]
    sinks2 = [sink_ref[layer * SWA_Q_HEADS + hd] * LOG2E for hd in heads]

    row = lax.broadcasted_iota(jnp.int32, (tq, win), 0)
    col = lax.broadcasted_iota(jnp.int32, (tq, win), 1)
    for case in range(SWA_OFFSETS):
        absrel = jnp.abs(col - row - case * SWA_RADIUS)
        for r, hd in enumerate(heads):
            slope2 = slopes_ref[hd] * LOG2E
            bias_ref[r, case] = jnp.where(absrel <= SWA_RADIUS, -slope2 * absrel.astype(F32), NEG)

    def tiles(it, carry):
        work = []
        for t in range(tiles_per_iter):
            q0 = pl.multiple_of((it * tiles_per_iter + t) * tq, tq)
            start = pl.multiple_of(jnp.clip(q0 - SWA_RADIUS, 0, seq - win), SWA_RADIUS)
            case = lax.div(q0 - start, SWA_RADIUS)
            k = k_ref[pl.ds(start, win), :]
            for r in range(SWA_GROUP):
                work.append((r, q0, start, _dot_nt(q_ref[r, pl.ds(q0, tq), :], k) + bias_ref[r, case]))
        for r, q0, start, s in work:
            m = jnp.maximum(jnp.max(s, axis=-1, keepdims=True), sinks2[r])
            p = jnp.exp2(s - m)
            den = jnp.sum(p, axis=-1, keepdims=True) + jnp.exp2(sinks2[r] - m)
            o = _dot(p.astype(BF16), v_ref[pl.ds(start, win), :]) / den
            y_ref[r, pl.ds(q0, tq), :] = o.astype(BF16)
        return carry

    lax.fori_loop(0, seq // (tq * tiles_per_iter), tiles, 0)


def _swa_call(qkv, slopes_c, sink_flat, layer, batch, seq):
    T = batch * seq
    tq = 256
    win = tq + 2 * SWA_RADIUS
    qkv_v = qkv.reshape(N_SLABS, batch, seq, LANES)
    tiles_per_iter = 2
    assert seq % (tq * tiles_per_iter) == 0
    kern = functools.partial(_swa_kernel, tq=tq, win=win, seq=seq, layer=layer, tiles_per_iter=tiles_per_iter)
    out = pl.pallas_call(
        kern,
        grid=(batch, SWA_KV_HEADS),
        in_specs=[
            pl.BlockSpec(memory_space=pltpu.SMEM),
            pl.BlockSpec(memory_space=pltpu.SMEM),
            pl.BlockSpec((SWA_GROUP, None, seq, LANES), lambda b, g: (SLAB_CQ // SWA_GROUP + g, b, 0, 0)),
            pl.BlockSpec((None, None, seq, LANES), lambda b, g: (SLAB_CK + g, b, 0, 0)),
            pl.BlockSpec((None, None, seq, LANES), lambda b, g: (SLAB_CV + g, b, 0, 0)),
        ],
        out_specs=pl.BlockSpec((SWA_GROUP, None, seq, LANES), lambda b, g: (g, b, 0, 0)),
        out_shape=jax.ShapeDtypeStruct((SWA_Q_HEADS, batch, seq, LANES), BF16),
        scratch_shapes=[pltpu.VMEM((SWA_GROUP, SWA_OFFSETS, tq, win), F32)],
        compiler_params=_params(2),
        name="window_gqa_attention",
    )(slopes_c, sink_flat, qkv_v, qkv_v, qkv_v)
    return out.reshape(SWA_Q_HEADS, T, LANES)


def _outproj_kernel(ya_ref, yb_ref, yc_ref, x_ref, w_ref, gn_ref, o_ref, hn_ref, wb_ref, *, cast_rows):
    @pl.when(pl.program_id(0) == 0)
    def _():
        for r0 in range(0, wb_ref.shape[0], cast_rows):
            wb_ref[r0:r0 + cast_rows, :] = w_ref[r0:r0 + cast_rows, :].astype(BF16)

    slabs = [y_ref[s] for y_ref in (ya_ref, yb_ref, yc_ref) for s in range(y_ref.shape[0])]
    y = jnp.concatenate(slabs, axis=-1)
    xn = x_ref[...] + _dot(y, wb_ref[...])
    o_ref[...] = xn
    hn_ref[...] = _rms(xn, gn_ref[...]).astype(BF16)


def _outproj_call(ya, yb, yc, x, w_out, layer, gains, gain_layer):
    T, D = x.shape
    tm = 512
    row_spec = pl.BlockSpec((tm, D), lambda i: (i, 0))
    slab_spec = lambda y: pl.BlockSpec((y.shape[0], tm, LANES), lambda i: (0, i, 0))
    return pl.pallas_call(
        functools.partial(_outproj_kernel, cast_rows=256),
        grid=(T // tm,),
        in_specs=[
            slab_spec(ya), slab_spec(yb), slab_spec(yc),
            row_spec,
            pl.BlockSpec((None,) + w_out.shape[1:], lambda i: (layer, 0, 0), pipeline_mode=pl.Buffered(1)),
            pl.BlockSpec((None, 1, D), lambda i: (gain_layer, 0, 0)),
        ],
        out_specs=[row_spec, row_spec],
        out_shape=[jax.ShapeDtypeStruct((T, D), F32), jax.ShapeDtypeStruct((T, D), BF16)],
        scratch_shapes=[pltpu.VMEM(w_out.shape[1:], BF16)],
        compiler_params=_params(1),
        name="output_projection",
    )(ya, yb, yc, x, w_out, gains)


def kernel(x, ffn1_norm, ffn1_w_gate, ffn1_w_up, ffn1_w_down, mix_norm, w_in, w_out, diff_lambda_q1, diff_lambda_k1, diff_lambda_q2, diff_lambda_k2, diff_subln, swa_sink, ffn2_norm, ffn2_w_gate, ffn2_w_up, ffn2_w_down, final_norm):
    batch, seq, d_model = x.shape
    depth = w_in.shape[0]
    xf = x.reshape(batch * seq, d_model)
    gain3 = lambda a: a.reshape(a.shape[0], 1, a.shape[1])
    ffn1_g, mix_g, ffn2_g = gain3(ffn1_norm), gain3(mix_norm), gain3(ffn2_norm)
    final_g = final_norm.reshape(1, 1, d_model)
    slopes_a = jnp.asarray(SLOPES_A, F32)
    slopes_b = jnp.asarray(SLOPES_B, F32)
    slopes_c = jnp.asarray(SLOPES_C, F32)
    sink_flat = swa_sink.astype(F32).reshape(-1)

    h = _norm_call(xf, ffn1_g, 0)
    for l in range(depth):
        lambda_init = 0.8 - 0.6 * math.exp(-0.3 * l)
        xf, h = _ffn_call(h, xf, ffn1_w_gate, ffn1_w_up, ffn1_w_down, l, mix_g, l, final=False)
        qkv = _inproj_call(h, w_in, l)
        ya = _dilated_attention(qkv, slopes_a, batch, seq)
        yb = _diff_call(qkv, slopes_b, diff_lambda_q1, diff_lambda_k1, diff_lambda_q2, diff_lambda_k2,
                        diff_subln, l, lambda_init, batch, seq)
        yc = _swa_call(qkv, slopes_c, sink_flat, l, batch, seq)
        xf, h = _outproj_call(ya, yb, yc, xf, w_out, l, ffn2_g, l)
        if l + 1 < depth:
            xf, h = _ffn_call(h, xf, ffn2_w_gate, ffn2_w_up, ffn2_w_down, l, ffn1_g, l + 1, final=False)
        else:
            xf = _ffn_call(h, xf, ffn2_w_gate, ffn2_w_up, ffn2_w_down, l, final_g, 0, final=True)
    return xf.reshape(batch, seq, d_model)
```

```python
import functools
import math

import numpy as np
import jax
import jax.numpy as jnp
from jax import lax
from jax.experimental import pallas as pl
from jax.experimental.pallas import tpu as pltpu

F32 = jnp.float32
BF16 = jnp.bfloat16

HEAD_DIM = 128
DIL_HEADS = 6
DIL_BRANCHES = ((128, 1), (512, 4), (2048, 16))
DIL_RADIUS = 64
DIFF_HEADS = 4
DIFF_QK_DIM = HEAD_DIM // 2
SWA_Q_HEADS = 6
SWA_KV_HEADS = 2
SWA_GROUP = SWA_Q_HEADS // SWA_KV_HEADS
SWA_RADIUS = 128
SWA_OFFSETS = 3
N_ALIBI_HEADS = SWA_Q_HEADS + DIL_HEADS + DIFF_HEADS
RMS_EPS = 1e-6
NEG = -1e30
ATTN_SCALE = HEAD_DIM ** -0.5
LOG2E = math.log2(math.e)
ATTN_Q_SCALE = ATTN_SCALE * LOG2E
DIFF_Q_SCALE = DIFF_QK_DIM ** -0.5 * LOG2E

SLAB_AQ, SLAB_AK, SLAB_AV = 0, 6, 12
SLAB_BQ, SLAB_BK, SLAB_BV = 18, 22, 26
SLAB_CQ, SLAB_CK, SLAB_CV = 30, 36, 38
N_SLABS = 40

LANES = 128
VMEM_LIMIT_BYTES = 60000 * 1024

_SLOPES = [float(np.float32(2.0 ** (-8.0 * (n + 1) / N_ALIBI_HEADS))) for n in range(N_ALIBI_HEADS)]
SLOPES_C = _SLOPES[:SWA_Q_HEADS]
SLOPES_A = _SLOPES[SWA_Q_HEADS:SWA_Q_HEADS + DIL_HEADS]
SLOPES_B = _SLOPES[SWA_Q_HEADS + DIL_HEADS:]


def _params(n_grid):
    return pltpu.CompilerParams(dimension_semantics=("arbitrary",) * n_grid,
                                vmem_limit_bytes=VMEM_LIMIT_BYTES)


def _rms(x, g):
    ms = jnp.mean(x * x, axis=-1, keepdims=True)
    return x * lax.rsqrt(ms + RMS_EPS) * g


def _dot(a, b):
    return jnp.dot(a, b, preferred_element_type=F32)


def _dot_nt(a, b):
    return lax.dot_general(a, b, (((1,), (1,)), ((), ())), preferred_element_type=F32)


def _norm_kernel(x_ref, g_ref, o_ref):
    o_ref[...] = _rms(x_ref[...], g_ref[...]).astype(BF16)


def _norm_call(x, gains, layer):
    T, D = x.shape
    tm = 512
    return pl.pallas_call(
        _norm_kernel,
        grid=(T // tm,),
        in_specs=[pl.BlockSpec((tm, D), lambda i: (i, 0)),
                  pl.BlockSpec((None, 1, D), lambda i: (layer, 0, 0))],
        out_specs=pl.BlockSpec((tm, D), lambda i: (i, 0)),
        out_shape=jax.ShapeDtypeStruct((T, D), BF16),
        compiler_params=_params(1),
        name="rms_norm_bf16",
    )(x, gains)


def _ffn_kernel(h_ref, x_ref, wg_ref, wu_ref, wd_ref, wgut_ref, wdt_ref, gn_ref,
                *out_refs, n_full, n_chunk, rows, final):
    o_ref = out_refs[0]
    j = pl.program_id(1)

    def swiglu(g, u):
        return (0.5 * (g * jax.nn.sigmoid(g)) * u).astype(BF16)

    @pl.when(j == 0)
    def _():
        tail = wdt_ref.shape[0]
        w_gu, w_d = wgut_ref[...].astype(BF16), wdt_ref[...].astype(BF16)
        half = h_ref.shape[0] // 2
        gus = [_dot(h_ref[r0:r0 + half, :], w_gu) for r0 in (0, half)]
        for r0, gu in zip((0, half), gus):
            o_ref[r0:r0 + half, :] = _dot(swiglu(gu[:, :tail], gu[:, tail:]), w_d)

    @pl.when((j > 0) & (j < n_full))
    def _():
        h = h_ref[...]
        act = swiglu(_dot(h, wg_ref[...].astype(BF16)), _dot(h, wu_ref[...].astype(BF16)))
        o_ref[...] += _dot(act, wd_ref[...].astype(BF16))

    @pl.when(j < n_chunk)
    def _():
        r0 = pl.multiple_of(j * rows, rows)
        o_ref[pl.ds(r0, rows), :] += x_ref[...]

    @pl.when(j == n_full)
    def _():
        w_g, w_u, w_d = wg_ref[...].astype(BF16), wu_ref[...].astype(BF16), wd_ref[...].astype(BF16)
        half = h_ref.shape[0] // 2
        acts = [swiglu(_dot(h_ref[r0:r0 + half, :], w_g), _dot(h_ref[r0:r0 + half, :], w_u))
                for r0 in (0, half)]
        for r0, act in zip((0, half), acts):
            xn = o_ref[r0:r0 + half, :] + _dot(act, w_d)
            if final:
                o_ref[r0:r0 + half, :] = _rms(xn, gn_ref[...])
            else:
                o_ref[r0:r0 + half, :] = xn
                out_refs[1][r0:r0 + half, :] = _rms(xn, gn_ref[...]).astype(BF16)


def _ffn_call(h, x, w_gate, w_up, w_down, layer, gains, gain_layer, final):
    T, D = x.shape
    d_ff = w_gate.shape[-1]
    tm, tf, tail = 1024, 256, 128
    n_full = d_ff // tf
    assert n_full * tf + tail == d_ff
    tail_blk = (n_full * tf) // tail
    n_chunk = 8
    rows = tm // n_chunk
    kern = functools.partial(_ffn_kernel, n_full=n_full, n_chunk=n_chunk, rows=rows, final=final)
    once = pl.Buffered(1)
    full = lambda j: jnp.maximum(j - 1, 0)
    w_gu_tail = jnp.concatenate([w_gate[layer, :, n_full * tf:], w_up[layer, :, n_full * tf:]], axis=-1)
    in_specs = [
        pl.BlockSpec((tm, D), lambda i, j: (i, 0)),
        pl.BlockSpec((rows, D), lambda i, j: (i * n_chunk + jnp.minimum(j, n_chunk - 1), 0)),
        pl.BlockSpec((None, D, tf), lambda i, j: (layer, 0, full(j))),
        pl.BlockSpec((None, D, tf), lambda i, j: (layer, 0, full(j))),
        pl.BlockSpec((None, tf, D), lambda i, j: (layer, full(j), 0)),
        pl.BlockSpec((D, 2 * tail), lambda i, j: (0, 0), pipeline_mode=once),
        pl.BlockSpec((None, tail, D), lambda i, j: (layer, tail_blk, 0), pipeline_mode=once),
        pl.BlockSpec((None, 1, D), lambda i, j: (gain_layer, 0, 0)),
    ]
    row_spec = pl.BlockSpec((tm, D), lambda i, j: (i, 0))
    if final:
        out_specs = row_spec
        out_shape = jax.ShapeDtypeStruct((T, D), F32)
    else:
        out_specs = [row_spec, row_spec]
        out_shape = [jax.ShapeDtypeStruct((T, D), F32), jax.ShapeDtypeStruct((T, D), BF16)]
    return pl.pallas_call(
        kern,
        grid=(T // tm, n_full + 1),
        in_specs=in_specs,
        out_specs=out_specs,
        out_shape=out_shape,
        compiler_params=_params(2),
        name="swiglu_half_step",
    )(h, x, w_gate, w_up, w_down, w_gu_tail, w_down, gains)


def _inproj_kernel(scale_ref, h_ref, w_ref, o_ref, *, n_sub):
    j = pl.program_id(1)
    r = _dot(h_ref[...], w_ref[...].astype(BF16))
    for t in range(n_sub):
        o_ref[t] = (r[:, t * LANES:(t + 1) * LANES] * scale_ref[j * n_sub + t]).astype(BF16)


def _inproj_call(h, w_in, layer):
    T, D = h.shape
    tm, tn = 1024, 8 * LANES
    n_sub = tn // LANES
    n_blk = w_in.shape[-1] // tn
    assert n_blk * n_sub == N_SLABS
    scale = np.ones((N_SLABS,), np.float32)
    scale[SLAB_AQ:SLAB_AK] = ATTN_Q_SCALE
    scale[SLAB_BQ:SLAB_BK] = DIFF_Q_SCALE
    scale[SLAB_CQ:SLAB_CK] = ATTN_Q_SCALE
    return pl.pallas_call(
        functools.partial(_inproj_kernel, n_sub=n_sub),
        grid=(T // tm, n_blk),
        in_specs=[pl.BlockSpec(memory_space=pltpu.SMEM),
                  pl.BlockSpec((tm, D), lambda i, j: (i, 0)),
                  pl.BlockSpec((None, D, tn), lambda i, j: (layer, 0, j))],
        out_specs=pl.BlockSpec((n_sub, tm, LANES), lambda i, j: (j, i, 0)),
        out_shape=jax.ShapeDtypeStruct((N_SLABS, T, LANES), BF16),
        compiler_params=_params(2),
        name="input_projection",
    )(jnp.asarray(scale), h, w_in)


DIL_CLASSES = 4
DIL_OFFSETS = 3
DIL_SAFE_BOUND = 100.0


def _dil_kernel(slopes_ref, q_ref, k_ref, v_ref, y_ref,
                stage, k4, v4, qf, q4, a1, m1, s1, a4, m4, s4, kn, bias_a, bias_b, *, tq, seq, blk):
    h = pl.program_id(1)
    i = pl.program_id(2)
    slope2 = slopes_ref[h] * LOG2E
    (_, d1), (_, d4), (_, d16) = DIL_BRANCHES
    per4, per16 = tq // d4, tq // d16
    n1, n4, n16 = blk, min(blk, per4), min(blk, per16)
    win1 = min(n1 + 2 * DIL_RADIUS, seq // d1)
    win4 = min(n4 + 2 * DIL_RADIUS, seq // d4)
    win16 = min(n16 + 2 * DIL_RADIUS, seq // d16)

    def classes(ref, n):
        return [ref[pl.ds(r, n, stride=DIL_CLASSES), :] for r in range(DIL_CLASSES)]

    @pl.when(i == 0)
    def _():
        for src, dst in ((k_ref, k4), (v_ref, v4)):
            stage[...] = src[...].astype(F32)
            if src is k_ref:
                k_sq = jnp.sum(stage[...] * stage[...], axis=-1, keepdims=True)
                kn[...] = jnp.broadcast_to(jnp.sqrt(jnp.max(k_sq, axis=0, keepdims=True)), kn.shape)
            for r, rows in enumerate(classes(stage, seq // DIL_CLASSES)):
                dst[r] = rows
        for idx, (dil, n_q, win, ref) in enumerate(((d1, n1, win1, bias_a), (d4, n4, win4, bias_a),
                                                    (d16, n16, win16, bias_b))):
            row = lax.broadcasted_iota(jnp.int32, (n_q, win), 0)
            col = lax.broadcasted_iota(jnp.int32, (n_q, win), 1)
            for case in range(DIL_OFFSETS):
                absrel = jnp.abs(col - row - case * DIL_RADIUS)
                tile = jnp.where(absrel <= DIL_RADIUS, (-slope2 * dil) * absrel.astype(F32), NEG)
                if ref is bias_a:
                    ref[idx, case] = tile
                else:
                    ref[case] = tile

    qf[...] = q_ref[...].astype(F32)
    for r, rows in enumerate(classes(qf, tq // DIL_CLASSES)):
        q4[r] = rows

    def window(u0, sub_len, win):
        sub_start = jnp.clip(u0 - DIL_RADIUS, 0, sub_len - win)
        return sub_start, lax.div(u0 - sub_start, DIL_RADIUS)

    q_sq = qf[...] * qf[...]
    q_norm_sq = jnp.sum(jnp.max(q_sq, axis=0, keepdims=True), axis=-1, keepdims=True)
    safe = jnp.max(jnp.sqrt(q_norm_sq) * kn[0:1, 0:1]) <= DIL_SAFE_BOUND

    def run(bounded):
        def softmax_block(s, v):
            wide = (s.shape[0], LANES)
            if bounded:
                m, p = None, jnp.exp2(s)
            else:
                m = jnp.max(s, axis=-1, keepdims=True)
                p = jnp.exp2(s - m)
                m = jnp.broadcast_to(m, wide)
            den = jnp.sum(p, axis=-1, keepdims=True)
            return _dot(p.astype(BF16), v), m, jnp.broadcast_to(den, wide)

        def load(refs, idx):
            return tuple(None if (bounded and (ref is m1 or ref is m4)) else ref[idx] for ref in refs)

        def store(refs, idx, vals):
            for ref, val in zip(refs, vals):
                if val is not None:
                    ref[idx] = val

        def merge(block, prev):
            (acc, m, den), (acc_p, m_p, den_p) = block, prev
            if bounded:
                return acc_p + acc, None, den_p + den
            m_new = jnp.maximum(m_p, m)
            w_p, w = jnp.exp2(m_p - m_new), jnp.exp2(m - m_new)
            return w_p * acc_p + w * acc, m_new, w_p * den_p + w * den

        blocks = []

        for b in range(tq // n1):
            def logits(b=b):
                start, case = window(i * tq + b * n1, seq // d1, win1)
                src = pl.ds(pl.multiple_of(start, DIL_RADIUS), win1)
                return _dot_nt(q_ref[b * n1:(b + 1) * n1, :], k_ref[src, :]) + bias_a[0, case], src

            def finish(s, src, b=b):
                dst = (slice(b * n1, (b + 1) * n1), slice(None))
                store((a1, m1, s1), dst, softmax_block(s, v_ref[src, :]))
                if b == tq // n1 - 1:
                    for r in range(DIL_CLASSES):
                        rows = (pl.ds(r, per4, stride=DIL_CLASSES), slice(None))
                        store((a4, m4, s4), r, load((a1, m1, s1), rows))
            blocks.append((logits, finish))

        for r in range(d4):
            for b in range(per4 // n4):
                def logits(r=r, b=b):
                    start, case = window(i * per4 + b * n4, seq // d4, win4)
                    src = pl.ds(pl.multiple_of(start, DIL_RADIUS), win4)
                    q = q4[r, b * n4:(b + 1) * n4, :].astype(BF16)
                    return _dot_nt(q, k4[r, src, :].astype(BF16)) + bias_a[1, case], src

                def finish(s, src, r=r, b=b):
                    dst = (r, slice(b * n4, (b + 1) * n4), slice(None))
                    store((a4, m4, s4), dst, merge(softmax_block(s, v4[r, src, :].astype(BF16)),
                                                   load((a4, m4, s4), dst)))
                blocks.append((logits, finish))

        for r16 in range(d16):
            r, a = r16 % DIL_CLASSES, r16 // DIL_CLASSES
            for b in range(per16 // n16):
                def logits(r=r, a=a, b=b):
                    start, case = window(i * per16 + b * n16, seq // d16, win16)
                    src = pl.ds(a + DIL_CLASSES * start, win16, stride=DIL_CLASSES)
                    q = q4[r, pl.ds(a + DIL_CLASSES * b * n16, n16, stride=DIL_CLASSES), :].astype(BF16)
                    return _dot_nt(q, k4[r, src, :].astype(BF16)) + bias_b[case], src

                def finish(s, src, r=r, a=a, b=b):
                    dst = (r, pl.ds(a + DIL_CLASSES * b * n16, n16, stride=DIL_CLASSES), slice(None))
                    store((a4, m4, s4), dst, merge(softmax_block(s, v4[r, src, :].astype(BF16)),
                                                   load((a4, m4, s4), dst)))
                blocks.append((logits, finish))

        ahead = 6
        queue = [blocks[n][0]() for n in range(ahead)]
        for n in range(len(blocks)):
            if n + ahead < len(blocks):
                queue.append(blocks[n + ahead][0]())
            blocks[n][1](*queue.pop(0))

        for r in range(DIL_CLASSES):
            a1[pl.ds(r, per4, stride=DIL_CLASSES), :] = a4[r] / s4[r]
        y_ref[...] = a1[...].astype(BF16)

    pl.when(safe)(lambda: run(True))
    pl.when(jnp.logical_not(safe))(lambda: run(False))


def _dilated_attention(qkv, slopes_a, batch, seq):
    T = batch * seq
    tq, blk = 2048, 128
    assert tuple(d for _, d in DIL_BRANCHES) == (1, DIL_CLASSES, DIL_CLASSES ** 2)
    assert tq % (8 * DIL_CLASSES ** 2) == 0 and seq % tq == 0
    per16 = tq // DIL_CLASSES ** 2
    qkv_v = qkv.reshape(N_SLABS, batch, seq, LANES)
    kern = functools.partial(_dil_kernel, tq=tq, seq=seq, blk=blk)
    cls = lambda n: pltpu.VMEM((DIL_CLASSES, n // DIL_CLASSES, LANES), F32)
    tile = lambda n: pltpu.VMEM((n, LANES), F32)
    win = blk + 2 * DIL_RADIUS
    out = pl.pallas_call(
        kern,
        grid=(batch, DIL_HEADS, seq // tq),
        in_specs=[
            pl.BlockSpec(memory_space=pltpu.SMEM),
            pl.BlockSpec((None, None, tq, LANES), lambda b, h, i: (SLAB_AQ + h, b, i, 0)),
            pl.BlockSpec((None, None, seq, LANES), lambda b, h, i: (SLAB_AK + h, b, 0, 0)),
            pl.BlockSpec((None, None, seq, LANES), lambda b, h, i: (SLAB_AV + h, b, 0, 0)),
        ],
        out_specs=pl.BlockSpec((None, None, tq, LANES), lambda b, h, i: (h, b, i, 0)),
        out_shape=jax.ShapeDtypeStruct((DIL_HEADS, batch, seq, LANES), BF16),
        scratch_shapes=[tile(seq), cls(seq), cls(seq), tile(tq), cls(tq),
                        tile(tq), tile(tq), tile(tq), cls(tq), cls(tq), cls(tq),
                        tile(8),
                        pltpu.VMEM((2, DIL_OFFSETS, blk, win), F32),
                        pltpu.VMEM((DIL_OFFSETS, per16, min(per16 + 2 * DIL_RADIUS, seq // DIL_CLASSES ** 2)), F32)],
        compiler_params=_params(3),
        name="dilated_attention",
    )(slopes_a, qkv_v, qkv_v, qkv_v)
    return out.reshape(DIL_HEADS, T, LANES)


DIFF_FEAT = 3
DIFF_VT_ROWS = HEAD_DIM + 16
DIFF_SAFE_BOUND = 100.0


def _diff_kernel(slopes_ref, q_ref, k_ref, v_ref, lq1_ref, lk1_ref, lq2_ref, lk2_ref, sg_ref,
                 y_ref, ka_ref, vt_ref, qt_ref, kn_ref, *, tq, tk, seq, n_sub, lambda_init):
    h = pl.program_id(1)
    i = pl.program_id(2)
    slope2 = slopes_ref[h] * LOG2E
    n_kb = seq // tk
    half = DIFF_QK_DIM

    @pl.when(i == 0)
    def _():
        lane = lax.broadcasted_iota(jnp.int32, (tk, LANES), 1)
        k_local = lax.broadcasted_iota(jnp.int32, (tk, LANES), 0)
        rem = slope2 * k_local.astype(F32)
        parts = []
        for _ in range(DIFF_FEAT):
            part = rem.astype(BF16).astype(F32)
            parts.append(part)
            rem = rem - part
        ones_row = (lax.broadcasted_iota(jnp.int32, (DIFF_VT_ROWS - HEAD_DIM, tk), 0) == 0)
        k_sq = [jnp.zeros((1, 1), F32)] * 2
        for c in range(n_kb):
            kf = k_ref[c * tk:(c + 1) * tk, :].astype(F32)
            for mp in range(2):
                lo, fo = mp * half, half - mp * half
                ka = jnp.where((lane >= lo) & (lane < lo + half), kf, 0.0)
                k_sq[mp] = jnp.maximum(k_sq[mp], jnp.max(jnp.sum(ka * ka, axis=-1, keepdims=True),
                                                         axis=0, keepdims=True))
                for t, part in enumerate(parts):
                    ka = jnp.where(lane == fo + t, part, ka)
                ka_ref[mp, c * tk:(c + 1) * tk, :] = ka.astype(BF16)
            vt_ref[c, :HEAD_DIM, :] = v_ref[c * tk:(c + 1) * tk, :].astype(F32).T.astype(BF16)
            vt_ref[c, HEAD_DIM:, :] = ones_row.astype(BF16)
        for mp in range(2):
            kn_ref[mp] = jnp.broadcast_to(jnp.sqrt(k_sq[mp]), kn_ref.shape[1:])

    def prepare(t):
        qt = q_ref[t * tq:(t + 1) * tq, :].astype(F32).T
        row = lax.broadcasted_iota(jnp.int32, (LANES, tq), 0)
        bound = None
        for mp in range(2):
            lo, fo = mp * half, half - mp * half
            is_own = (row >= lo) & (row < lo + half)
            own = jnp.where(is_own, qt, 0.0)
            feat = (row >= fo) & (row < fo + DIFF_FEAT)
            qt_ref[t, 0, mp] = jnp.where(feat, 1.0, own).astype(BF16)
            qt_ref[t, 1, mp] = jnp.where(feat, -1.0, own).astype(BF16)
            qt_ref[t, 2, mp] = own.astype(BF16)
            q_norm = jnp.sqrt(jnp.sum(own * own, axis=0, keepdims=True))
            worst = jnp.max(q_norm * kn_ref[mp])
            bound = worst if bound is None else jnp.maximum(bound, worst)
        return (i * n_sub + t) * tq, bound

    q_local = lax.broadcasted_iota(jnp.int32, (1, tq), 1).astype(F32)

    def logits(t, q0, j, complete):
        kb_diag = q0 // tk
        kb = kb_diag + j
        kb = jnp.where(kb >= n_kb, kb - n_kb, kb)
        k0 = pl.multiple_of(kb * tk, tk)
        if j < max(1, tq // tk):
            krow = lax.broadcasted_iota(jnp.int32, (tk, tq), 0)
            qcol = lax.broadcasted_iota(jnp.int32, (tk, tq), 1)
            tile = slope2 * jnp.abs((qcol - krow) + (q0 - k0)).astype(F32)
            return kb, 0.0, [_dot(ka_ref[mp, pl.ds(k0, tk), :], qt_ref[t, 2, mp]) - tile for mp in range(2)]
        right = (kb > kb_diag).astype(jnp.int32)
        sgn = (1 - 2 * right).astype(F32)
        row_term = (-sgn * slope2) * ((q0 - k0).astype(F32) + q_local)
        s_maps = [_dot(ka_ref[mp, pl.ds(k0, tk), :], qt_ref[t, right, mp]) for mp in range(2)]
        if complete:
            return kb, 0.0, [s + row_term for s in s_maps]
        return kb, row_term, s_maps

    def stream(t, q0, consume, complete):
        ahead = 3
        queue = [logits(t, q0, j, complete) for j in range(min(ahead, n_kb))]
        for j in range(n_kb):
            cur = queue.pop(0)
            if j + ahead < n_kb:
                queue.append(logits(t, q0, j + ahead, complete))
            consume(*cur)

    def finish(t, acc0, acc1):
        lam = (jnp.exp(jnp.sum(lq1_ref[...] * lk1_ref[...], axis=-1, keepdims=True))
               - jnp.exp(jnp.sum(lq2_ref[...] * lk2_ref[...], axis=-1, keepdims=True))
               + lambda_init)
        o0 = acc0[:HEAD_DIM] / acc0[HEAD_DIM:HEAD_DIM + 1]
        o1 = acc1[:HEAD_DIM] / acc1[HEAD_DIM:HEAD_DIM + 1]
        o = (o0 - lam * o1).T
        y_ref[t * tq:(t + 1) * tq, :] = (_rms(o, sg_ref[...]) * (1.0 - lambda_init)).astype(BF16)

    tiles = [prepare(t) for t in range(n_sub)]
    bound = functools.reduce(jnp.maximum, [b for _, b in tiles])
    safe = bound <= DIFF_SAFE_BOUND

    @pl.when(safe)
    def _():
        for t, (q0, _) in enumerate(tiles):
            accs = [jnp.zeros((DIFF_VT_ROWS, tq), F32) for _ in range(2)]

            def consume(kb, row_term, s_maps, accs=accs):
                vt = vt_ref[kb]
                for mp, s in enumerate(s_maps):
                    accs[mp] = accs[mp] + _dot(vt, jnp.exp2(s).astype(BF16))

            stream(t, q0, consume, True)
            finish(t, *accs)

    @pl.when(jnp.logical_not(safe))
    def _():
        for t, (q0, _) in enumerate(tiles):
            state = []

            def consume(kb, row_term, s_maps, state=state):
                vt = vt_ref[kb]
                part = []
                for s in s_maps:
                    m = jnp.max(s, axis=0, keepdims=True)
                    part.append((m + row_term, _dot(vt, jnp.exp2(s - m).astype(BF16))))
                if not state:
                    state.extend(part)
                    return
                for mp, ((m, acc), (m_b, acc_b)) in enumerate(zip(list(state), part)):
                    m_new = jnp.maximum(m, m_b)
                    state[mp] = (m_new, jnp.exp2(m - m_new) * acc + jnp.exp2(m_b - m_new) * acc_b)

            stream(t, q0, consume, False)
            finish(t, state[0][1], state[1][1])


def _diff_call(qkv, slopes_b, lq1, lk1, lq2, lk2, subln, layer, lambda_init, batch, seq):
    T = batch * seq
    tq, tk = 512, 256
    assert (tk % tq == 0 or tq % tk == 0) and seq % tk == 0
    qkv_v = qkv.reshape(N_SLABS, batch, seq, LANES)
    vec = lambda a: a.reshape(a.shape[0], 1, a.shape[1])
    lam_spec = pl.BlockSpec((None, 1, DIFF_QK_DIM), lambda b, h, i: (layer, 0, 0))
    n_sub = 2
    assert seq % (tq * n_sub) == 0
    kern = functools.partial(_diff_kernel, tq=tq, tk=tk, seq=seq, n_sub=n_sub, lambda_init=lambda_init)
    out = pl.pallas_call(
        kern,
        grid=(batch, DIFF_HEADS, seq // (tq * n_sub)),
        in_specs=[
            pl.BlockSpec(memory_space=pltpu.SMEM),
            pl.BlockSpec((None, None, tq * n_sub, LANES), lambda b, h, i: (SLAB_BQ + h, b, i, 0)),
            pl.BlockSpec((None, None, seq, LANES), lambda b, h, i: (SLAB_BK + h, b, 0, 0)),
            pl.BlockSpec((None, None, seq, LANES), lambda b, h, i: (SLAB_BV + h, b, 0, 0)),
            lam_spec, lam_spec, lam_spec, lam_spec,
            pl.BlockSpec((None, 1, HEAD_DIM), lambda b, h, i: (layer, 0, 0)),
        ],
        out_specs=pl.BlockSpec((None, None, tq * n_sub, LANES), lambda b, h, i: (h, b, i, 0)),
        out_shape=jax.ShapeDtypeStruct((DIFF_HEADS, batch, seq, LANES), BF16),
        scratch_shapes=[pltpu.VMEM((2, seq, LANES), BF16), pltpu.VMEM((seq // tk, DIFF_VT_ROWS, tk), BF16),
                        pltpu.VMEM((n_sub, 3, 2, LANES, tq), BF16), pltpu.VMEM((2, 1, tq), F32)],
        compiler_params=_params(3),
        name="differential_attention",
    )(slopes_b, qkv_v, qkv_v, qkv_v, vec(lq1), vec(lk1), vec(lq2), vec(lk2), vec(subln))
    return out.reshape(DIFF_HEADS, T, LANES)


def _swa_kernel(slopes_ref, sink_ref, q_ref, k_ref, v_ref, y_ref, bias_ref, *, tq, win, seq, layer,
                tiles_per_iter):
    g = pl.program_id(1)
    heads = [g * SWA_GROUP + r for r in range(SWA_GROUP)]
    sinks2 = [sink_ref[layer * SWA_Q_HEADS + hd] * LOG2E for hd in heads]

    row = lax.broadcasted_iota(jnp.int32, (tq, win), 0)
    col = lax.broadcasted_iota(jnp.int32, (tq, win), 1)
    for case in range(SWA_OFFSETS):
        absrel = jnp.abs(col - row - case * SWA_RADIUS)
        for r, hd in enumerate(heads):
            slope2 = slopes_ref[hd] * LOG2E
            bias_ref[r, case] = jnp.where(absrel <= SWA_RADIUS, -slope2 * absrel.astype(F32), NEG)

    def tiles(it, carry):
        work = []
        for t in range(tiles_per_iter):
            q0 = pl.multiple_of((it * tiles_per_iter + t) * tq, tq)
            start = pl.multiple_of(jnp.clip(q0 - SWA_RADIUS, 0, seq - win), SWA_RADIUS)
            case = lax.div(q0 - start, SWA_RADIUS)
            k = k_ref[pl.ds(start, win), :]
            for r in range(SWA_GROUP):
                work.append((r, q0, start, _dot_nt(q_ref[r, pl.ds(q0, tq), :], k) + bias_ref[r, case]))
        for r, q0, start, s in work:
            m = jnp.maximum(jnp.max(s, axis=-1, keepdims=True), sinks2[r])
            p = jnp.exp2(s - m)
            den = jnp.sum(p, axis=-1, keepdims=True) + jnp.exp2(sinks2[r] - m)
            o = _dot(p.astype(BF16), v_ref[pl.ds(start, win), :]) / den
            y_ref[r, pl.ds(q0, tq), :] = o.astype(BF16)
        return carry

    lax.fori_loop(0, seq // (tq * tiles_per_iter), tiles, 0)


def _swa_call(qkv, slopes_c, sink_flat, layer, batch, seq):
    T = batch * seq
    tq = 256
    win = tq + 2 * SWA_RADIUS
    qkv_v = qkv.reshape(N_SLABS, batch, seq, LANES)
    tiles_per_iter = 2
    assert seq % (tq * tiles_per_iter) == 0
    kern = functools.partial(_swa_kernel, tq=tq, win=win, seq=seq, layer=layer, tiles_per_iter=tiles_per_iter)
    out = pl.pallas_call(
        kern,
        grid=(batch, SWA_KV_HEADS),
        in_specs=[
            pl.BlockSpec(memory_space=pltpu.SMEM),
            pl.BlockSpec(memory_space=pltpu.SMEM),
            pl.BlockSpec((SWA_GROUP, None, seq, LANES), lambda b, g: (SLAB_CQ // SWA_GROUP + g, b, 0, 0)),
            pl.BlockSpec((None, None, seq, LANES), lambda b, g: (SLAB_CK + g, b, 0, 0)),
            pl.BlockSpec((None, None, seq, LANES), lambda b, g: (SLAB_CV + g, b, 0, 0)),
        ],
        out_specs=pl.BlockSpec((SWA_GROUP, None, seq, LANES), lambda b, g: (g, b, 0, 0)),
        out_shape=jax.ShapeDtypeStruct((SWA_Q_HEADS, batch, seq, LANES), BF16),
        scratch_shapes=[pltpu.VMEM((SWA_GROUP, SWA_OFFSETS, tq, win), F32)],
        compiler_params=_params(2),
        name="window_gqa_attention",
    )(slopes_c, sink_flat, qkv_v, qkv_v, qkv_v)
    return out.reshape(SWA_Q_HEADS, T, LANES)


def _outproj_kernel(ya_ref, yb_ref, yc_ref, x_ref, w_ref, gn_ref, o_ref, hn_ref, wb_ref, *, cast_rows):
    @pl.when(pl.program_id(0) == 0)
    def _():
        for r0 in range(0, wb_ref.shape[0], cast_rows):
            wb_ref[r0:r0 + cast_rows, :] = w_ref[r0:r0 + cast_rows, :].astype(BF16)

    slabs = [y_ref[s] for y_ref in (ya_ref, yb_ref, yc_ref) for s in range(y_ref.shape[0])]
    y = jnp.concatenate(slabs, axis=-1)
    xn = x_ref[...] + _dot(y, wb_ref[...])
    o_ref[...] = xn
    hn_ref[...] = _rms(xn, gn_ref[...]).astype(BF16)


def _outproj_call(ya, yb, yc, x, w_out, layer, gains, gain_layer):
    T, D = x.shape
    tm = 512
    row_spec = pl.BlockSpec((tm, D), lambda i: (i, 0))
    slab_spec = lambda y: pl.BlockSpec((y.shape[0], tm, LANES), lambda i: (0, i, 0))
    return pl.pallas_call(
        functools.partial(_outproj_kernel, cast_rows=256),
        grid=(T // tm,),
        in_specs=[
            slab_spec(ya), slab_spec(yb), slab_spec(yc),
            row_spec,
            pl.BlockSpec((None,) + w_out.shape[1:], lambda i: (layer, 0, 0), pipeline_mode=pl.Buffered(1)),
            pl.BlockSpec((None, 1, D), lambda i: (gain_layer, 0, 0)),
        ],
        out_specs=[row_spec, row_spec],
        out_shape=[jax.ShapeDtypeStruct((T, D), F32), jax.ShapeDtypeStruct((T, D), BF16)],
        scratch_shapes=[pltpu.VMEM(w_out.shape[1:], BF16)],
        compiler_params=_params(1),
        name="output_projection",
    )(ya, yb, yc, x, w_out, gains)


def kernel(x, ffn1_norm, ffn1_w_gate, ffn1_w_up, ffn1_w_down, mix_norm, w_in, w_out, diff_lambda_q1, diff_lambda_k1, diff_lambda_q2, diff_lambda_k2, diff_subln, swa_sink, ffn2_norm, ffn2_w_gate, ffn2_w_up, ffn2_w_down, final_norm):
    batch, seq, d_model = x.shape
    depth = w_in.shape[0]
    xf = x.reshape(batch * seq, d_model)
    gain3 = lambda a: a.reshape(a.shape[0], 1, a.shape[1])
    ffn1_g, mix_g, ffn2_g = gain3(ffn1_norm), gain3(mix_norm), gain3(ffn2_norm)
    final_g = final_norm.reshape(1, 1, d_model)
    slopes_a = jnp.asarray(SLOPES_A, F32)
    slopes_b = jnp.asarray(SLOPES_B, F32)
    slopes_c = jnp.asarray(SLOPES_C, F32)
    sink_flat = swa_sink.astype(F32).reshape(-1)

    h = _norm_call(xf, ffn1_g, 0)
    for l in range(depth):
        lambda_init = 0.8 - 0.6 * math.exp(-0.3 * l)
        xf, h = _ffn_call(h, xf, ffn1_w_gate, ffn1_w_up, ffn1_w_down, l, mix_g, l, final=False)
        qkv = _inproj_call(h, w_in, l)
        ya = _dilated_attention(qkv, slopes_a, batch, seq)
        yb = _diff_call(qkv, slopes_b, diff_lambda_q1, diff_lambda_k1, diff_lambda_q2, diff_lambda_k2,
                        diff_subln, l, lambda_init, batch, seq)
        yc = _swa_call(qkv, slopes_c, sink_flat, l, batch, seq)
        xf, h = _outproj_call(ya, yb, yc, xf, w_out, l, ffn2_g, l)
        if l + 1 < depth:
            xf, h = _ffn_call(h, xf, ffn2_w_gate, ffn2_w_up, ffn2_w_down, l, ffn1_g, l + 1, final=False)
        else:
            xf = _ffn_call(h, xf, ffn2_w_gate, ffn2_w_up, ffn2_w_down, l, final_g, 0, final=True)
    return xf.reshape(batch, seq, d_model)
```

```python
import functools
import math

import numpy as np
import jax
import jax.numpy as jnp
from jax import lax
from jax.experimental import pallas as pl
from jax.experimental.pallas import tpu as pltpu

F32 = jnp.float32
BF16 = jnp.bfloat16

HEAD_DIM = 128
DIL_HEADS = 6
DIL_BRANCHES = ((128, 1), (512, 4), (2048, 16))
DIL_RADIUS = 64
DIFF_HEADS = 4
DIFF_QK_DIM = HEAD_DIM // 2
SWA_Q_HEADS = 6
SWA_KV_HEADS = 2
SWA_GROUP = SWA_Q_HEADS // SWA_KV_HEADS
SWA_RADIUS = 128
SWA_OFFSETS = 3
N_ALIBI_HEADS = SWA_Q_HEADS + DIL_HEADS + DIFF_HEADS
RMS_EPS = 1e-6
NEG = -1e30
ATTN_SCALE = HEAD_DIM ** -0.5
LOG2E = math.log2(math.e)
ATTN_Q_SCALE = ATTN_SCALE * LOG2E
DIFF_Q_SCALE = DIFF_QK_DIM ** -0.5 * LOG2E

SLAB_AQ, SLAB_AK, SLAB_AV = 0, 6, 12
SLAB_BQ, SLAB_BK, SLAB_BV = 18, 22, 26
SLAB_CQ, SLAB_CK, SLAB_CV = 30, 36, 38
N_SLABS = 40

LANES = 128
VMEM_LIMIT_BYTES = 60000 * 1024

_SLOPES = [float(np.float32(2.0 ** (-8.0 * (n + 1) / N_ALIBI_HEADS))) for n in range(N_ALIBI_HEADS)]
SLOPES_C = _SLOPES[:SWA_Q_HEADS]
SLOPES_A = _SLOPES[SWA_Q_HEADS:SWA_Q_HEADS + DIL_HEADS]
SLOPES_B = _SLOPES[SWA_Q_HEADS + DIL_HEADS:]


def _params(n_grid):
    return pltpu.CompilerParams(dimension_semantics=("arbitrary",) * n_grid,
                                vmem_limit_bytes=VMEM_LIMIT_BYTES)


def _rms(x, g):
    ms = jnp.mean(x * x, axis=-1, keepdims=True)
    return x * lax.rsqrt(ms + RMS_EPS) * g


def _dot(a, b):
    return jnp.dot(a, b, preferred_element_type=F32)


def _dot_nt(a, b):
    return lax.dot_general(a, b, (((1,), (1,)), ((), ())), preferred_element_type=F32)


def _norm_kernel(x_ref, g_ref, o_ref):
    o_ref[...] = _rms(x_ref[...], g_ref[...]).astype(BF16)


def _norm_call(x, gains, layer):
    T, D = x.shape
    tm = 512
    return pl.pallas_call(
        _norm_kernel,
        grid=(T // tm,),
        in_specs=[pl.BlockSpec((tm, D), lambda i: (i, 0)),
                  pl.BlockSpec((None, 1, D), lambda i: (layer, 0, 0))],
        out_specs=pl.BlockSpec((tm, D), lambda i: (i, 0)),
        out_shape=jax.ShapeDtypeStruct((T, D), BF16),
        compiler_params=_params(1),
        name="rms_norm_bf16",
    )(x, gains)


def _ffn_kernel(h_ref, x_ref, wg_ref, wu_ref, wd_ref, wgut_ref, wdt_ref, gn_ref,
                *out_refs, n_full, n_chunk, rows, final):
    o_ref = out_refs[0]
    j = pl.program_id(1)

    def swiglu(g, u):
        return (0.5 * (g * jax.nn.sigmoid(g)) * u).astype(BF16)

    @pl.when(j == 0)
    def _():
        tail = wdt_ref.shape[0]
        w_gu, w_d = wgut_ref[...].astype(BF16), wdt_ref[...].astype(BF16)
        half = h_ref.shape[0] // 2
        gus = [_dot(h_ref[r0:r0 + half, :], w_gu) for r0 in (0, half)]
        for r0, gu in zip((0, half), gus):
            o_ref[r0:r0 + half, :] = _dot(swiglu(gu[:, :tail], gu[:, tail:]), w_d)

    @pl.when((j > 0) & (j < n_full))
    def _():
        h = h_ref[...]
        act = swiglu(_dot(h, wg_ref[...].astype(BF16)), _dot(h, wu_ref[...].astype(BF16)))
        o_ref[...] += _dot(act, wd_ref[...].astype(BF16))

    @pl.when(j < n_chunk)
    def _():
        r0 = pl.multiple_of(j * rows, rows)
        o_ref[pl.ds(r0, rows), :] += x_ref[...]

    @pl.when(j == n_full)
    def _():
        w_g, w_u, w_d = wg_ref[...].astype(BF16), wu_ref[...].astype(BF16), wd_ref[...].astype(BF16)
        half = h_ref.shape[0] // 2
        acts = [swiglu(_dot(h_ref[r0:r0 + half, :], w_g), _dot(h_ref[r0:r0 + half, :], w_u))
                for r0 in (0, half)]
        for r0, act in zip((0, half), acts):
            xn = o_ref[r0:r0 + half, :] + _dot(act, w_d)
            if final:
                o_ref[r0:r0 + half, :] = _rms(xn, gn_ref[...])
            else:
                o_ref[r0:r0 + half, :] = xn
                out_refs[1][r0:r0 + half, :] = _rms(xn, gn_ref[...]).astype(BF16)


def _ffn_call(h, x, w_gate, w_up, w_down, layer, gains, gain_layer, final):
    T, D = x.shape
    d_ff = w_gate.shape[-1]
    tm, tf, tail = 1024, 256, 128
    n_full = d_ff // tf
    assert n_full * tf + tail == d_ff
    tail_blk = (n_full * tf) // tail
    n_chunk = 8
    rows = tm // n_chunk
    kern = functools.partial(_ffn_kernel, n_full=n_full, n_chunk=n_chunk, rows=rows, final=final)
    once = pl.Buffered(1)
    full = lambda j: jnp.maximum(j - 1, 0)
    w_gu_tail = jnp.concatenate([w_gate[layer, :, n_full * tf:], w_up[layer, :, n_full * tf:]], axis=-1)
    in_specs = [
        pl.BlockSpec((tm, D), lambda i, j: (i, 0)),
        pl.BlockSpec((rows, D), lambda i, j: (i * n_chunk + jnp.minimum(j, n_chunk - 1), 0)),
        pl.BlockSpec((None, D, tf), lambda i, j: (layer, 0, full(j))),
        pl.BlockSpec((None, D, tf), lambda i, j: (layer, 0, full(j))),
        pl.BlockSpec((None, tf, D), lambda i, j: (layer, full(j), 0)),
        pl.BlockSpec((D, 2 * tail), lambda i, j: (0, 0), pipeline_mode=once),
        pl.BlockSpec((None, tail, D), lambda i, j: (layer, tail_blk, 0), pipeline_mode=once),
        pl.BlockSpec((None, 1, D), lambda i, j: (gain_layer, 0, 0)),
    ]
    row_spec = pl.BlockSpec((tm, D), lambda i, j: (i, 0))
    if final:
        out_specs = row_spec
        out_shape = jax.ShapeDtypeStruct((T, D), F32)
    else:
        out_specs = [row_spec, row_spec]
        out_shape = [jax.ShapeDtypeStruct((T, D), F32), jax.ShapeDtypeStruct((T, D), BF16)]
    return pl.pallas_call(
        kern,
        grid=(T // tm, n_full + 1),
        in_specs=in_specs,
        out_specs=out_specs,
        out_shape=out_shape,
        compiler_params=_params(2),
        name="swiglu_half_step",
    )(h, x, w_gate, w_up, w_down, w_gu_tail, w_down, gains)


def _inproj_kernel(scale_ref, h_ref, w_ref, o_ref, *, n_sub):
    j = pl.program_id(1)
    r = _dot(h_ref[...], w_ref[...].astype(BF16))
    for t in range(n_sub):
        o_ref[t] = (r[:, t * LANES:(t + 1) * LANES] * scale_ref[j * n_sub + t]).astype(BF16)


def _inproj_call(h, w_in, layer):
    T, D = h.shape
    tm, tn = 1024, 8 * LANES
    n_sub = tn // LANES
    n_blk = w_in.shape[-1] // tn
    assert n_blk * n_sub == N_SLABS
    scale = np.ones((N_SLABS,), np.float32)
    scale[SLAB_AQ:SLAB_AK] = ATTN_Q_SCALE
    scale[SLAB_BQ:SLAB_BK] = DIFF_Q_SCALE
    scale[SLAB_CQ:SLAB_CK] = ATTN_Q_SCALE
    return pl.pallas_call(
        functools.partial(_inproj_kernel, n_sub=n_sub),
        grid=(T // tm, n_blk),
        in_specs=[pl.BlockSpec(memory_space=pltpu.SMEM),
                  pl.BlockSpec((tm, D), lambda i, j: (i, 0)),
                  pl.BlockSpec((None, D, tn), lambda i, j: (layer, 0, j))],
        out_specs=pl.BlockSpec((n_sub, tm, LANES), lambda i, j: (j, i, 0)),
        out_shape=jax.ShapeDtypeStruct((N_SLABS, T, LANES), BF16),
        compiler_params=_params(2),
        name="input_projection",
    )(jnp.asarray(scale), h, w_in)


DIL_CLASSES = 4
DIL_OFFSETS = 3
DIL_SAFE_BOUND = 100.0


def _dil_kernel(slopes_ref, q_ref, k_ref, v_ref, y_ref,
                stage, k4, v4, qf, q4, a1, m1, s1, a4, m4, s4, kn, bias_a, bias_b, *, tq, seq, blk):
    h = pl.program_id(1)
    i = pl.program_id(2)
    slope2 = slopes_ref[h] * LOG2E
    (_, d1), (_, d4), (_, d16) = DIL_BRANCHES
    per4, per16 = tq // d4, tq // d16
    n1, n4, n16 = blk, min(blk, per4), min(blk, per16)
    win1 = min(n1 + 2 * DIL_RADIUS, seq // d1)
    win4 = min(n4 + 2 * DIL_RADIUS, seq // d4)
    win16 = min(n16 + 2 * DIL_RADIUS, seq // d16)

    def classes(ref, n):
        return [ref[pl.ds(r, n, stride=DIL_CLASSES), :] for r in range(DIL_CLASSES)]

    @pl.when(i == 0)
    def _():
        for src, dst in ((k_ref, k4), (v_ref, v4)):
            stage[...] = src[...].astype(F32)
            if src is k_ref:
                k_sq = jnp.sum(stage[...] * stage[...], axis=-1, keepdims=True)
                kn[...] = jnp.broadcast_to(jnp.sqrt(jnp.max(k_sq, axis=0, keepdims=True)), kn.shape)
            for r, rows in enumerate(classes(stage, seq // DIL_CLASSES)):
                dst[r] = rows
        for idx, (dil, n_q, win, ref) in enumerate(((d1, n1, win1, bias_a), (d4, n4, win4, bias_a),
                                                    (d16, n16, win16, bias_b))):
            row = lax.broadcasted_iota(jnp.int32, (n_q, win), 0)
            col = lax.broadcasted_iota(jnp.int32, (n_q, win), 1)
            for case in range(DIL_OFFSETS):
                absrel = jnp.abs(col - row - case * DIL_RADIUS)
                tile = jnp.where(absrel <= DIL_RADIUS, (-slope2 * dil) * absrel.astype(F32), NEG)
                if ref is bias_a:
                    ref[idx, case] = tile
                else:
                    ref[case] = tile

    qf[...] = q_ref[...].astype(F32)
    for r, rows in enumerate(classes(qf, tq // DIL_CLASSES)):
        q4[r] = rows

    def window(u0, sub_len, win):
        sub_start = jnp.clip(u0 - DIL_RADIUS, 0, sub_len - win)
        return sub_start, lax.div(u0 - sub_start, DIL_RADIUS)

    q_sq = qf[...] * qf[...]
    q_norm_sq = jnp.sum(jnp.max(q_sq, axis=0, keepdims=True), axis=-1, keepdims=True)
    safe = jnp.max(jnp.sqrt(q_norm_sq) * kn[0:1, 0:1]) <= DIL_SAFE_BOUND

    def run(bounded):
        def softmax_block(s, v):
            wide = (s.shape[0], LANES)
            if bounded:
                m, p = None, jnp.exp2(s)
            else:
                m = jnp.max(s, axis=-1, keepdims=True)
                p = jnp.exp2(s - m)
                m = jnp.broadcast_to(m, wide)
            den = jnp.sum(p, axis=-1, keepdims=True)
            return _dot(p.astype(BF16), v), m, jnp.broadcast_to(den, wide)

        def load(refs, idx):
            return tuple(None if (bounded and (ref is m1 or ref is m4)) else ref[idx] for ref in refs)

        def store(refs, idx, vals):
            for ref, val in zip(refs, vals):
                if val is not None:
                    ref[idx] = val

        def merge(block, prev):
            (acc, m, den), (acc_p, m_p, den_p) = block, prev
            if bounded:
                return acc_p + acc, None, den_p + den
            m_new = jnp.maximum(m_p, m)
            w_p, w = jnp.exp2(m_p - m_new), jnp.exp2(m - m_new)
            return w_p * acc_p + w * acc, m_new, w_p * den_p + w * den

        blocks = []

        for b in range(tq // n1):
            def logits(b=b):
                start, case = window(i * tq + b * n1, seq // d1, win1)
                src = pl.ds(pl.multiple_of(start, DIL_RADIUS), win1)
                return _dot_nt(q_ref[b * n1:(b + 1) * n1, :], k_ref[src, :]) + bias_a[0, case], src

            def finish(s, src, b=b):
                dst = (slice(b * n1, (b + 1) * n1), slice(None))
                store((a1, m1, s1), dst, softmax_block(s, v_ref[src, :]))
                if b == tq // n1 - 1:
                    for r in range(DIL_CLASSES):
                        rows = (pl.ds(r, per4, stride=DIL_CLASSES), slice(None))
                        store((a4, m4, s4), r, load((a1, m1, s1), rows))
            blocks.append((logits, finish))

        for r in range(d4):
            for b in range(per4 // n4):
                def logits(r=r, b=b):
                    start, case = window(i * per4 + b * n4, seq // d4, win4)
                    src = pl.ds(pl.multiple_of(start, DIL_RADIUS), win4)
                    q = q4[r, b * n4:(b + 1) * n4, :].astype(BF16)
                    return _dot_nt(q, k4[r, src, :].astype(BF16)) + bias_a[1, case], src

                def finish(s, src, r=r, b=b):
                    dst = (r, slice(b * n4, (b + 1) * n4), slice(None))
                    store((a4, m4, s4), dst, merge(softmax_block(s, v4[r, src, :].astype(BF16)),
                                                   load((a4, m4, s4), dst)))
                blocks.append((logits, finish))

        for r16 in range(d16):
            r, a = r16 % DIL_CLASSES, r16 // DIL_CLASSES
            for b in range(per16 // n16):
                def logits(r=r, a=a, b=b):
                    start, case = window(i * per16 + b * n16, seq // d16, win16)
                    src = pl.ds(a + DIL_CLASSES * start, win16, stride=DIL_CLASSES)
                    q = q4[r, pl.ds(a + DIL_CLASSES * b * n16, n16, stride=DIL_CLASSES), :].astype(BF16)
                    return _dot_nt(q, k4[r, src, :].astype(BF16)) + bias_b[case], src

                def finish(s, src, r=r, a=a, b=b):
                    dst = (r, pl.ds(a + DIL_CLASSES * b * n16, n16, stride=DIL_CLASSES), slice(None))
                    store((a4, m4, s4), dst, merge(softmax_block(s, v4[r, src, :].astype(BF16)),
                                                   load((a4, m4, s4), dst)))
                blocks.append((logits, finish))

        ahead = 6
        queue = [blocks[n][0]() for n in range(ahead)]
        for n in range(len(blocks)):
            if n + ahead < len(blocks):
                queue.append(blocks[n + ahead][0]())
            blocks[n][1](*queue.pop(0))

        for r in range(DIL_CLASSES):
            a1[pl.ds(r, per4, stride=DIL_CLASSES), :] = a4[r] / s4[r]
        y_ref[...] = a1[...].astype(BF16)

    pl.when(safe)(lambda: run(True))
    pl.when(jnp.logical_not(safe))(lambda: run(False))


def _dilated_attention(qkv, slopes_a, batch, seq):
    T = batch * seq
    tq, blk = 2048, 128
    assert tuple(d for _, d in DIL_BRANCHES) == (1, DIL_CLASSES, DIL_CLASSES ** 2)
    assert tq % (8 * DIL_CLASSES ** 2) == 0 and seq % tq == 0
    per16 = tq // DIL_CLASSES ** 2
    qkv_v = qkv.reshape(N_SLABS, batch, seq, LANES)
    kern = functools.partial(_dil_kernel, tq=tq, seq=seq, blk=blk)
    cls = lambda n: pltpu.VMEM((DIL_CLASSES, n // DIL_CLASSES, LANES), F32)
    tile = lambda n: pltpu.VMEM((n, LANES), F32)
    win = blk + 2 * DIL_RADIUS
    out = pl.pallas_call(
        kern,
        grid=(batch, DIL_HEADS, seq // tq),
        in_specs=[
            pl.BlockSpec(memory_space=pltpu.SMEM),
            pl.BlockSpec((None, None, tq, LANES), lambda b, h, i: (SLAB_AQ + h, b, i, 0)),
            pl.BlockSpec((None, None, seq, LANES), lambda b, h, i: (SLAB_AK + h, b, 0, 0)),
            pl.BlockSpec((None, None, seq, LANES), lambda b, h, i: (SLAB_AV + h, b, 0, 0)),
        ],
        out_specs=pl.BlockSpec((None, None, tq, LANES), lambda b, h, i: (h, b, i, 0)),
        out_shape=jax.ShapeDtypeStruct((DIL_HEADS, batch, seq, LANES), BF16),
        scratch_shapes=[tile(seq), cls(seq), cls(seq), tile(tq), cls(tq),
                        tile(tq), tile(tq), tile(tq), cls(tq), cls(tq), cls(tq),
                        tile(8),
                        pltpu.VMEM((2, DIL_OFFSETS, blk, win), F32),
                        pltpu.VMEM((DIL_OFFSETS, per16, min(per16 + 2 * DIL_RADIUS, seq // DIL_CLASSES ** 2)), F32)],
        compiler_params=_params(3),
        name="dilated_attention",
    )(slopes_a, qkv_v, qkv_v, qkv_v)
    return out.reshape(DIL_HEADS, T, LANES)


DIFF_FEAT = 3
DIFF_VT_ROWS = HEAD_DIM + 16
DIFF_SAFE_BOUND = 100.0


def _diff_kernel(slopes_ref, q_ref, k_ref, v_ref, lq1_ref, lk1_ref, lq2_ref, lk2_ref, sg_ref,
                 y_ref, ka_ref, vt_ref, qt_ref, kn_ref, *, tq, tk, seq, n_sub, lambda_init):
    h = pl.program_id(1)
    i = pl.program_id(2)
    slope2 = slopes_ref[h] * LOG2E
    n_kb = seq // tk
    half = DIFF_QK_DIM

    @pl.when(i == 0)
    def _():
        lane = lax.broadcasted_iota(jnp.int32, (tk, LANES), 1)
        k_local = lax.broadcasted_iota(jnp.int32, (tk, LANES), 0)
        rem = slope2 * k_local.astype(F32)
        parts = []
        for _ in range(DIFF_FEAT):
            part = rem.astype(BF16).astype(F32)
            parts.append(part)
            rem = rem - part
        ones_row = (lax.broadcasted_iota(jnp.int32, (DIFF_VT_ROWS - HEAD_DIM, tk), 0) == 0)
        k_sq = [jnp.zeros((1, 1), F32)] * 2
        for c in range(n_kb):
            kf = k_ref[c * tk:(c + 1) * tk, :].astype(F32)
            for mp in range(2):
                lo, fo = mp * half, half - mp * half
                ka = jnp.where((lane >= lo) & (lane < lo + half), kf, 0.0)
                k_sq[mp] = jnp.maximum(k_sq[mp], jnp.max(jnp.sum(ka * ka, axis=-1, keepdims=True),
                                                         axis=0, keepdims=True))
                for t, part in enumerate(parts):
                    ka = jnp.where(lane == fo + t, part, ka)
                ka_ref[mp, c * tk:(c + 1) * tk, :] = ka.astype(BF16)
            vt_ref[c, :HEAD_DIM, :] = v_ref[c * tk:(c + 1) * tk, :].astype(F32).T.astype(BF16)
            vt_ref[c, HEAD_DIM:, :] = ones_row.astype(BF16)
        for mp in range(2):
            kn_ref[mp] = jnp.broadcast_to(jnp.sqrt(k_sq[mp]), kn_ref.shape[1:])

    def prepare(t):
        qt = q_ref[t * tq:(t + 1) * tq, :].astype(F32).T
        row = lax.broadcasted_iota(jnp.int32, (LANES, tq), 0)
        bound = None
        for mp in range(2):
            lo, fo = mp * half, half - mp * half
            is_own = (row >= lo) & (row < lo + half)
            own = jnp.where(is_own, qt, 0.0)
            feat = (row >= fo) & (row < fo + DIFF_FEAT)
            qt_ref[t, 0, mp] = jnp.where(feat, 1.0, own).astype(BF16)
            qt_ref[t, 1, mp] = jnp.where(feat, -1.0, own).astype(BF16)
            qt_ref[t, 2, mp] = own.astype(BF16)
            q_norm = jnp.sqrt(jnp.sum(own * own, axis=0, keepdims=True))
            worst = jnp.max(q_norm * kn_ref[mp])
            bound = worst if bound is None else jnp.maximum(bound, worst)
        return (i * n_sub + t) * tq, bound

    q_local = lax.broadcasted_iota(jnp.int32, (1, tq), 1).astype(F32)

    def logits(t, q0, j, complete):
        kb_diag = q0 // tk
        kb = kb_diag + j
        kb = jnp.where(kb >= n_kb, kb - n_kb, kb)
        k0 = pl.multiple_of(kb * tk, tk)
        if j < max(1, tq // tk):
            krow = lax.broadcasted_iota(jnp.int32, (tk, tq), 0)
            qcol = lax.broadcasted_iota(jnp.int32, (tk, tq), 1)
            tile = slope2 * jnp.abs((qcol - krow) + (q0 - k0)).astype(F32)
            return kb, 0.0, [_dot(ka_ref[mp, pl.ds(k0, tk), :], qt_ref[t, 2, mp]) - tile for mp in range(2)]
        right = (kb > kb_diag).astype(jnp.int32)
        sgn = (1 - 2 * right).astype(F32)
        row_term = (-sgn * slope2) * ((q0 - k0).astype(F32) + q_local)
        s_maps = [_dot(ka_ref[mp, pl.ds(k0, tk), :], qt_ref[t, right, mp]) for mp in range(2)]
        if complete:
            return kb, 0.0, [s + row_term for s in s_maps]
        return kb, row_term, s_maps

    def stream(t, q0, consume, complete):
        ahead = 3
        queue = [logits(t, q0, j, complete) for j in range(min(ahead, n_kb))]
        for j in range(n_kb):
            cur = queue.pop(0)
            if j + ahead < n_kb:
                queue.append(logits(t, q0, j + ahead, complete))
            consume(*cur)

    def finish(t, acc0, acc1):
        lam = (jnp.exp(jnp.sum(lq1_ref[...] * lk1_ref[...], axis=-1, keepdims=True))
               - jnp.exp(jnp.sum(lq2_ref[...] * lk2_ref[...], axis=-1, keepdims=True))
               + lambda_init)
        o0 = acc0[:HEAD_DIM] / acc0[HEAD_DIM:HEAD_DIM + 1]
        o1 = acc1[:HEAD_DIM] / acc1[HEAD_DIM:HEAD_DIM + 1]
        o = (o0 - lam * o1).T
        y_ref[t * tq:(t + 1) * tq, :] = (_rms(o, sg_ref[...]) * (1.0 - lambda_init)).astype(BF16)

    tiles = [prepare(t) for t in range(n_sub)]
    bound = functools.reduce(jnp.maximum, [b for _, b in tiles])
    safe = bound <= DIFF_SAFE_BOUND

    @pl.when(safe)
    def _():
        for t, (q0, _) in enumerate(tiles):
            accs = [jnp.zeros((DIFF_VT_ROWS, tq), F32) for _ in range(2)]

            def consume(kb, row_term, s_maps, accs=accs):
                vt = vt_ref[kb]
                for mp, s in enumerate(s_maps):
                    accs[mp] = accs[mp] + _dot(vt, jnp.exp2(s).astype(BF16))

            stream(t, q0, consume, True)
            finish(t, *accs)

    @pl.when(jnp.logical_not(safe))
    def _():
        for t, (q0, _) in enumerate(tiles):
            state = []

            def consume(kb, row_term, s_maps, state=state):
                vt = vt_ref[kb]
                part = []
                for s in s_maps:
                    m = jnp.max(s, axis=0, keepdims=True)
                    part.append((m + row_term, _dot(vt, jnp.exp2(s - m).astype(BF16))))
                if not state:
                    state.extend(part)
                    return
                for mp, ((m, acc), (m_b, acc_b)) in enumerate(zip(list(state), part)):
                    m_new = jnp.maximum(m, m_b)
                    state[mp] = (m_new, jnp.exp2(m - m_new) * acc + jnp.exp2(m_b - m_new) * acc_b)

            stream(t, q0, consume, False)
            finish(t, state[0][1], state[1][1])


def _diff_call(qkv, slopes_b, lq1, lk1, lq2, lk2, subln, layer, lambda_init, batch, seq):
    T = batch * seq
    tq, tk = 512, 256
    assert (tk % tq == 0 or tq % tk == 0) and seq % tk == 0
    qkv_v = qkv.reshape(N_SLABS, batch, seq, LANES)
    vec = lambda a: a.reshape(a.shape[0], 1, a.shape[1])
    lam_spec = pl.BlockSpec((None, 1, DIFF_QK_DIM), lambda b, h, i: (layer, 0, 0))
    n_sub = 2
    assert seq % (tq * n_sub) == 0
    kern = functools.partial(_diff_kernel, tq=tq, tk=tk, seq=seq, n_sub=n_sub, lambda_init=lambda_init)
    out = pl.pallas_call(
        kern,
        grid=(batch, DIFF_HEADS, seq // (tq * n_sub)),
        in_specs=[
            pl.BlockSpec(memory_space=pltpu.SMEM),
            pl.BlockSpec((None, None, tq * n_sub, LANES), lambda b, h, i: (SLAB_BQ + h, b, i, 0)),
            pl.BlockSpec((None, None, seq, LANES), lambda b, h, i: (SLAB_BK + h, b, 0, 0)),
            pl.BlockSpec((None, None, seq, LANES), lambda b, h, i: (SLAB_BV + h, b, 0, 0)),
            lam_spec, lam_spec, lam_spec, lam_spec,
            pl.BlockSpec((None, 1, HEAD_DIM), lambda b, h, i: (layer, 0, 0)),
        ],
        out_specs=pl.BlockSpec((None, None, tq * n_sub, LANES), lambda b, h, i: (h, b, i, 0)),
        out_shape=jax.ShapeDtypeStruct((DIFF_HEADS, batch, seq, LANES), BF16),
        scratch_shapes=[pltpu.VMEM((2, seq, LANES), BF16), pltpu.VMEM((seq // tk, DIFF_VT_ROWS, tk), BF16),
                        pltpu.VMEM((n_sub, 3, 2, LANES, tq), BF16), pltpu.VMEM((2, 1, tq), F32)],
        compiler_params=_params(3),
        name="differential_attention",
    )(slopes_b, qkv_v, qkv_v, qkv_v, vec(lq1), vec(lk1), vec(lq2), vec(lk2), vec(subln))
    return out.reshape(DIFF_HEADS, T, LANES)


def _swa_kernel(slopes_ref, sink_ref, q_ref, k_ref, v_ref, w_ref, y_ref, wb_ref, bias_ref, *, tq, win, seq,
                layer, tiles_per_iter):
    g = pl.program_id(1)
    wb_ref[...] = w_ref[...].astype(BF16)
    heads = [g * SWA_GROUP + r for r in range(SWA_GROUP)]
    sinks2 = [sink_ref[layer * SWA_Q_HEADS + hd] * LOG2E for hd in heads]

    row = lax.broadcasted_iota(jnp.int32, (tq, win), 0)
    col = lax.broadcasted_iota(jnp.int32, (tq, win), 1)
    for case in range(SWA_OFFSETS):
        absrel = jnp.abs(col - row - case * SWA_RADIUS)
        for r, hd in enumerate(heads):
            slope2 = slopes_ref[hd] * LOG2E
            bias_ref[r, case] = jnp.where(absrel <= SWA_RADIUS, -slope2 * absrel.astype(F32), NEG)

    def tiles(it, carry):
        work = []
        for t in range(tiles_per_iter):
            q0 = pl.multiple_of((it * tiles_per_iter + t) * tq, tq)
            start = pl.multiple_of(jnp.clip(q0 - SWA_RADIUS, 0, seq - win), SWA_RADIUS)
            case = lax.div(q0 - start, SWA_RADIUS)
            k = k_ref[pl.ds(start, win), :]
            for r in range(SWA_GROUP):
                work.append((r, q0, start, _dot_nt(q_ref[r, pl.ds(q0, tq), :], k) + bias_ref[r, case]))
        for r, q0, start, s in work:
            m = jnp.maximum(jnp.max(s, axis=-1, keepdims=True), sinks2[r])
            p = jnp.exp2(s - m)
            den = jnp.sum(p, axis=-1, keepdims=True) + jnp.exp2(sinks2[r] - m)
            o = _dot(p.astype(BF16), v_ref[pl.ds(start, win), :]) / den
            y_ref[r, pl.ds(q0, tq), :] = o.astype(BF16)
        return carry

    lax.fori_loop(0, seq // (tq * tiles_per_iter), tiles, 0)


def _swa_call(qkv, slopes_c, sink_flat, w_out, layer, batch, seq):
    T = batch * seq
    n_steps = batch * SWA_KV_HEADS
    w_rows = w_out.shape[1] // n_steps
    tq = 256
    win = tq + 2 * SWA_RADIUS
    qkv_v = qkv.reshape(N_SLABS, batch, seq, LANES)
    tiles_per_iter = 2
    assert seq % (tq * tiles_per_iter) == 0
    kern = functools.partial(_swa_kernel, tq=tq, win=win, seq=seq, layer=layer, tiles_per_iter=tiles_per_iter)
    out = pl.pallas_call(
        kern,
        grid=(batch, SWA_KV_HEADS),
        in_specs=[
            pl.BlockSpec(memory_space=pltpu.SMEM),
            pl.BlockSpec(memory_space=pltpu.SMEM),
            pl.BlockSpec((SWA_GROUP, None, seq, LANES), lambda b, g: (SLAB_CQ // SWA_GROUP + g, b, 0, 0)),
            pl.BlockSpec((None, None, seq, LANES), lambda b, g: (SLAB_CK + g, b, 0, 0)),
            pl.BlockSpec((None, None, seq, LANES), lambda b, g: (SLAB_CV + g, b, 0, 0)),
            pl.BlockSpec((None, w_rows, w_out.shape[2]), lambda b, g: (layer, b * SWA_KV_HEADS + g, 0)),
        ],
        out_specs=[pl.BlockSpec((SWA_GROUP, None, seq, LANES), lambda b, g: (g, b, 0, 0)),
                   pl.BlockSpec((w_rows, w_out.shape[2]), lambda b, g: (b * SWA_KV_HEADS + g, 0))],
        out_shape=[jax.ShapeDtypeStruct((SWA_Q_HEADS, batch, seq, LANES), BF16),
                   jax.ShapeDtypeStruct(w_out.shape[1:], BF16)],
        scratch_shapes=[pltpu.VMEM((SWA_GROUP, SWA_OFFSETS, tq, win), F32)],
        compiler_params=_params(2),
        name="window_gqa_attention",
    )(slopes_c, sink_flat, qkv_v, qkv_v, qkv_v, w_out)
    return out[0].reshape(SWA_Q_HEADS, T, LANES), out[1]


def _outproj_kernel(ya_ref, yb_ref, yc_ref, x_ref, wb_ref, gn_ref, o_ref, hn_ref):
    slabs = [y_ref[s] for y_ref in (ya_ref, yb_ref, yc_ref) for s in range(y_ref.shape[0])]
    y = jnp.concatenate(slabs, axis=-1)
    xn = x_ref[...] + _dot(y, wb_ref[...])
    o_ref[...] = xn
    hn_ref[...] = _rms(xn, gn_ref[...]).astype(BF16)


def _outproj_call(ya, yb, yc, x, w_out_bf16, gains, gain_layer):
    T, D = x.shape
    tm = 512
    row_spec = pl.BlockSpec((tm, D), lambda i: (i, 0))
    slab_spec = lambda y: pl.BlockSpec((y.shape[0], tm, LANES), lambda i: (0, i, 0))
    return pl.pallas_call(
        _outproj_kernel,
        grid=(T // tm,),
        in_specs=[
            slab_spec(ya), slab_spec(yb), slab_spec(yc),
            row_spec,
            pl.BlockSpec(w_out_bf16.shape, lambda i: (0, 0), pipeline_mode=pl.Buffered(1)),
            pl.BlockSpec((None, 1, D), lambda i: (gain_layer, 0, 0)),
        ],
        out_specs=[row_spec, row_spec],
        out_shape=[jax.ShapeDtypeStruct((T, D), F32), jax.ShapeDtypeStruct((T, D), BF16)],
        compiler_params=_params(1),
        name="output_projection",
    )(ya, yb, yc, x, w_out_bf16, gains)


def kernel(x, ffn1_norm, ffn1_w_gate, ffn1_w_up, ffn1_w_down, mix_norm, w_in, w_out, diff_lambda_q1, diff_lambda_k1, diff_lambda_q2, diff_lambda_k2, diff_subln, swa_sink, ffn2_norm, ffn2_w_gate, ffn2_w_up, ffn2_w_down, final_norm):
    batch, seq, d_model = x.shape
    depth = w_in.shape[0]
    xf = x.reshape(batch * seq, d_model)
    gain3 = lambda a: a.reshape(a.shape[0], 1, a.shape[1])
    ffn1_g, mix_g, ffn2_g = gain3(ffn1_norm), gain3(mix_norm), gain3(ffn2_norm)
    final_g = final_norm.reshape(1, 1, d_model)
    slopes_a = jnp.asarray(SLOPES_A, F32)
    slopes_b = jnp.asarray(SLOPES_B, F32)
    slopes_c = jnp.asarray(SLOPES_C, F32)
    sink_flat = swa_sink.astype(F32).reshape(-1)

    h = _norm_call(xf, ffn1_g, 0)
    for l in range(depth):
        lambda_init = 0.8 - 0.6 * math.exp(-0.3 * l)
        xf, h = _ffn_call(h, xf, ffn1_w_gate, ffn1_w_up, ffn1_w_down, l, mix_g, l, final=False)
        qkv = _inproj_call(h, w_in, l)
        ya = _dilated_attention(qkv, slopes_a, batch, seq)
        yb = _diff_call(qkv, slopes_b, diff_lambda_q1, diff_lambda_k1, diff_lambda_q2, diff_lambda_k2,
                        diff_subln, l, lambda_init, batch, seq)
        yc, w_out_bf16 = _swa_call(qkv, slopes_c, sink_flat, w_out, l, batch, seq)
        xf, h = _outproj_call(ya, yb, yc, xf, w_out_bf16, ffn2_g, l)
        if l + 1 < depth:
            xf, h = _ffn_call(h, xf, ffn2_w_gate, ffn2_w_up, ffn2_w_down, l, ffn1_g, l + 1, final=False)
        else:
            xf = _ffn_call(h, xf, ffn2_w_gate, ffn2_w_up, ffn2_w_down, l, final_g, 0, final=True)
    return xf.reshape(batch, seq, d_model)
```
